```python
import jax, jax.numpy as jnp
from jax import lax
import numpy as np

D_MODEL = 2048
BATCH = 4
SEQ = 2048
DEPTH = 2

CHUNK = 64
RET_HEADS = 8
RET_HEAD_DIM = D_MODEL // 16
RET_WIDTH = RET_HEADS * RET_HEAD_DIM
LRU_WIDTH = D_MODEL // 2
LRU_GROUPS = 16
LRU_GROUP_DIM = LRU_WIDTH // LRU_GROUPS
CONV_W = 4
LRU_C = 8.0
MIX_WIDTH = RET_WIDTH + LRU_WIDTH
IN_WIDTH = 4 * RET_WIDTH + 2 * LRU_WIDTH
D_FF = ((8 * D_MODEL // 3 + 255) // 256) * 256
ROPE_BASE = 10000.0
EPS = 1e-6

kernel_name = "hybrid_retention_rglru_swiglu"


def _rmsnorm(x, g):
    xf = x.astype(jnp.float32)
    y = xf * lax.rsqrt(jnp.mean(xf * xf, axis=-1, keepdims=True) + EPS)
    return (y * g.astype(jnp.float32)).astype(x.dtype)


def _rotary(t, pos):
    dk = t.shape[-1]
    inv = 1.0 / (ROPE_BASE ** (jnp.arange(0, dk, 2, dtype=jnp.float32) / dk))
    ang = pos[:, None] * inv[None, :]
    cos = jnp.cos(ang)[None, :, None, :]
    sin = jnp.sin(ang)[None, :, None, :]
    t1, t2 = t[..., : dk // 2], t[..., dk // 2:]
    return jnp.concatenate([t1 * cos - t2 * sin, t1 * sin + t2 * cos], axis=-1)


def _retention(q, k, v, g, gn_g):
    B, T, _ = q.shape
    H, dk, C = RET_HEADS, RET_HEAD_DIM, CHUNK
    nc = T // C
    pos = jnp.arange(T, dtype=jnp.float32)
    q = _rotary(q.reshape(B, T, H, dk), pos)
    k = _rotary(k.reshape(B, T, H, dk), pos) * (dk ** -0.5)
    v = v.reshape(B, T, H, dk)
    qc = q.reshape(B, nc, C, H, dk)
    kc = k.reshape(B, nc, C, H, dk)
    vc = v.reshape(B, nc, C, H, dk)

    log_g = jnp.log1p(-jnp.exp2(-5.0 - jnp.arange(H, dtype=jnp.float32)))
    idx = jnp.arange(C, dtype=jnp.float32)
    dist = jnp.abs(idx[:, None] - idx[None, :])
    d_intra = jnp.exp(log_g[:, None, None] * dist)

    scores = jnp.einsum('bnahd,bnchd->bnhac', qc, kc) * d_intra
    o_intra = jnp.einsum('bnhac,bnche->bnahe', scores, vc)

    k_dec = jnp.exp(log_g[:, None] * (C - 1.0 - idx)[None, :])
    u = jnp.einsum('bnchd,hc,bnche->nbhde', kc, k_dec, vc)
    chunk_dec = jnp.exp(log_g * C)[None, :, None, None]

    def step(s, u_j):
        return chunk_dec * s + u_j, s

    _, s_in = lax.scan(step, jnp.zeros(u.shape[1:], u.dtype), u)
    q_dec = jnp.exp(log_g[:, None] * (idx + 1.0)[None, :])
    o_cross = jnp.einsum('bnahd,ha,nbhde->bnahe', qc, q_dec, s_in)

    o = (o_intra + o_cross).reshape(B, T, H, dk)
    mu = jnp.mean(o, axis=-1, keepdims=True)
    var = jnp.mean(jnp.square(o - mu), axis=-1, keepdims=True)
    on = ((o - mu) * lax.rsqrt(var + EPS)).reshape(B, T, RET_WIDTH) * gn_g
    return on * jax.nn.silu(g)


def _rg_lru_branch(xb, yb, conv_w, conv_b, wa, ba, wx, bx, lam, norm_g):
    B, T, W = xb.shape
    xp = jnp.pad(xb, ((0, 0), (CONV_W - 1, 0), (0, 0)))
    xc = conv_b + sum(xp[:, j:j + T] * conv_w[j] for j in range(CONV_W))
    xg = xc.reshape(B, T, LRU_GROUPS, LRU_GROUP_DIM)
    r = jax.nn.sigmoid(jnp.einsum('btgi,gij->btgj', xg, wa).reshape(B, T, W) + ba)
    i = jax.nn.sigmoid(jnp.einsum('btgi,gij->btgj', xg, wx).reshape(B, T, W) + bx)
    log_a = -LRU_C * r * jax.nn.softplus(-lam)
    a = jnp.exp(log_a)
    b = jnp.sqrt(-jnp.expm1(2.0 * log_a)) * (i * xc)

    def comb(left, right):
        a1, b1 = left
        a2, b2 = right
        return a1 * a2, a2 * b1 + b2

    _, h = lax.associative_scan(comb, (a, b), axis=1)
    y = h * jax.nn.gelu(yb)
    y = y * lax.rsqrt(jnp.mean(y * y, axis=-1, keepdims=True) + EPS)
    return y * norm_g


def setup_inputs(seed: int = 0) -> dict:
    key = jax.random.key(seed)
    ks = jax.random.split(key, 20)
    f32 = jnp.float32

    def nrm(k, shape, scale):
        return jax.random.normal(k, shape, f32) * scale

    def gain(k, shape):
        return 1.0 + 0.02 * jax.random.normal(k, shape, f32)

    u = jax.random.uniform(ks[13], (DEPTH, LRU_WIDTH), f32, 0.9, 0.999)
    a0 = u ** (1.0 / LRU_C)
    lam = jnp.log(a0) - jnp.log1p(-a0)
    return {
        "x": nrm(ks[0], (BATCH, SEQ, D_MODEL), 1.0),
        "norm1_g": gain(ks[1], (DEPTH, D_MODEL)),
        "w_in": nrm(ks[2], (DEPTH, D_MODEL, IN_WIDTH), D_MODEL ** -0.5),
        "ret_gn_g": gain(ks[3], (DEPTH, RET_WIDTH)),
        "lru_conv_w": nrm(ks[4], (DEPTH, CONV_W, LRU_WIDTH), CONV_W ** -0.5),
        "lru_conv_b": nrm(ks[5], (DEPTH, LRU_WIDTH), 0.01),
        "lru_wa": nrm(ks[6], (DEPTH, LRU_GROUPS, LRU_GROUP_DIM, LRU_GROUP_DIM), LRU_GROUP_DIM ** -0.5),
        "lru_ba": nrm(ks[7], (DEPTH, LRU_WIDTH), 0.01),
        "lru_wx": nrm(ks[8], (DEPTH, LRU_GROUPS, LRU_GROUP_DIM, LRU_GROUP_DIM), LRU_GROUP_DIM ** -0.5),
        "lru_bx": nrm(ks[9], (DEPTH, LRU_WIDTH), 0.01),
        "lru_lambda": lam,
        "lru_norm_g": gain(ks[10], (DEPTH, LRU_WIDTH)),
        "w_out": nrm(ks[11], (DEPTH, MIX_WIDTH, D_MODEL), MIX_WIDTH ** -0.5),
        "norm2_g": gain(ks[12], (DEPTH, D_MODEL)),
        "ffn_w_gate": nrm(ks[14], (DEPTH, D_MODEL, D_FF), D_MODEL ** -0.5),
        "ffn_w_up": nrm(ks[15], (DEPTH, D_MODEL, D_FF), D_MODEL ** -0.5),
        "ffn_w_down": nrm(ks[16], (DEPTH, D_FF, D_MODEL), D_FF ** -0.5),
        "final_g": gain(ks[17], (D_MODEL,)),
    }


def reference(x, norm1_g, w_in, ret_gn_g, lru_conv_w, lru_conv_b, lru_wa, lru_ba,
              lru_wx, lru_bx, lru_lambda, lru_norm_g, w_out, norm2_g,
              ffn_w_gate, ffn_w_up, ffn_w_down, final_g):
    f32 = jnp.float32
    R, L = RET_WIDTH, LRU_WIDTH
    for l in range(DEPTH):
        h = _rmsnorm(x, norm1_g[l])
        p = (h @ w_in[l]).astype(f32)
        q, k, v, g, xb, yb = jnp.split(p, [R, 2 * R, 3 * R, 4 * R, 4 * R + L], axis=-1)
        o_ret = _retention(q, k, v, g, ret_gn_g[l].astype(f32))
        o_lru = _rg_lru_branch(xb, yb, lru_conv_w[l].astype(f32), lru_conv_b[l].astype(f32),
                               lru_wa[l].astype(f32), lru_ba[l].astype(f32),
                               lru_wx[l].astype(f32), lru_bx[l].astype(f32),
                               lru_lambda[l].astype(f32), lru_norm_g[l].astype(f32))
        mix = jnp.concatenate([o_ret, o_lru], axis=-1).astype(x.dtype)
        x = x + mix @ w_out[l]
        h = _rmsnorm(x, norm2_g[l])
        x = x + (jax.nn.silu(h @ ffn_w_gate[l]) * (h @ ffn_w_up[l])) @ ffn_w_down[l]
    return _rmsnorm(x, final_g)
```

```python
import functools

import numpy as np
import jax
import jax.numpy as jnp
from jax import lax
from jax.experimental import pallas as pl
from jax.experimental.pallas import tpu as pltpu

D_MODEL = 2048
SEQ = 2048
RET_HEADS = 8
HEAD_DIM = 128
RET_WIDTH = RET_HEADS * HEAD_DIM
LRU_WIDTH = 1024
LRU_GROUPS = 16
LRU_GROUP_DIM = 64
CONV_W = 4
LRU_C = 8.0
IN_WIDTH = 4 * RET_WIDTH + 2 * LRU_WIDTH
CHUNK = 64
ROPE_BASE = 10000.0
EPS = 1e-6

F32 = jnp.float32
BF16 = jnp.bfloat16

RET_BLOCK = 256
LRU_BLOCK = 256
GATE_BLOCK = 256
SUBLANES = 8
VMEM_LIMIT = 56 * 1024 * 1024


def _params(*sem):
    return pltpu.CompilerParams(dimension_semantics=sem, vmem_limit_bytes=VMEM_LIMIT)


def _rms_scale(x, g):
    ms = jnp.mean(x * x, axis=-1, keepdims=True)
    return x * lax.rsqrt(ms + EPS) * g


def _in_proj_kernel(x_ref, g_ref, w_ref, o_ref, h_ref):
    @pl.when(pl.program_id(1) == 0)
    def _():
        h_ref[...] = _rms_scale(x_ref[...], g_ref[...]).astype(BF16)

    o_ref[...] = jnp.dot(h_ref[...], w_ref[...],
                         preferred_element_type=F32).astype(o_ref.dtype)


def _in_proj(x, g, w, tm=1024, tn=1024):
    m, d = x.shape
    n = w.shape[1]
    return pl.pallas_call(
        _in_proj_kernel,
        grid=(m // tm, n // tn),
        in_specs=[pl.BlockSpec((tm, d), lambda i, j: (i, 0)),
                  pl.BlockSpec((1, d), lambda i, j: (0, 0)),
                  pl.BlockSpec((d, tn), lambda i, j: (0, j))],
        out_specs=pl.BlockSpec((tm, tn), lambda i, j: (i, j)),
        out_shape=jax.ShapeDtypeStruct((m, n), BF16),
        scratch_shapes=[pltpu.VMEM((tm, d), BF16)],
        compiler_params=_params("parallel", "arbitrary"),
        name="in_proj",
    )(x, g, w)


def _retention_kernel(q_ref, k_ref, v_ref, g_ref, cq_ref, sq_ref, ck_ref, sk_ref,
                      dmask_ref, qdec_ref, kdec_ref, gn_ref, o_ref, s_ref, *, sdec):
    @pl.when(pl.program_id(1) == 0)
    def _():
        s_ref[...] = jnp.zeros_like(s_ref)

    cq, sq, ck, sk = cq_ref[...], sq_ref[...], ck_ref[...], sk_ref[...]
    half = HEAD_DIM // 2
    for h in range(RET_HEADS):
        sl = slice(h * HEAD_DIM, (h + 1) * HEAD_DIM)
        q = q_ref[:, sl].astype(F32)
        k = k_ref[:, sl].astype(F32)
        v = v_ref[:, sl]
        qr = q * cq + pltpu.roll(q, half, 1) * sq
        kr = k * ck + pltpu.roll(k, half, 1) * sk
        qb = qr.astype(BF16)
        kb = kr.astype(BF16)
        scores = lax.dot_general(qb, kb, (((1,), (1,)), ((), ())),
                                 preferred_element_type=F32) * dmask_ref[h]
        o = jnp.dot(scores.astype(BF16), v, preferred_element_type=F32)
        state = s_ref[h]
        o = o + qdec_ref[h] * jnp.dot(qb, state.astype(BF16), preferred_element_type=F32)
        kd = (kr * kdec_ref[h]).astype(BF16)
        s_ref[h] = sdec[h] * state + lax.dot_general(
            kd, v, (((0,), (0,)), ((), ())), preferred_element_type=F32)
        mu = jnp.mean(o, axis=-1, keepdims=True)
        oc = o - mu
        var = jnp.mean(oc * oc, axis=-1, keepdims=True)
        on = oc * lax.rsqrt(var + EPS) * gn_ref[:, sl]
        gate = g_ref[:, sl].astype(F32)
        o_ref[:, sl] = (on * (gate * jax.nn.sigmoid(gate))).astype(o_ref.dtype)


def _retention_tables():
    cs = RET_BLOCK
    pos = np.arange(SEQ, dtype=np.float32)
    inv = (1.0 / (ROPE_BASE ** (np.arange(0, HEAD_DIM, 2, dtype=np.float32) / HEAD_DIM))).astype(np.float32)
    ang = jnp.asarray(pos[:, None] * inv[None, :])
    cos, sin = jnp.cos(ang), jnp.sin(ang)
    cfull = jnp.concatenate([cos, cos], axis=-1)
    ssign = jnp.concatenate([-sin, sin], axis=-1)
    kscale = HEAD_DIM ** -0.5
    gamma_log = np.log1p(-np.exp2(-5.0 - np.arange(RET_HEADS, dtype=np.float64)))
    idx = np.arange(cs)
    dist = np.abs(idx[:, None] - idx[None, :])
    visible = (idx[None, :] // CHUNK) <= (idx[:, None] // CHUNK)
    dmask = np.where(visible[None], np.exp(gamma_log[:, None, None] * dist[None]), 0.0)
    qdec = np.exp(gamma_log[:, None] * (idx + 1.0)[None, :])
    kdec = np.exp(gamma_log[:, None] * (cs - 1.0 - idx)[None, :])
    sdec = tuple(float(s) for s in np.exp(gamma_log * cs))
    bcast = lambda a: jnp.asarray(np.broadcast_to(a[:, :, None], (RET_HEADS, cs, HEAD_DIM)), F32)
    return (cfull, ssign, cfull * kscale, ssign * kscale,
            jnp.asarray(dmask, F32), bcast(qdec), bcast(kdec), sdec)


def _retention(p, gn_g, tables, batch):
    cfull, ssign, ckf, skf, dmask, qdec, kdec, sdec = tables
    m = p.shape[0]
    cs = RET_BLOCK
    nt = SEQ // cs
    row = lambda b, t: b * nt + t
    pspec = lambda c: pl.BlockSpec((cs, RET_WIDTH), lambda b, t, c=c: (row(b, t), c))
    tspec = pl.BlockSpec((cs, HEAD_DIM), lambda b, t: (t, 0))
    full3 = lambda a: pl.BlockSpec(a.shape, lambda b, t: (0, 0, 0))
    return pl.pallas_call(
        functools.partial(_retention_kernel, sdec=sdec),
        grid=(batch, nt),
        in_specs=[pspec(0), pspec(1), pspec(2), pspec(3),
                  tspec, tspec, tspec, tspec,
                  full3(dmask), full3(qdec), full3(kdec),
                  pl.BlockSpec((1, RET_WIDTH), lambda b, t: (0, 0))],
        out_specs=pl.BlockSpec((cs, RET_WIDTH), lambda b, t: (row(b, t), 0)),
        out_shape=jax.ShapeDtypeStruct((m, RET_WIDTH), BF16),
        scratch_shapes=[pltpu.VMEM((RET_HEADS, HEAD_DIM, HEAD_DIM), F32)],
        compiler_params=_params("parallel", "arbitrary"),
        name="retention",
    )(p, p, p, p, cfull, ssign, ckf, skf, dmask, qdec, kdec, gn_g)


def _lru_kernel(xb_ref, yb_ref, cw_ref, cb_ref, wa_ref, ba_ref, wx_ref, bx_ref,
                lam_ref, ng_ref, o_ref, xpad_ref, a_ref, b_ref, h_ref, carry_ref):
    tt = xb_ref.shape[0]
    pad = SUBLANES

    @pl.when(pl.program_id(1) == 0)
    def _():
        xpad_ref[0:pad, :] = jnp.zeros((pad, LRU_WIDTH), F32)
        carry_ref[...] = jnp.zeros_like(carry_ref)

    x = xb_ref[...].astype(F32)
    xpad_ref[pad:pad + tt, :] = x
    xc = cb_ref[...] + cw_ref[CONV_W - 1:CONV_W, :] * x
    for j in range(CONV_W - 1):
        off = pad - (CONV_W - 1) + j
        xc = xc + cw_ref[j:j + 1, :] * xpad_ref[off:off + tt, :]
    xpad_ref[0:pad, :] = x[tt - pad:tt, :]

    softplus_neg_lam = (jnp.maximum(-lam_ref[...], 0.0)
                        + jnp.log1p(jnp.exp(-jnp.abs(lam_ref[...]))))
    for c in range(LRU_WIDTH // GATE_BLOCK):
        sl = slice(c * GATE_BLOCK, (c + 1) * GATE_BLOCK)
        xcb = xc[:, sl]
        xcb16 = xcb.astype(BF16)
        r = jax.nn.sigmoid(jnp.dot(xcb16, wa_ref[c], preferred_element_type=F32) + ba_ref[:, sl])
        i = jax.nn.sigmoid(jnp.dot(xcb16, wx_ref[c], preferred_element_type=F32) + bx_ref[:, sl])
        log_a = -LRU_C * r * softplus_neg_lam[:, sl]
        a = jnp.exp(log_a)
        a_ref[:, sl] = a
        b_ref[:, sl] = jnp.sqrt(-jnp.tanh(log_a) * (a * a + 1.0)) * (i * xcb)

    row = lax.broadcasted_iota(jnp.int32, (SUBLANES, LRU_WIDTH), 0)

    def group(gidx, carry):
        start = pl.multiple_of(gidx * SUBLANES, SUBLANES)
        a = a_ref[pl.ds(start, SUBLANES), :]
        b = b_ref[pl.ds(start, SUBLANES), :]
        for s in (1, 2, 4):
            keep = row >= s
            b = jnp.where(keep, a * pltpu.roll(b, s, 0) + b, b)
            a = jnp.where(keep, a * pltpu.roll(a, s, 0), a)
        h = a * carry + b
        h_ref[pl.ds(start, SUBLANES), :] = h
        return h[SUBLANES - 1:SUBLANES, :]

    carry_ref[...] = lax.fori_loop(0, tt // SUBLANES, group, carry_ref[...], unroll=4)

    y = h_ref[...] * jax.nn.gelu(yb_ref[...].astype(F32))
    o_ref[...] = _rms_scale(y, ng_ref[...]).astype(o_ref.dtype)


def _block_diag(w):
    per = GATE_BLOCK // LRU_GROUP_DIM
    nb = LRU_GROUPS // per
    w4 = w.reshape(nb, per, LRU_GROUP_DIM, LRU_GROUP_DIM)
    bd = jnp.einsum('cipq,ij->cipjq', w4, jnp.eye(per, dtype=w.dtype))
    return bd.reshape(nb, GATE_BLOCK, GATE_BLOCK).astype(BF16)


def _lru(p, cw, cb, wa, ba, wx, bx, lam, ng, batch):
    m = p.shape[0]
    tt = LRU_BLOCK
    nt = SEQ // tt
    row = lambda b, t: b * nt + t
    xcol = 4 * RET_WIDTH // LRU_WIDTH
    vec = pl.BlockSpec((1, LRU_WIDTH), lambda b, t: (0, 0))
    wspec = pl.BlockSpec(wa.shape, lambda b, t: (0, 0, 0))
    return pl.pallas_call(
        _lru_kernel,
        grid=(batch, nt),
        in_specs=[pl.BlockSpec((tt, LRU_WIDTH), lambda b, t: (row(b, t), xcol)),
                  pl.BlockSpec((tt, LRU_WIDTH), lambda b, t: (row(b, t), xcol + 1)),
                  pl.BlockSpec((CONV_W, LRU_WIDTH), lambda b, t: (0, 0)),
                  vec, wspec, vec, wspec, vec, vec, vec],
        out_specs=pl.BlockSpec((tt, LRU_WIDTH), lambda b, t: (row(b, t), 0)),
        out_shape=jax.ShapeDtypeStruct((m, LRU_WIDTH), BF16),
        scratch_shapes=[pltpu.VMEM((tt + SUBLANES, LRU_WIDTH), F32),
                        pltpu.VMEM((tt, LRU_WIDTH), F32),
                        pltpu.VMEM((tt, LRU_WIDTH), F32),
                        pltpu.VMEM((tt, LRU_WIDTH), F32),
                        pltpu.VMEM((1, LRU_WIDTH), F32)],
        compiler_params=_params("parallel", "arbitrary"),
        name="rg_lru",
    )(p, p, cw, cb, wa, ba, wx, bx, lam, ng)


def _out_proj_kernel(x_ref, r_ref, l_ref, w_ref, o_ref):
    acc = jnp.dot(r_ref[...], w_ref[0:RET_WIDTH, :], preferred_element_type=F32)
    acc = acc + jnp.dot(l_ref[...], w_ref[RET_WIDTH:, :], preferred_element_type=F32)
    o_ref[...] = x_ref[...] + acc


def _out_proj(x, o_ret, o_lru, w, tm=1024, tn=1024):
    m, d = x.shape
    return pl.pallas_call(
        _out_proj_kernel,
        grid=(m // tm, d // tn),
        in_specs=[pl.BlockSpec((tm, tn), lambda i, j: (i, j)),
                  pl.BlockSpec((tm, RET_WIDTH), lambda i, j: (i, 0)),
                  pl.BlockSpec((tm, LRU_WIDTH), lambda i, j: (i, 0)),
                  pl.BlockSpec((w.shape[0], tn), lambda i, j: (0, j))],
        out_specs=pl.BlockSpec((tm, tn), lambda i, j: (i, j)),
        out_shape=jax.ShapeDtypeStruct((m, d), F32),
        compiler_params=_params("parallel", "arbitrary"),
        name="out_proj",
    )(x, o_ret, o_lru, w)


def _ffn_kernel(x_ref, g_ref, wg_ref, wu_ref, wd_ref, fg_ref, o_ref, h_ref, *, final_norm):
    j = pl.program_id(1)

    @pl.when(j == 0)
    def _():
        x = x_ref[...]
        h_ref[...] = _rms_scale(x, g_ref[...]).astype(BF16)
        o_ref[...] = x

    h = h_ref[...]
    gate = jnp.dot(h, wg_ref[...], preferred_element_type=F32)
    up = jnp.dot(h, wu_ref[...], preferred_element_type=F32)
    mid = (gate * jax.nn.sigmoid(gate) * up).astype(BF16)
    o_ref[...] += jnp.dot(mid, wd_ref[...], preferred_element_type=F32)

    if final_norm:
        @pl.when(j == pl.num_programs(1) - 1)
        def _():
            o_ref[...] = _rms_scale(o_ref[...], fg_ref[...])


def _ffn(x, g, wg, wu, wd, fg, final_norm, tm=512, tf=512):
    m, d = x.shape
    f = wg.shape[1]
    return pl.pallas_call(
        functools.partial(_ffn_kernel, final_norm=final_norm),
        grid=(m // tm, f // tf),
        in_specs=[pl.BlockSpec((tm, d), lambda i, j: (i, 0)),
                  pl.BlockSpec((1, d), lambda i, j: (0, 0)),
                  pl.BlockSpec((d, tf), lambda i, j: (0, j)),
                  pl.BlockSpec((d, tf), lambda i, j: (0, j)),
                  pl.BlockSpec((tf, d), lambda i, j: (j, 0)),
                  pl.BlockSpec((1, d), lambda i, j: (0, 0))],
        out_specs=pl.BlockSpec((tm, d), lambda i, j: (i, 0)),
        out_shape=jax.ShapeDtypeStruct((m, d), F32),
        scratch_shapes=[pltpu.VMEM((tm, d), BF16)],
        compiler_params=_params("parallel", "arbitrary"),
        name="ffn_final" if final_norm else "ffn",
    )(x, g, wg, wu, wd, fg)


def kernel(x, norm1_g, w_in, ret_gn_g, lru_conv_w, lru_conv_b, lru_wa, lru_ba, lru_wx, lru_bx,
           lru_lambda, lru_norm_g, w_out, norm2_g, ffn_w_gate, ffn_w_up, ffn_w_down, final_g):
    batch, seq, d = x.shape
    depth = w_in.shape[0]
    assert (seq, d) == (SEQ, D_MODEL)
    tables = _retention_tables()
    row = lambda a: a.reshape(1, -1).astype(F32)
    xf = x.reshape(batch * seq, d)
    for l in range(depth):
        p = _in_proj(xf, row(norm1_g[l]), w_in[l].astype(BF16))
        o_ret = _retention(p, row(ret_gn_g[l]), tables, batch)
        o_lru = _lru(p, lru_conv_w[l].astype(F32), row(lru_conv_b[l]),
                     _block_diag(lru_wa[l]), row(lru_ba[l]),
                     _block_diag(lru_wx[l]), row(lru_bx[l]),
                     row(lru_lambda[l]), row(lru_norm_g[l]), batch)
        xf = _out_proj(xf, o_ret, o_lru, w_out[l].astype(BF16))
        xf = _ffn(xf, row(norm2_g[l]), ffn_w_gate[l].astype(BF16), ffn_w_up[l].astype(BF16),
                  ffn_w_down[l].astype(BF16), row(final_g), final_norm=(l == depth - 1))
    return xf.reshape(batch, seq, d)
```

```python
import functools

import numpy as np
import jax
import jax.numpy as jnp
from jax import lax
from jax.experimental import pallas as pl
from jax.experimental.pallas import tpu as pltpu

D_MODEL = 2048
SEQ = 2048
RET_HEADS = 8
HEAD_DIM = 128
RET_WIDTH = RET_HEADS * HEAD_DIM
LRU_WIDTH = 1024
LRU_GROUPS = 16
LRU_GROUP_DIM = 64
CONV_W = 4
LRU_C = 8.0
IN_WIDTH = 4 * RET_WIDTH + 2 * LRU_WIDTH
CHUNK = 64
ROPE_BASE = 10000.0
EPS = 1e-6

F32 = jnp.float32
BF16 = jnp.bfloat16

RET_BLOCK = 256
LRU_BLOCK = 256
GATE_BLOCK = 256
SUBLANES = 8
VMEM_LIMIT = 56 * 1024 * 1024


def _params(*sem):
    return pltpu.CompilerParams(dimension_semantics=sem, vmem_limit_bytes=VMEM_LIMIT)


def _rms_scale(x, g):
    ms = jnp.mean(x * x, axis=-1, keepdims=True)
    return x * lax.rsqrt(ms + EPS) * g


def _in_proj_kernel(x_ref, g_ref, w_ref, o_ref, h_ref):
    @pl.when(pl.program_id(1) == 0)
    def _():
        h_ref[...] = _rms_scale(x_ref[...], g_ref[...]).astype(BF16)

    o_ref[...] = jnp.dot(h_ref[...], w_ref[...].astype(BF16),
                         preferred_element_type=F32).astype(o_ref.dtype)


def _in_proj(x, g, w, layer, tm=1024, tn=1024):
    m, d = x.shape
    n = w.shape[2]
    return pl.pallas_call(
        _in_proj_kernel,
        grid=(m // tm, n // tn),
        in_specs=[pl.BlockSpec((tm, d), lambda i, j: (i, 0)),
                  pl.BlockSpec((1, d), lambda i, j: (0, 0)),
                  pl.BlockSpec((None, d, tn), lambda i, j: (layer, 0, j))],
        out_specs=pl.BlockSpec((tm, tn), lambda i, j: (i, j)),
        out_shape=jax.ShapeDtypeStruct((m, n), BF16),
        scratch_shapes=[pltpu.VMEM((tm, d), BF16)],
        compiler_params=_params("parallel", "arbitrary"),
        name="in_proj",
    )(x, g, w)


def _retention_kernel(q_ref, k_ref, v_ref, g_ref, cq_ref, sq_ref, ck_ref, sk_ref,
                      dmask_ref, qdec_ref, kdec_ref, gn_ref, o_ref, s_ref, *, sdec):
    @pl.when(pl.program_id(1) == 0)
    def _():
        s_ref[...] = jnp.zeros_like(s_ref)

    cq, sq, ck, sk = cq_ref[...], sq_ref[...], ck_ref[...], sk_ref[...]
    half = HEAD_DIM // 2
    for h in range(RET_HEADS):
        sl = slice(h * HEAD_DIM, (h + 1) * HEAD_DIM)
        q = q_ref[:, sl].astype(F32)
        k = k_ref[:, sl].astype(F32)
        v = v_ref[:, sl]
        qr = q * cq + pltpu.roll(q, half, 1) * sq
        kr = k * ck + pltpu.roll(k, half, 1) * sk
        qb = qr.astype(BF16)
        kb = kr.astype(BF16)
        scores = lax.dot_general(qb, kb, (((1,), (1,)), ((), ())),
                                 preferred_element_type=F32) * dmask_ref[h]
        o = jnp.dot(scores.astype(BF16), v, preferred_element_type=F32)
        state = s_ref[h]
        o = o + qdec_ref[h] * jnp.dot(qb, state.astype(BF16), preferred_element_type=F32)
        kd = (kr * kdec_ref[h]).astype(BF16)
        s_ref[h] = sdec[h] * state + lax.dot_general(
            kd, v, (((0,), (0,)), ((), ())), preferred_element_type=F32)
        mu = jnp.mean(o, axis=-1, keepdims=True)
        oc = o - mu
        var = jnp.mean(oc * oc, axis=-1, keepdims=True)
        on = oc * lax.rsqrt(var + EPS) * gn_ref[:, sl]
        gate = g_ref[:, sl].astype(F32)
        o_ref[:, sl] = (on * (gate * jax.nn.sigmoid(gate))).astype(o_ref.dtype)


def _retention_tables():
    cs = RET_BLOCK
    pos = np.arange(SEQ, dtype=np.float32)
    inv = (1.0 / (ROPE_BASE ** (np.arange(0, HEAD_DIM, 2, dtype=np.float32) / HEAD_DIM))).astype(np.float32)
    ang = jnp.asarray(pos[:, None] * inv[None, :])
    cos, sin = jnp.cos(ang), jnp.sin(ang)
    cfull = jnp.concatenate([cos, cos], axis=-1)
    ssign = jnp.concatenate([-sin, sin], axis=-1)
    kscale = HEAD_DIM ** -0.5
    gamma_log = np.log1p(-np.exp2(-5.0 - np.arange(RET_HEADS, dtype=np.float64)))
    idx = np.arange(cs)
    dist = np.abs(idx[:, None] - idx[None, :])
    visible = (idx[None, :] // CHUNK) <= (idx[:, None] // CHUNK)
    dmask = np.where(visible[None], np.exp(gamma_log[:, None, None] * dist[None]), 0.0)
    qdec = np.exp(gamma_log[:, None] * (idx + 1.0)[None, :])
    kdec = np.exp(gamma_log[:, None] * (cs - 1.0 - idx)[None, :])
    sdec = tuple(float(s) for s in np.exp(gamma_log * cs))
    bcast = lambda a: jnp.asarray(np.broadcast_to(a[:, :, None], (RET_HEADS, cs, HEAD_DIM)), F32)
    return (cfull, ssign, cfull * kscale, ssign * kscale,
            jnp.asarray(dmask, F32), bcast(qdec), bcast(kdec), sdec)


def _retention(p, gn_g, tables, batch):
    cfull, ssign, ckf, skf, dmask, qdec, kdec, sdec = tables
    m = p.shape[0]
    cs = RET_BLOCK
    nt = SEQ // cs
    row = lambda b, t: b * nt + t
    pspec = lambda c: pl.BlockSpec((cs, RET_WIDTH), lambda b, t, c=c: (row(b, t), c))
    tspec = pl.BlockSpec((cs, HEAD_DIM), lambda b, t: (t, 0))
    full3 = lambda a: pl.BlockSpec(a.shape, lambda b, t: (0, 0, 0))
    return pl.pallas_call(
        functools.partial(_retention_kernel, sdec=sdec),
        grid=(batch, nt),
        in_specs=[pspec(0), pspec(1), pspec(2), pspec(3),
                  tspec, tspec, tspec, tspec,
                  full3(dmask), full3(qdec), full3(kdec),
                  pl.BlockSpec((1, RET_WIDTH), lambda b, t: (0, 0))],
        out_specs=pl.BlockSpec((cs, RET_WIDTH), lambda b, t: (row(b, t), 0)),
        out_shape=jax.ShapeDtypeStruct((m, RET_WIDTH), BF16),
        scratch_shapes=[pltpu.VMEM((RET_HEADS, HEAD_DIM, HEAD_DIM), F32)],
        compiler_params=_params("parallel", "arbitrary"),
        name="retention",
    )(p, p, p, p, cfull, ssign, ckf, skf, dmask, qdec, kdec, gn_g)


def _lru_kernel(xb_ref, yb_ref, cw_ref, cb_ref, wa_ref, ba_ref, wx_ref, bx_ref,
                lam_ref, ng_ref, o_ref, xpad_ref, a_ref, b_ref, h_ref, carry_ref):
    tt = xb_ref.shape[0]
    pad = SUBLANES

    @pl.when(pl.program_id(1) == 0)
    def _():
        xpad_ref[0:pad, :] = jnp.zeros((pad, LRU_WIDTH), F32)
        carry_ref[...] = jnp.zeros_like(carry_ref)

    x = xb_ref[...].astype(F32)
    xpad_ref[pad:pad + tt, :] = x
    xc = cb_ref[...] + cw_ref[CONV_W - 1:CONV_W, :] * x
    for j in range(CONV_W - 1):
        off = pad - (CONV_W - 1) + j
        xc = xc + cw_ref[j:j + 1, :] * xpad_ref[off:off + tt, :]
    xpad_ref[0:pad, :] = x[tt - pad:tt, :]

    softplus_neg_lam = (jnp.maximum(-lam_ref[...], 0.0)
                        + jnp.log1p(jnp.exp(-jnp.abs(lam_ref[...]))))
    for c in range(LRU_WIDTH // GATE_BLOCK):
        sl = slice(c * GATE_BLOCK, (c + 1) * GATE_BLOCK)
        xcb = xc[:, sl]
        xcb16 = xcb.astype(BF16)
        r = jax.nn.sigmoid(jnp.dot(xcb16, wa_ref[c], preferred_element_type=F32) + ba_ref[:, sl])
        i = jax.nn.sigmoid(jnp.dot(xcb16, wx_ref[c], preferred_element_type=F32) + bx_ref[:, sl])
        log_a = -LRU_C * r * softplus_neg_lam[:, sl]
        a = jnp.exp(log_a)
        a_ref[:, sl] = a
        b_ref[:, sl] = jnp.sqrt(-jnp.tanh(log_a) * (a * a + 1.0)) * (i * xcb)

    row = lax.broadcasted_iota(jnp.int32, (SUBLANES, LRU_WIDTH), 0)

    def group(gidx, carry):
        start = pl.multiple_of(gidx * SUBLANES, SUBLANES)
        a = a_ref[pl.ds(start, SUBLANES), :]
        b = b_ref[pl.ds(start, SUBLANES), :]
        for s in (1, 2, 4):
            keep = row >= s
            b = jnp.where(keep, a * pltpu.roll(b, s, 0) + b, b)
            a = jnp.where(keep, a * pltpu.roll(a, s, 0), a)
        h = a * carry + b
        h_ref[pl.ds(start, SUBLANES), :] = h
        return h[SUBLANES - 1:SUBLANES, :]

    carry_ref[...] = lax.fori_loop(0, tt // SUBLANES, group, carry_ref[...], unroll=4)

    y = h_ref[...] * jax.nn.gelu(yb_ref[...].astype(F32))
    o_ref[...] = _rms_scale(y, ng_ref[...]).astype(o_ref.dtype)


def _block_diag(w):
    per = GATE_BLOCK // LRU_GROUP_DIM
    nb = LRU_GROUPS // per
    w4 = w.reshape(nb, per, LRU_GROUP_DIM, LRU_GROUP_DIM)
    bd = jnp.einsum('cipq,ij->cipjq', w4, jnp.eye(per, dtype=w.dtype))
    return bd.reshape(nb, GATE_BLOCK, GATE_BLOCK).astype(BF16)


def _lru(p, cw, cb, wa, ba, wx, bx, lam, ng, batch):
    m = p.shape[0]
    tt = LRU_BLOCK
    nt = SEQ // tt
    row = lambda b, t: b * nt + t
    xcol = 4 * RET_WIDTH // LRU_WIDTH
    vec = pl.BlockSpec((1, LRU_WIDTH), lambda b, t: (0, 0))
    wspec = pl.BlockSpec(wa.shape, lambda b, t: (0, 0, 0))
    return pl.pallas_call(
        _lru_kernel,
        grid=(batch, nt),
        in_specs=[pl.BlockSpec((tt, LRU_WIDTH), lambda b, t: (row(b, t), xcol)),
                  pl.BlockSpec((tt, LRU_WIDTH), lambda b, t: (row(b, t), xcol + 1)),
                  pl.BlockSpec((CONV_W, LRU_WIDTH), lambda b, t: (0, 0)),
                  vec, wspec, vec, wspec, vec, vec, vec],
        out_specs=pl.BlockSpec((tt, LRU_WIDTH), lambda b, t: (row(b, t), 0)),
        out_shape=jax.ShapeDtypeStruct((m, LRU_WIDTH), BF16),
        scratch_shapes=[pltpu.VMEM((tt + SUBLANES, LRU_WIDTH), F32),
                        pltpu.VMEM((tt, LRU_WIDTH), F32),
                        pltpu.VMEM((tt, LRU_WIDTH), F32),
                        pltpu.VMEM((tt, LRU_WIDTH), F32),
                        pltpu.VMEM((1, LRU_WIDTH), F32)],
        compiler_params=_params("parallel", "arbitrary"),
        name="rg_lru",
    )(p, p, cw, cb, wa, ba, wx, bx, lam, ng)


def _out_proj_kernel(x_ref, r_ref, l_ref, w_ref, o_ref):
    acc = jnp.dot(r_ref[...], w_ref[0:RET_WIDTH, :].astype(BF16), preferred_element_type=F32)
    acc = acc + jnp.dot(l_ref[...], w_ref[RET_WIDTH:, :].astype(BF16), preferred_element_type=F32)
    o_ref[...] = x_ref[...] + acc


def _out_proj(x, o_ret, o_lru, w, layer, tm=1024, tn=1024):
    m, d = x.shape
    return pl.pallas_call(
        _out_proj_kernel,
        grid=(m // tm, d // tn),
        in_specs=[pl.BlockSpec((tm, tn), lambda i, j: (i, j)),
                  pl.BlockSpec((tm, RET_WIDTH), lambda i, j: (i, 0)),
                  pl.BlockSpec((tm, LRU_WIDTH), lambda i, j: (i, 0)),
                  pl.BlockSpec((None, w.shape[1], tn), lambda i, j: (layer, 0, j))],
        out_specs=pl.BlockSpec((tm, tn), lambda i, j: (i, j)),
        out_shape=jax.ShapeDtypeStruct((m, d), F32),
        compiler_params=_params("parallel", "arbitrary"),
        name="out_proj",
    )(x, o_ret, o_lru, w)


def _ffn_kernel(x_ref, g_ref, wg_ref, wu_ref, wd_ref, fg_ref, o_ref, h_ref, *, final_norm):
    j = pl.program_id(1)

    @pl.when(j == 0)
    def _():
        x = x_ref[...]
        h_ref[...] = _rms_scale(x, g_ref[...]).astype(BF16)
        o_ref[...] = x

    h = h_ref[...]
    gate = jnp.dot(h, wg_ref[...].astype(BF16), preferred_element_type=F32)
    up = jnp.dot(h, wu_ref[...].astype(BF16), preferred_element_type=F32)
    mid = (gate * jax.nn.sigmoid(gate) * up).astype(BF16)
    o_ref[...] += jnp.dot(mid, wd_ref[...].astype(BF16), preferred_element_type=F32)

    if final_norm:
        @pl.when(j == pl.num_programs(1) - 1)
        def _():
            o_ref[...] = _rms_scale(o_ref[...], fg_ref[...])


def _ffn(x, g, wg, wu, wd, fg, layer, final_norm, tm=1024, tf=256):
    m, d = x.shape
    f = wg.shape[2]
    return pl.pallas_call(
        functools.partial(_ffn_kernel, final_norm=final_norm),
        grid=(m // tm, f // tf),
        in_specs=[pl.BlockSpec((tm, d), lambda i, j: (i, 0), pipeline_mode=pl.Buffered(1)),
                  pl.BlockSpec((1, d), lambda i, j: (0, 0)),
                  pl.BlockSpec((None, d, tf), lambda i, j: (layer, 0, j)),
                  pl.BlockSpec((None, d, tf), lambda i, j: (layer, 0, j)),
                  pl.BlockSpec((None, tf, d), lambda i, j: (layer, j, 0)),
                  pl.BlockSpec((1, d), lambda i, j: (0, 0))],
        out_specs=pl.BlockSpec((tm, d), lambda i, j: (i, 0)),
        out_shape=jax.ShapeDtypeStruct((m, d), F32),
        scratch_shapes=[pltpu.VMEM((tm, d), BF16)],
        compiler_params=_params("parallel", "arbitrary"),
        name="ffn_final" if final_norm else "ffn",
    )(x, g, wg, wu, wd, fg)


def kernel(x, norm1_g, w_in, ret_gn_g, lru_conv_w, lru_conv_b, lru_wa, lru_ba, lru_wx, lru_bx,
           lru_lambda, lru_norm_g, w_out, norm2_g, ffn_w_gate, ffn_w_up, ffn_w_down, final_g):
    batch, seq, d = x.shape
    depth = w_in.shape[0]
    assert (seq, d) == (SEQ, D_MODEL)
    tables = _retention_tables()
    row = lambda a: a.reshape(1, -1).astype(F32)
    xf = x.reshape(batch * seq, d)
    for l in range(depth):
        p = _in_proj(xf, row(norm1_g[l]), w_in, l)
        o_ret = _retention(p, row(ret_gn_g[l]), tables, batch)
        o_lru = _lru(p, lru_conv_w[l].astype(F32), row(lru_conv_b[l]),
                     _block_diag(lru_wa[l]), row(lru_ba[l]),
                     _block_diag(lru_wx[l]), row(lru_bx[l]),
                     row(lru_lambda[l]), row(lru_norm_g[l]), batch)
        xf = _out_proj(xf, o_ret, o_lru, w_out, l)
        xf = _ffn(xf, row(norm2_g[l]), ffn_w_gate, ffn_w_up, ffn_w_down, row(final_g), l,
                  final_norm=(l == depth - 1))
    return xf.reshape(batch, seq, d)
```

```python
import functools

import numpy as np
import jax
import jax.numpy as jnp
from jax import lax
from jax.experimental import pallas as pl
from jax.experimental.pallas import tpu as pltpu

D_MODEL = 2048
SEQ = 2048
RET_HEADS = 8
HEAD_DIM = 128
RET_WIDTH = RET_HEADS * HEAD_DIM
LRU_WIDTH = 1024
LRU_GROUPS = 16
LRU_GROUP_DIM = 64
CONV_W = 4
LRU_C = 8.0
IN_WIDTH = 4 * RET_WIDTH + 2 * LRU_WIDTH
CHUNK = 64
ROPE_BASE = 10000.0
EPS = 1e-6

F32 = jnp.float32
BF16 = jnp.bfloat16

SUBLANES = 8
TM = 1024
IN_TN = 768
FFN_TF = 256
OUT_TN = 1024
MIX_STEPS = IN_WIDTH // IN_TN
LRU_BLOCK = TM // MIX_STEPS
RET_GROUPS = 2
RET_GROUP_HEADS = RET_HEADS // RET_GROUPS
RET_GROUP_WIDTH = RET_GROUP_HEADS * HEAD_DIM
RET_BLOCK = LRU_BLOCK * RET_GROUPS
GATE_BLOCK = 256
VMEM_LIMIT = 56 * 1024 * 1024


def _params(*sem):
    return pltpu.CompilerParams(dimension_semantics=sem, vmem_limit_bytes=VMEM_LIMIT)


def _rms_scale(x, g):
    ms = jnp.mean(x * x, axis=-1, keepdims=True)
    return x * lax.rsqrt(ms + EPS) * g


def _retention_head(h, hg, q_ref, k_ref, v_ref, g_ref, cq_ref, sq_ref, ck_ref, sk_ref,
                    dmask_ref, qdec_ref, kdec_ref, sdec_ref, gn_ref, o_ref, s_ref):
    cq, sq, ck, sk = cq_ref[...], sq_ref[...], ck_ref[...], sk_ref[...]
    half = HEAD_DIM // 2
    sl = slice(h * HEAD_DIM, (h + 1) * HEAD_DIM)
    q = q_ref[:, sl].astype(F32)
    k = k_ref[:, sl].astype(F32)
    v = v_ref[:, sl]
    qr = q * cq + pltpu.roll(q, half, 1) * sq
    kr = k * ck + pltpu.roll(k, half, 1) * sk
    qb = qr.astype(BF16)
    kb = kr.astype(BF16)
    scores = lax.dot_general(qb, kb, (((1,), (1,)), ((), ())),
                             preferred_element_type=F32) * dmask_ref[h]
    o = jnp.dot(scores.astype(BF16), v, preferred_element_type=F32)
    sidx = hg * RET_GROUP_HEADS + h
    state = s_ref[sidx]
    o = o + qdec_ref[h] * jnp.dot(qb, state.astype(BF16), preferred_element_type=F32)
    kd = (kr * kdec_ref[h]).astype(BF16)
    s_ref[sidx] = sdec_ref[h] * state + lax.dot_general(
        kd, v, (((0,), (0,)), ((), ())), preferred_element_type=F32)
    mu = jnp.mean(o, axis=-1, keepdims=True)
    oc = o - mu
    var = jnp.mean(oc * oc, axis=-1, keepdims=True)
    on = oc * lax.rsqrt(var + EPS) * gn_ref[:, sl]
    gate = g_ref[:, sl].astype(F32)
    o_ref[:, sl] = (on * (gate * jax.nn.sigmoid(gate))).astype(o_ref.dtype)


def _lru_cols(c, xb_ref, yb_ref, cw_ref, cb_ref, wa_ref, ba_ref, wx_ref, bx_ref, lam_ref,
              xpad_ref, a_ref, b_ref, y_ref, carry_ref):
    tt = xb_ref.shape[0]
    pad = SUBLANES
    sl = slice(c * GATE_BLOCK, (c + 1) * GATE_BLOCK)
    x = xb_ref[:, sl].astype(F32)
    xpad_ref[pad:pad + tt, sl] = x
    xc = cb_ref[:, sl] + cw_ref[CONV_W - 1:CONV_W, sl] * x
    for j in range(CONV_W - 1):
        off = pad - (CONV_W - 1) + j
        xc = xc + cw_ref[j:j + 1, sl] * xpad_ref[off:off + tt, sl]
    xpad_ref[0:pad, sl] = x[tt - pad:tt, :]

    lam = lam_ref[:, sl]
    softplus_neg_lam = jnp.maximum(-lam, 0.0) + jnp.log1p(jnp.exp(-jnp.abs(lam)))
    xc16 = xc.astype(BF16)
    r = jax.nn.sigmoid(jnp.dot(xc16, wa_ref[c], preferred_element_type=F32) + ba_ref[:, sl])
    i = jax.nn.sigmoid(jnp.dot(xc16, wx_ref[c], preferred_element_type=F32) + bx_ref[:, sl])
    log_a = -LRU_C * r * softplus_neg_lam
    a = jnp.exp(log_a)
    a_ref[:, sl] = a
    b_ref[:, sl] = jnp.sqrt(-jnp.tanh(log_a) * (a * a + 1.0)) * (i * xc)

    row = lax.broadcasted_iota(jnp.int32, (SUBLANES, GATE_BLOCK), 0)
    carry = carry_ref[:, sl]
    for gidx in range(tt // SUBLANES):
        rows = slice(gidx * SUBLANES, (gidx + 1) * SUBLANES)
        a = a_ref[rows, sl]
        b = b_ref[rows, sl]
        for s in (1, 2, 4):
            keep = row >= s
            b = jnp.where(keep, a * pltpu.roll(b, s, 0) + b, b)
            a = jnp.where(keep, a * pltpu.roll(a, s, 0), a)
        h = a * carry + b
        b_ref[rows, sl] = h
        carry = h[SUBLANES - 1:SUBLANES, :]
    carry_ref[:, sl] = carry

    y = b_ref[:, sl] * jax.nn.gelu(yb_ref[:, sl].astype(F32))
    y_ref[:, sl] = y
    return y * y


def _lru_finish(ysq, ng_ref, o_ref, y_ref):
    ms = jnp.sum(ysq, axis=-1, keepdims=True) * (1.0 / LRU_WIDTH)
    o_ref[...] = (y_ref[...] * lax.rsqrt(ms + EPS) * ng_ref[...]).astype(o_ref.dtype)


N_MIX_IN = 23
N_MIX_OUT = 2
N_MIX_SCRATCH = 6
MIX_PIECES = LRU_WIDTH // GATE_BLOCK
assert MIX_PIECES == RET_GROUP_HEADS


def _mixer_resets(s, active, mix_scratch):
    s_ref, xpad_ref, _, _, _, carry_ref = mix_scratch
    lru_blocks = SEQ // LRU_BLOCK
    hg = s % RET_GROUPS

    @pl.when(jnp.logical_and(active, s % lru_blocks == 0))
    def _():
        xpad_ref[0:SUBLANES, :] = jnp.zeros((SUBLANES, LRU_WIDTH), F32)
        carry_ref[...] = jnp.zeros_like(carry_ref)

    @pl.when(jnp.logical_and(active, (s // RET_GROUPS) % (SEQ // RET_BLOCK) == 0))
    def _():
        s_ref[pl.ds(hg * RET_GROUP_HEADS, RET_GROUP_HEADS)] = jnp.zeros(
            (RET_GROUP_HEADS, HEAD_DIM, HEAD_DIM), F32)


def _mixer_piece(c, s, ysq, mix_in, mix_out, mix_scratch):
    (q, k, v, g, xb, yb, cq, sq, ck, sk, dmask, qdec, kdec, sdec, gn,
     cw, cb, wa, ba, wx, bx, lam, ng) = mix_in
    o_ret, o_lru = mix_out
    s_ref, xpad_ref, a_ref, b_ref, y_ref, carry_ref = mix_scratch
    _retention_head(c, s % RET_GROUPS, q, k, v, g, cq, sq, ck, sk, dmask, qdec, kdec, sdec, gn,
                    o_ret, s_ref)
    ysq_c = _lru_cols(c, xb, yb, cw, cb, wa, ba, wx, bx, lam, xpad_ref, a_ref, b_ref, y_ref,
                      carry_ref)
    ysq = ysq_c if ysq is None else ysq + ysq_c
    if c == MIX_PIECES - 1:
        _lru_finish(ysq, ng, o_lru, y_ref)
    return ysq


def _mixer_specs(smap, layer):
    hg = lambda i, j: smap(i, j) % RET_GROUPS
    rblk = lambda i, j: smap(i, j) // RET_GROUPS
    tblk = lambda i, j: rblk(i, j) % (SEQ // RET_BLOCK)
    pcol = lambda c: pl.BlockSpec((RET_BLOCK, RET_GROUP_WIDTH),
                                  lambda i, j, c=c: (rblk(i, j), c * RET_GROUPS + hg(i, j)))
    lcol = lambda c: pl.BlockSpec((LRU_BLOCK, LRU_WIDTH), lambda i, j, c=c: (smap(i, j), c))
    table = pl.BlockSpec((RET_BLOCK, HEAD_DIM), lambda i, j: (tblk(i, j), 0))
    per_group = lambda *tail: pl.BlockSpec((RET_GROUP_HEADS,) + tail,
                                           lambda i, j: (hg(i, j),) + (0,) * len(tail))
    vec = pl.BlockSpec((None, 1, LRU_WIDTH), lambda i, j: (layer, 0, 0))
    gatew = pl.BlockSpec((None, LRU_WIDTH // GATE_BLOCK, GATE_BLOCK, GATE_BLOCK),
                         lambda i, j: (layer, 0, 0, 0))
    xcol = 4 * RET_WIDTH // LRU_WIDTH
    in_specs = [pcol(0), pcol(1), pcol(2), pcol(3), lcol(xcol), lcol(xcol + 1),
                table, table, table, table,
                per_group(RET_BLOCK, RET_BLOCK), per_group(RET_BLOCK, HEAD_DIM),
                per_group(RET_BLOCK, HEAD_DIM), per_group(1, HEAD_DIM),
                pl.BlockSpec((None, 1, RET_GROUP_WIDTH), lambda i, j: (layer, 0, hg(i, j))),
                pl.BlockSpec((None, CONV_W, LRU_WIDTH), lambda i, j: (layer, 0, 0)),
                vec, gatew, vec, gatew, vec, vec, vec]
    out_specs = [pl.BlockSpec((RET_BLOCK, RET_GROUP_WIDTH), lambda i, j: (rblk(i, j), hg(i, j))),
                 pl.BlockSpec((LRU_BLOCK, LRU_WIDTH), lambda i, j: (smap(i, j), 0))]
    scratch = [pltpu.VMEM((RET_HEADS, HEAD_DIM, HEAD_DIM), F32),
               pltpu.VMEM((LRU_BLOCK + SUBLANES, LRU_WIDTH), F32),
               pltpu.VMEM((LRU_BLOCK, LRU_WIDTH), F32),
               pltpu.VMEM((LRU_BLOCK, LRU_WIDTH), F32),
               pltpu.VMEM((LRU_BLOCK, LRU_WIDTH), F32),
               pltpu.VMEM((1, LRU_WIDTH), F32)]
    return in_specs, out_specs, scratch


def _mixer_out_shapes(rows):
    return [jax.ShapeDtypeStruct((rows, RET_WIDTH), BF16),
            jax.ShapeDtypeStruct((rows, LRU_WIDTH), BF16)]


def _mixer_operands(p, mix):
    return (p,) * 6 + tuple(mix)


def _in_proj_chunk(c, chunks, wb, o_ref, h_ref):
    rows = slice(c * TM // chunks, (c + 1) * TM // chunks)
    o_ref[rows, :] = jnp.dot(h_ref[rows, :], wb, preferred_element_type=F32).astype(o_ref.dtype)


def _in_proj_prologue(x_ref, g_ref, h_ref):
    @pl.when(pl.program_id(1) == 0)
    def _():
        h_ref[...] = _rms_scale(x_ref[...], g_ref[...]).astype(BF16)


def _in_proj_kernel(x_ref, g_ref, w_ref, o_ref, h_ref):
    _in_proj_prologue(x_ref, g_ref, h_ref)
    _in_proj_chunk(0, 1, w_ref[...].astype(BF16), o_ref, h_ref)


def _in_proj_mix_kernel(*refs):
    x_ref, g_ref, w_ref = refs[:3]
    mix_in = refs[3:3 + N_MIX_IN]
    o_ref = refs[3 + N_MIX_IN]
    mix_out = refs[4 + N_MIX_IN:4 + N_MIX_IN + N_MIX_OUT]
    h_ref = refs[-N_MIX_SCRATCH - 1]
    mix_scratch = refs[-N_MIX_SCRATCH:]
    s = pl.program_id(0) * MIX_STEPS + pl.program_id(1)
    _in_proj_prologue(x_ref, g_ref, h_ref)
    _mixer_resets(s, True, mix_scratch)
    wb = w_ref[...].astype(BF16)
    ysq = None
    for c in range(MIX_PIECES):
        _in_proj_chunk(c, MIX_PIECES, wb, o_ref, h_ref)
        ysq = _mixer_piece(c, s, ysq, mix_in, mix_out, mix_scratch)


def _in_proj_specs(x_tile0, layer):
    d = D_MODEL
    in_specs = [pl.BlockSpec((TM, d), lambda i, j: (x_tile0 + i, 0), pipeline_mode=pl.Buffered(1)),
                pl.BlockSpec((None, 1, d), lambda i, j: (layer, 0, 0)),
                pl.BlockSpec((None, d, IN_TN), lambda i, j: (layer, 0, j))]
    out_spec = pl.BlockSpec((TM, IN_TN), lambda i, j: (i, j))
    return in_specs, out_spec


def _in_proj(x, x_tile0, rows, g, w, layer, mix=None):
    in_specs, out_spec = _in_proj_specs(x_tile0, layer)
    grid = (rows // TM, IN_WIDTH // IN_TN)
    p_shape = jax.ShapeDtypeStruct((rows, IN_WIDTH), BF16)
    h_scratch = pltpu.VMEM((TM, D_MODEL), BF16)
    if mix is None:
        return pl.pallas_call(
            _in_proj_kernel, grid=grid, in_specs=in_specs, out_specs=out_spec,
            out_shape=p_shape, scratch_shapes=[h_scratch],
            compiler_params=_params("parallel", "arbitrary"), name="in_proj",
        )(x, g, w)
    p_other, mix_params = mix
    m_in, m_out, m_scratch = _mixer_specs(lambda i, j: i * MIX_STEPS + j, layer)
    return pl.pallas_call(
        _in_proj_mix_kernel, grid=grid,
        in_specs=in_specs + m_in, out_specs=[out_spec] + m_out,
        out_shape=[p_shape] + _mixer_out_shapes(p_other.shape[0]),
        scratch_shapes=[h_scratch] + m_scratch,
        compiler_params=_params("arbitrary", "arbitrary"), name="in_proj_mix",
    )(x, g, w, *_mixer_operands(p_other, mix_params))


def _out_proj_kernel(x_ref, r_ref, l_ref, w_ref, o_ref):
    acc = jnp.dot(r_ref[...], w_ref[0:RET_WIDTH, :].astype(BF16), preferred_element_type=F32)
    acc = acc + jnp.dot(l_ref[...], w_ref[RET_WIDTH:, :].astype(BF16), preferred_element_type=F32)
    o_ref[...] = x_ref[...] + acc


def _out_proj(x, x_tile0, o_ret, o_lru, w, layer):
    rows = o_ret.shape[0]
    d = D_MODEL
    return pl.pallas_call(
        _out_proj_kernel,
        grid=(d // OUT_TN, rows // TM),
        in_specs=[pl.BlockSpec((TM, OUT_TN), lambda j, i: (x_tile0 + i, j)),
                  pl.BlockSpec((TM, RET_WIDTH), lambda j, i: (i, 0)),
                  pl.BlockSpec((TM, LRU_WIDTH), lambda j, i: (i, 0)),
                  pl.BlockSpec((None, w.shape[1], OUT_TN), lambda j, i: (layer, 0, j))],
        out_specs=pl.BlockSpec((TM, OUT_TN), lambda j, i: (i, j)),
        out_shape=jax.ShapeDtypeStruct((rows, d), F32),
        compiler_params=_params("parallel", "arbitrary"),
        name="out_proj",
    )(x, o_ret, o_lru, w)


def _ffn_prologue(x_ref, g_ref, o_ref, h_ref):
    @pl.when(pl.program_id(1) == 0)
    def _():
        x = x_ref[...]
        h_ref[...] = _rms_scale(x, g_ref[...]).astype(BF16)
        o_ref[...] = x


def _ffn_chunk(c, chunks, wg, wu, wd, o_ref, h_ref):
    rows = slice(c * TM // chunks, (c + 1) * TM // chunks)
    h = h_ref[rows, :]
    gate = jnp.dot(h, wg, preferred_element_type=F32)
    up = jnp.dot(h, wu, preferred_element_type=F32)
    mid = (gate * jax.nn.sigmoid(gate) * up).astype(BF16)
    o_ref[rows, :] += jnp.dot(mid, wd, preferred_element_type=F32)


def _ffn_weights(wg_ref, wu_ref, wd_ref):
    return wg_ref[...].astype(BF16), wu_ref[...].astype(BF16), wd_ref[...].astype(BF16)


def _ffn_epilogue(fg_ref, o_ref, final_norm):
    if final_norm:
        @pl.when(pl.program_id(1) == pl.num_programs(1) - 1)
        def _():
            o_ref[...] = _rms_scale(o_ref[...], fg_ref[...])


def _ffn_kernel(x_ref, g_ref, wg_ref, wu_ref, wd_ref, fg_ref, o_ref, h_ref, *, final_norm):
    _ffn_prologue(x_ref, g_ref, o_ref, h_ref)
    _ffn_chunk(0, 1, *_ffn_weights(wg_ref, wu_ref, wd_ref), o_ref, h_ref)
    _ffn_epilogue(fg_ref, o_ref, final_norm)


def _ffn_mix_kernel(*refs, final_norm):
    x_ref, g_ref, wg_ref, wu_ref, wd_ref, fg_ref = refs[:6]
    mix_in = refs[6:6 + N_MIX_IN]
    o_ref = refs[6 + N_MIX_IN]
    mix_out = refs[7 + N_MIX_IN:7 + N_MIX_IN + N_MIX_OUT]
    h_ref = refs[-N_MIX_SCRATCH - 1]
    mix_scratch = refs[-N_MIX_SCRATCH:]
    j = pl.program_id(1)
    active = j < MIX_STEPS
    s = pl.program_id(0) * MIX_STEPS + jnp.minimum(j, MIX_STEPS - 1)
    _ffn_prologue(x_ref, g_ref, o_ref, h_ref)
    _mixer_resets(s, active, mix_scratch)

    @pl.when(active)
    def _():
        weights = _ffn_weights(wg_ref, wu_ref, wd_ref)
        ysq = None
        for c in range(MIX_PIECES):
            _ffn_chunk(c, MIX_PIECES, *weights, o_ref, h_ref)
            ysq = _mixer_piece(c, s, ysq, mix_in, mix_out, mix_scratch)

    @pl.when(jnp.logical_not(active))
    def _():
        _ffn_chunk(0, 1, *_ffn_weights(wg_ref, wu_ref, wd_ref), o_ref, h_ref)

    _ffn_epilogue(fg_ref, o_ref, final_norm)


def _ffn(x, g, wg, wu, wd, fg, layer, final_norm, mix=None):
    rows, d = x.shape
    f = wg.shape[2]
    grid = (rows // TM, f // FFN_TF)
    in_specs = [pl.BlockSpec((TM, d), lambda i, j: (i, 0), pipeline_mode=pl.Buffered(1)),
                pl.BlockSpec((None, 1, d), lambda i, j: (layer, 0, 0)),
                pl.BlockSpec((None, d, FFN_TF), lambda i, j: (layer, 0, j)),
                pl.BlockSpec((None, d, FFN_TF), lambda i, j: (layer, 0, j)),
                pl.BlockSpec((None, FFN_TF, d), lambda i, j: (layer, j, 0)),
                pl.BlockSpec((1, d), lambda i, j: (0, 0))]
    out_spec = pl.BlockSpec((TM, d), lambda i, j: (i, 0))
    out_shape = jax.ShapeDtypeStruct((rows, d), F32)
    h_scratch = pltpu.VMEM((TM, d), BF16)
    if mix is None:
        return pl.pallas_call(
            functools.partial(_ffn_kernel, final_norm=final_norm),
            grid=grid, in_specs=in_specs, out_specs=out_spec, out_shape=out_shape,
            scratch_shapes=[h_scratch],
            compiler_params=_params("parallel", "arbitrary"),
            name="ffn_final" if final_norm else "ffn",
        )(x, g, wg, wu, wd, fg)
    p_other, mix_params = mix
    m_in, m_out, m_scratch = _mixer_specs(
        lambda i, j: i * MIX_STEPS + jnp.minimum(j, MIX_STEPS - 1), layer)
    return pl.pallas_call(
        functools.partial(_ffn_mix_kernel, final_norm=final_norm),
        grid=grid, in_specs=in_specs + m_in, out_specs=[out_spec] + m_out,
        out_shape=[out_shape] + _mixer_out_shapes(p_other.shape[0]),
        scratch_shapes=[h_scratch] + m_scratch,
        compiler_params=_params("arbitrary", "arbitrary"),
        name="ffn_mix_final" if final_norm else "ffn_mix",
    )(x, g, wg, wu, wd, fg, *_mixer_operands(p_other, mix_params))


def _retention_tables():
    cs = RET_BLOCK
    pos = np.arange(SEQ, dtype=np.float32)
    inv = (1.0 / (ROPE_BASE ** (np.arange(0, HEAD_DIM, 2, dtype=np.float32) / HEAD_DIM))).astype(np.float32)
    ang = jnp.asarray(pos[:, None] * inv[None, :])
    cos, sin = jnp.cos(ang), jnp.sin(ang)
    cfull = jnp.concatenate([cos, cos], axis=-1)
    ssign = jnp.concatenate([-sin, sin], axis=-1)
    kscale = HEAD_DIM ** -0.5
    gamma_log = np.log1p(-np.exp2(-5.0 - np.arange(RET_HEADS, dtype=np.float64)))
    idx = np.arange(cs)
    dist = np.abs(idx[:, None] - idx[None, :])
    visible = (idx[None, :] // CHUNK) <= (idx[:, None] // CHUNK)
    dmask = np.where(visible[None], np.exp(gamma_log[:, None, None] * dist[None]), 0.0)
    qdec = np.exp(gamma_log[:, None] * (idx + 1.0)[None, :])
    kdec = np.exp(gamma_log[:, None] * (cs - 1.0 - idx)[None, :])
    sdec = np.broadcast_to(np.exp(gamma_log * cs)[:, None, None], (RET_HEADS, 1, HEAD_DIM))
    bcast = lambda a: jnp.asarray(np.broadcast_to(a[:, :, None], (RET_HEADS, cs, HEAD_DIM)), F32)
    return (cfull, ssign, cfull * kscale, ssign * kscale,
            jnp.asarray(dmask, F32), bcast(qdec), bcast(kdec), jnp.asarray(sdec, F32))


def _block_diag(w):
    per = GATE_BLOCK // LRU_GROUP_DIM
    nb = LRU_GROUPS // per
    w4 = w.reshape(w.shape[0], nb, per, LRU_GROUP_DIM, LRU_GROUP_DIM)
    bd = jnp.einsum('lcipq,ij->lcipjq', w4, jnp.eye(per, dtype=w.dtype))
    return bd.reshape(w.shape[0], nb, GATE_BLOCK, GATE_BLOCK).astype(BF16)


def kernel(x, norm1_g, w_in, ret_gn_g, lru_conv_w, lru_conv_b, lru_wa, lru_ba, lru_wx, lru_bx,
           lru_lambda, lru_norm_g, w_out, norm2_g, ffn_w_gate, ffn_w_up, ffn_w_down, final_g):
    batch, seq, d = x.shape
    depth = w_in.shape[0]
    assert (seq, d) == (SEQ, D_MODEL) and batch % 2 == 0
    rows = batch * seq
    half = rows // 2
    half_tiles = half // TM
    vecs = lambda a: a.reshape(depth, 1, -1).astype(F32)
    mix_params = _retention_tables() + (
        vecs(ret_gn_g), lru_conv_w.astype(F32), vecs(lru_conv_b),
        _block_diag(lru_wa), vecs(lru_ba), _block_diag(lru_wx), vecs(lru_bx),
        vecs(lru_lambda), vecs(lru_norm_g))
    g1, g2, fg = vecs(norm1_g), vecs(norm2_g), final_g.reshape(1, d).astype(F32)

    xf = x.reshape(rows, d)
    xa, xb = (xf, 0), (xf, half_tiles)
    for l in range(depth):
        last = l == depth - 1
        ffn_w = (ffn_w_gate, ffn_w_up, ffn_w_down, fg, l, last)
        p_a = _in_proj(*xa, half, g1, w_in, l)
        p_b, ret_a, lru_a = _in_proj(*xb, half, g1, w_in, l, mix=(p_a, mix_params))
        x1_a = _out_proj(*xa, ret_a, lru_a, w_out, l)
        x2_a, ret_b, lru_b = _ffn(x1_a, g2, *ffn_w, mix=(p_b, mix_params))
        x1_b = _out_proj(*xb, ret_b, lru_b, w_out, l)
        x2_b = _ffn(x1_b, g2, *ffn_w)
        xa, xb = (x2_a, 0), (x2_b, 0)
    return jnp.concatenate([xa[0], xb[0]], axis=0).reshape(batch, seq, d)
```

```python
import functools

import numpy as np
import jax
import jax.numpy as jnp
from jax import lax
from jax.experimental import pallas as pl
from jax.experimental.pallas import tpu as pltpu

D_MODEL = 2048
SEQ = 2048
RET_HEADS = 8
HEAD_DIM = 128
RET_WIDTH = RET_HEADS * HEAD_DIM
LRU_WIDTH = 1024
LRU_GROUPS = 16
LRU_GROUP_DIM = 64
CONV_W = 4
LRU_C = 8.0
IN_WIDTH = 4 * RET_WIDTH + 2 * LRU_WIDTH
CHUNK = 64
ROPE_BASE = 10000.0
EPS = 1e-6

F32 = jnp.float32
BF16 = jnp.bfloat16

SUBLANES = 8
TM = 1024
IN_TN = 768
FFN_TF = 512
OUT_TN = 1024
MIX_STEPS = IN_WIDTH // IN_TN
LRU_BLOCK = TM // MIX_STEPS
RET_GROUPS = 2
RET_GROUP_HEADS = RET_HEADS // RET_GROUPS
RET_GROUP_WIDTH = RET_GROUP_HEADS * HEAD_DIM
RET_BLOCK = LRU_BLOCK * RET_GROUPS
GATE_BLOCK = 256
VMEM_LIMIT = 52 * 1024 * 1024


def _params(*sem):
    return pltpu.CompilerParams(dimension_semantics=sem, vmem_limit_bytes=VMEM_LIMIT)


def _rms_scale(x, g):
    ms = jnp.mean(x * x, axis=-1, keepdims=True)
    return x * lax.rsqrt(ms + EPS) * g


def _retention_head(h, hg, q_ref, k_ref, v_ref, g_ref, cq_ref, sq_ref, ck_ref, sk_ref,
                    dmask_ref, qdec_ref, kdec_ref, sdec_ref, gn_ref, o_ref, s_ref):
    cq, sq, ck, sk = cq_ref[...], sq_ref[...], ck_ref[...], sk_ref[...]
    half = HEAD_DIM // 2
    sl = slice(h * HEAD_DIM, (h + 1) * HEAD_DIM)
    q = q_ref[:, sl].astype(F32)
    k = k_ref[:, sl].astype(F32)
    v = v_ref[:, sl]
    qr = q * cq + pltpu.roll(q, half, 1) * sq
    kr = k * ck + pltpu.roll(k, half, 1) * sk
    qb = qr.astype(BF16)
    kb = kr.astype(BF16)
    scores = lax.dot_general(qb, kb, (((1,), (1,)), ((), ())),
                             preferred_element_type=F32) * dmask_ref[h]
    o = jnp.dot(scores.astype(BF16), v, preferred_element_type=F32)
    sidx = hg * RET_GROUP_HEADS + h
    state = s_ref[sidx]
    o = o + qdec_ref[h] * jnp.dot(qb, state.astype(BF16), preferred_element_type=F32)
    kd = (kr * kdec_ref[h]).astype(BF16)
    s_ref[sidx] = sdec_ref[h] * state + lax.dot_general(
        kd, v, (((0,), (0,)), ((), ())), preferred_element_type=F32)
    mu = jnp.mean(o, axis=-1, keepdims=True)
    oc = o - mu
    var = jnp.mean(oc * oc, axis=-1, keepdims=True)
    on = oc * lax.rsqrt(var + EPS) * gn_ref[:, sl]
    gate = g_ref[:, sl].astype(F32)
    o_ref[:, sl] = (on * (gate * jax.nn.sigmoid(gate))).astype(o_ref.dtype)


def _lru_cols(c, xb_ref, yb_ref, cw_ref, cb_ref, wa_ref, ba_ref, wx_ref, bx_ref, lam_ref,
              xpad_ref, a_ref, b_ref, y_ref, carry_ref):
    tt = xb_ref.shape[0]
    pad = SUBLANES
    sl = slice(c * GATE_BLOCK, (c + 1) * GATE_BLOCK)
    x = xb_ref[:, sl].astype(F32)
    xpad_ref[pad:pad + tt, sl] = x
    xc = cb_ref[:, sl] + cw_ref[CONV_W - 1:CONV_W, sl] * x
    for j in range(CONV_W - 1):
        off = pad - (CONV_W - 1) + j
        xc = xc + cw_ref[j:j + 1, sl] * xpad_ref[off:off + tt, sl]
    xpad_ref[0:pad, sl] = x[tt - pad:tt, :]

    lam = lam_ref[:, sl]
    softplus_neg_lam = jnp.maximum(-lam, 0.0) + jnp.log1p(jnp.exp(-jnp.abs(lam)))
    xc16 = xc.astype(BF16)
    r = jax.nn.sigmoid(jnp.dot(xc16, wa_ref[c], preferred_element_type=F32) + ba_ref[:, sl])
    i = jax.nn.sigmoid(jnp.dot(xc16, wx_ref[c], preferred_element_type=F32) + bx_ref[:, sl])
    log_a = -LRU_C * r * softplus_neg_lam
    a = jnp.exp(log_a)
    a_ref[:, sl] = a
    b_ref[:, sl] = jnp.sqrt(-jnp.tanh(log_a) * (a * a + 1.0)) * (i * xc)

    row = lax.broadcasted_iota(jnp.int32, (SUBLANES, GATE_BLOCK), 0)
    carry = carry_ref[:, sl]
    for gidx in range(tt // SUBLANES):
        rows = slice(gidx * SUBLANES, (gidx + 1) * SUBLANES)
        a = a_ref[rows, sl]
        b = b_ref[rows, sl]
        for s in (1, 2, 4):
            keep = row >= s
            b = jnp.where(keep, a * pltpu.roll(b, s, 0) + b, b)
            a = jnp.where(keep, a * pltpu.roll(a, s, 0), a)
        h = a * carry + b
        b_ref[rows, sl] = h
        carry = h[SUBLANES - 1:SUBLANES, :]
    carry_ref[:, sl] = carry

    y = b_ref[:, sl] * jax.nn.gelu(yb_ref[:, sl].astype(F32))
    y_ref[:, sl] = y
    return y * y


def _lru_finish(ysq, ng_ref, o_ref, y_ref):
    ms = jnp.sum(ysq, axis=-1, keepdims=True) * (1.0 / LRU_WIDTH)
    o_ref[...] = (y_ref[...] * lax.rsqrt(ms + EPS) * ng_ref[...]).astype(o_ref.dtype)


N_MIX_IN = 23
N_MIX_OUT = 2
N_MIX_SCRATCH = 6
MIX_PIECES = LRU_WIDTH // GATE_BLOCK
assert MIX_PIECES == RET_GROUP_HEADS


def _mixer_resets(s, active, mix_scratch):
    s_ref, xpad_ref, _, _, _, carry_ref = mix_scratch
    lru_blocks = SEQ // LRU_BLOCK
    hg = s % RET_GROUPS

    @pl.when(jnp.logical_and(active, s % lru_blocks == 0))
    def _():
        xpad_ref[0:SUBLANES, :] = jnp.zeros((SUBLANES, LRU_WIDTH), F32)
        carry_ref[...] = jnp.zeros_like(carry_ref)

    @pl.when(jnp.logical_and(active, (s // RET_GROUPS) % (SEQ // RET_BLOCK) == 0))
    def _():
        s_ref[pl.ds(hg * RET_GROUP_HEADS, RET_GROUP_HEADS)] = jnp.zeros(
            (RET_GROUP_HEADS, HEAD_DIM, HEAD_DIM), F32)


def _mixer_piece(c, s, ysq, mix_in, mix_out, mix_scratch):
    (q, k, v, g, xb, yb, cq, sq, ck, sk, dmask, qdec, kdec, sdec, gn,
     cw, cb, wa, ba, wx, bx, lam, ng) = mix_in
    o_ret, o_lru = mix_out
    s_ref, xpad_ref, a_ref, b_ref, y_ref, carry_ref = mix_scratch
    _retention_head(c, s % RET_GROUPS, q, k, v, g, cq, sq, ck, sk, dmask, qdec, kdec, sdec, gn,
                    o_ret, s_ref)
    ysq_c = _lru_cols(c, xb, yb, cw, cb, wa, ba, wx, bx, lam, xpad_ref, a_ref, b_ref, y_ref,
                      carry_ref)
    ysq = ysq_c if ysq is None else ysq + ysq_c
    if c == MIX_PIECES - 1:
        _lru_finish(ysq, ng, o_lru, y_ref)
    return ysq


def _mixer_specs(smap, layer):
    hg = lambda i, j: smap(i, j) % RET_GROUPS
    rblk = lambda i, j: smap(i, j) // RET_GROUPS
    tblk = lambda i, j: rblk(i, j) % (SEQ // RET_BLOCK)
    pcol = lambda c: pl.BlockSpec((RET_BLOCK, RET_GROUP_WIDTH),
                                  lambda i, j, c=c: (rblk(i, j), c * RET_GROUPS + hg(i, j)))
    lcol = lambda c: pl.BlockSpec((LRU_BLOCK, LRU_WIDTH), lambda i, j, c=c: (smap(i, j), c))
    table = pl.BlockSpec((RET_BLOCK, HEAD_DIM), lambda i, j: (tblk(i, j), 0))
    per_group = lambda *tail: pl.BlockSpec((RET_GROUP_HEADS,) + tail,
                                           lambda i, j: (hg(i, j),) + (0,) * len(tail))
    vec = pl.BlockSpec((None, 1, LRU_WIDTH), lambda i, j: (layer, 0, 0))
    gatew = pl.BlockSpec((None, LRU_WIDTH // GATE_BLOCK, GATE_BLOCK, GATE_BLOCK),
                         lambda i, j: (layer, 0, 0, 0))
    xcol = 4 * RET_WIDTH // LRU_WIDTH
    in_specs = [pcol(0), pcol(1), pcol(2), pcol(3), lcol(xcol), lcol(xcol + 1),
                table, table, table, table,
                per_group(RET_BLOCK, RET_BLOCK), per_group(RET_BLOCK, HEAD_DIM),
                per_group(RET_BLOCK, HEAD_DIM), per_group(1, HEAD_DIM),
                pl.BlockSpec((None, 1, RET_GROUP_WIDTH), lambda i, j: (layer, 0, hg(i, j))),
                pl.BlockSpec((None, CONV_W, LRU_WIDTH), lambda i, j: (layer, 0, 0)),
                vec, gatew, vec, gatew, vec, vec, vec]
    out_specs = [pl.BlockSpec((RET_BLOCK, RET_GROUP_WIDTH), lambda i, j: (rblk(i, j), hg(i, j))),
                 pl.BlockSpec((LRU_BLOCK, LRU_WIDTH), lambda i, j: (smap(i, j), 0))]
    scratch = [pltpu.VMEM((RET_HEADS, HEAD_DIM, HEAD_DIM), F32),
               pltpu.VMEM((LRU_BLOCK + SUBLANES, LRU_WIDTH), F32),
               pltpu.VMEM((LRU_BLOCK, LRU_WIDTH), F32),
               pltpu.VMEM((LRU_BLOCK, LRU_WIDTH), F32),
               pltpu.VMEM((LRU_BLOCK, LRU_WIDTH), F32),
               pltpu.VMEM((1, LRU_WIDTH), F32)]
    return in_specs, out_specs, scratch


def _mixer_out_shapes(rows):
    return [jax.ShapeDtypeStruct((rows, RET_WIDTH), BF16),
            jax.ShapeDtypeStruct((rows, LRU_WIDTH), BF16)]


def _mixer_operands(p, mix):
    return (p,) * 6 + tuple(mix)


def _in_proj_chunk(c, chunks, wb, o_ref, h_ref):
    rows = slice(c * TM // chunks, (c + 1) * TM // chunks)
    o_ref[rows, :] = jnp.dot(h_ref[rows, :], wb, preferred_element_type=F32).astype(o_ref.dtype)


def _in_proj_prologue(x_ref, g_ref, h_ref):
    @pl.when(pl.program_id(1) == 0)
    def _():
        h_ref[...] = _rms_scale(x_ref[...], g_ref[...]).astype(BF16)


def _in_proj_kernel(x_ref, g_ref, w_ref, o_ref, h_ref):
    _in_proj_prologue(x_ref, g_ref, h_ref)
    _in_proj_chunk(0, 1, w_ref[...], o_ref, h_ref)


def _in_proj_mix_kernel(*refs):
    x_ref, g_ref, w_ref = refs[:3]
    mix_in = refs[3:3 + N_MIX_IN]
    o_ref = refs[3 + N_MIX_IN]
    mix_out = refs[4 + N_MIX_IN:4 + N_MIX_IN + N_MIX_OUT]
    h_ref = refs[-N_MIX_SCRATCH - 1]
    mix_scratch = refs[-N_MIX_SCRATCH:]
    s = pl.program_id(0) * MIX_STEPS + pl.program_id(1)
    _in_proj_prologue(x_ref, g_ref, h_ref)
    _mixer_resets(s, True, mix_scratch)
    wb = w_ref[...]
    ysq = None
    for c in range(MIX_PIECES):
        _in_proj_chunk(c, MIX_PIECES, wb, o_ref, h_ref)
        ysq = _mixer_piece(c, s, ysq, mix_in, mix_out, mix_scratch)


def _in_proj_specs(x_tile0, layer):
    d = D_MODEL
    in_specs = [pl.BlockSpec((TM, d), lambda i, j: (x_tile0 + i, 0), pipeline_mode=pl.Buffered(1)),
                pl.BlockSpec((None, 1, d), lambda i, j: (layer, 0, 0)),
                pl.BlockSpec((None, None, d, IN_TN), lambda i, j: (layer, j, 0, 0))]
    out_spec = pl.BlockSpec((TM, IN_TN), lambda i, j: (i, j))
    return in_specs, out_spec


def _in_proj(x, x_tile0, rows, g, w, layer, mix=None):
    in_specs, out_spec = _in_proj_specs(x_tile0, layer)
    grid = (rows // TM, IN_WIDTH // IN_TN)
    p_shape = jax.ShapeDtypeStruct((rows, IN_WIDTH), BF16)
    h_scratch = pltpu.VMEM((TM, D_MODEL), BF16)
    if mix is None:
        return pl.pallas_call(
            _in_proj_kernel, grid=grid, in_specs=in_specs, out_specs=out_spec,
            out_shape=p_shape, scratch_shapes=[h_scratch],
            compiler_params=_params("parallel", "arbitrary"), name="in_proj",
        )(x, g, w)
    p_other, mix_params = mix
    m_in, m_out, m_scratch = _mixer_specs(lambda i, j: i * MIX_STEPS + j, layer)
    return pl.pallas_call(
        _in_proj_mix_kernel, grid=grid,
        in_specs=in_specs + m_in, out_specs=[out_spec] + m_out,
        out_shape=[p_shape] + _mixer_out_shapes(p_other.shape[0]),
        scratch_shapes=[h_scratch] + m_scratch,
        compiler_params=_params("arbitrary", "arbitrary"), name="in_proj_mix",
    )(x, g, w, *_mixer_operands(p_other, mix_params))


def _out_proj_kernel(x_ref, r_ref, l_ref, w_ref, o_ref):
    acc = jnp.dot(r_ref[...], w_ref[0:RET_WIDTH, :], preferred_element_type=F32)
    acc = acc + jnp.dot(l_ref[...], w_ref[RET_WIDTH:, :], preferred_element_type=F32)
    o_ref[...] = x_ref[...] + acc


def _out_proj(x, x_tile0, o_ret, o_lru, w, layer):
    rows = o_ret.shape[0]
    d = D_MODEL
    return pl.pallas_call(
        _out_proj_kernel,
        grid=(d // OUT_TN, rows // TM),
        in_specs=[pl.BlockSpec((TM, OUT_TN), lambda j, i: (x_tile0 + i, j)),
                  pl.BlockSpec((TM, RET_WIDTH), lambda j, i: (i, 0)),
                  pl.BlockSpec((TM, LRU_WIDTH), lambda j, i: (i, 0)),
                  pl.BlockSpec((None, None, d, OUT_TN), lambda j, i: (layer, j, 0, 0))],
        out_specs=pl.BlockSpec((TM, OUT_TN), lambda j, i: (i, j)),
        out_shape=jax.ShapeDtypeStruct((rows, d), F32),
        compiler_params=_params("parallel", "arbitrary"),
        name="out_proj",
    )(x, o_ret, o_lru, w)


def _ffn_prologue(x_hbm, g_ref, o_ref, h_ref):
    @pl.when(pl.program_id(1) == 0)
    def _():
        row0 = pl.multiple_of(pl.program_id(0) * TM, TM)
        pltpu.sync_copy(x_hbm.at[pl.ds(row0, TM), :], o_ref)
        h_ref[...] = _rms_scale(o_ref[...], g_ref[...]).astype(BF16)


def _ffn_chunk(c, chunks, wg, wu, wd, o_ref, h_ref):
    rows = slice(c * TM // chunks, (c + 1) * TM // chunks)
    h = h_ref[rows, :]
    gate = jnp.dot(h, wg, preferred_element_type=F32)
    up = jnp.dot(h, wu, preferred_element_type=F32)
    mid = (gate * jax.nn.sigmoid(gate) * up).astype(BF16)
    o_ref[rows, :] += jnp.dot(mid, wd, preferred_element_type=F32)


def _ffn_weights(wg_ref, wu_ref, wd_ref):
    return wg_ref[...], wu_ref[...], wd_ref[...]


def _ffn_epilogue(fg_ref, o_ref, final_norm):
    if final_norm:
        @pl.when(pl.program_id(1) == pl.num_programs(1) - 1)
        def _():
            o_ref[...] = _rms_scale(o_ref[...], fg_ref[...])


def _ffn_kernel(x_ref, g_ref, wg_ref, wu_ref, wd_ref, fg_ref, o_ref, h_ref, *, final_norm):
    _ffn_prologue(x_ref, g_ref, o_ref, h_ref)
    _ffn_chunk(0, 1, *_ffn_weights(wg_ref, wu_ref, wd_ref), o_ref, h_ref)
    _ffn_epilogue(fg_ref, o_ref, final_norm)


def _ffn_mix_kernel(*refs, final_norm):
    x_ref, g_ref, wg_ref, wu_ref, wd_ref, fg_ref = refs[:6]
    mix_in = refs[6:6 + N_MIX_IN]
    o_ref = refs[6 + N_MIX_IN]
    mix_out = refs[7 + N_MIX_IN:7 + N_MIX_IN + N_MIX_OUT]
    h_ref = refs[-N_MIX_SCRATCH - 1]
    mix_scratch = refs[-N_MIX_SCRATCH:]
    j = pl.program_id(1)
    active = j < MIX_STEPS
    s = pl.program_id(0) * MIX_STEPS + jnp.minimum(j, MIX_STEPS - 1)
    _ffn_prologue(x_ref, g_ref, o_ref, h_ref)
    _mixer_resets(s, active, mix_scratch)

    @pl.when(active)
    def _():
        weights = _ffn_weights(wg_ref, wu_ref, wd_ref)
        ysq = None
        for c in range(MIX_PIECES):
            _ffn_chunk(c, MIX_PIECES, *weights, o_ref, h_ref)
            ysq = _mixer_piece(c, s, ysq, mix_in, mix_out, mix_scratch)

    @pl.when(jnp.logical_not(active))
    def _():
        _ffn_chunk(0, 1, *_ffn_weights(wg_ref, wu_ref, wd_ref), o_ref, h_ref)

    _ffn_epilogue(fg_ref, o_ref, final_norm)


def _ffn(x, g, wg, wu, wd, fg, layer, final_norm, mix=None):
    rows, d = x.shape
    grid = (rows // TM, wg.shape[1])
    in_specs = [pl.BlockSpec(memory_space=pl.ANY),
                pl.BlockSpec((None, 1, d), lambda i, j: (layer, 0, 0)),
                pl.BlockSpec((None, None, d, FFN_TF), lambda i, j: (layer, j, 0, 0)),
                pl.BlockSpec((None, None, d, FFN_TF), lambda i, j: (layer, j, 0, 0)),
                pl.BlockSpec((None, FFN_TF, d), lambda i, j: (layer, j, 0)),
                pl.BlockSpec((1, d), lambda i, j: (0, 0))]
    out_spec = pl.BlockSpec((TM, d), lambda i, j: (i, 0))
    out_shape = jax.ShapeDtypeStruct((rows, d), F32)
    h_scratch = pltpu.VMEM((TM, d), BF16)
    if mix is None:
        return pl.pallas_call(
            functools.partial(_ffn_kernel, final_norm=final_norm),
            grid=grid, in_specs=in_specs, out_specs=out_spec, out_shape=out_shape,
            scratch_shapes=[h_scratch],
            compiler_params=_params("parallel", "arbitrary"),
            name="ffn_final" if final_norm else "ffn",
        )(x, g, wg, wu, wd, fg)
    p_other, mix_params = mix
    m_in, m_out, m_scratch = _mixer_specs(
        lambda i, j: i * MIX_STEPS + jnp.minimum(j, MIX_STEPS - 1), layer)
    return pl.pallas_call(
        functools.partial(_ffn_mix_kernel, final_norm=final_norm),
        grid=grid, in_specs=in_specs + m_in, out_specs=[out_spec] + m_out,
        out_shape=[out_shape] + _mixer_out_shapes(p_other.shape[0]),
        scratch_shapes=[h_scratch] + m_scratch,
        compiler_params=_params("arbitrary", "arbitrary"),
        name="ffn_mix_final" if final_norm else "ffn_mix",
    )(x, g, wg, wu, wd, fg, *_mixer_operands(p_other, mix_params))


def _cast_kernel(w_ref, o_ref):
    o_ref[...] = w_ref[...].astype(o_ref.dtype)


def _column_tiles(w, tn):
    nl, k, n = w.shape
    return pl.pallas_call(
        _cast_kernel,
        grid=(nl, n // tn),
        in_specs=[pl.BlockSpec((None, k, tn), lambda l, j: (l, 0, j))],
        out_specs=pl.BlockSpec((None, None, k, tn), lambda l, j: (l, j, 0, 0)),
        out_shape=jax.ShapeDtypeStruct((nl, n // tn, k, tn), BF16),
        compiler_params=_params("parallel", "parallel"),
        name="weight_column_tiles",
    )(w)


def _row_tiles(w, tk):
    nl, k, n = w.shape
    return pl.pallas_call(
        _cast_kernel,
        grid=(nl, k // tk),
        in_specs=[pl.BlockSpec((None, tk, n), lambda l, j: (l, j, 0))],
        out_specs=pl.BlockSpec((None, tk, n), lambda l, j: (l, j, 0)),
        out_shape=jax.ShapeDtypeStruct((nl, k, n), BF16),
        compiler_params=_params("parallel", "parallel"),
        name="weight_row_tiles",
    )(w)


def _retention_tables():
    cs = RET_BLOCK
    pos = np.arange(SEQ, dtype=np.float32)
    inv = (1.0 / (ROPE_BASE ** (np.arange(0, HEAD_DIM, 2, dtype=np.float32) / HEAD_DIM))).astype(np.float32)
    ang = jnp.asarray(pos[:, None] * inv[None, :])
    cos, sin = jnp.cos(ang), jnp.sin(ang)
    cfull = jnp.concatenate([cos, cos], axis=-1)
    ssign = jnp.concatenate([-sin, sin], axis=-1)
    kscale = HEAD_DIM ** -0.5
    gamma_log = np.log1p(-np.exp2(-5.0 - np.arange(RET_HEADS, dtype=np.float64)))
    idx = np.arange(cs)
    dist = np.abs(idx[:, None] - idx[None, :])
    visible = (idx[None, :] // CHUNK) <= (idx[:, None] // CHUNK)
    dmask = np.where(visible[None], np.exp(gamma_log[:, None, None] * dist[None]), 0.0)
    qdec = np.exp(gamma_log[:, None] * (idx + 1.0)[None, :])
    kdec = np.exp(gamma_log[:, None] * (cs - 1.0 - idx)[None, :])
    sdec = np.broadcast_to(np.exp(gamma_log * cs)[:, None, None], (RET_HEADS, 1, HEAD_DIM))
    bcast = lambda a: jnp.asarray(np.broadcast_to(a[:, :, None], (RET_HEADS, cs, HEAD_DIM)), F32)
    return (cfull, ssign, cfull * kscale, ssign * kscale,
            jnp.asarray(dmask, F32), bcast(qdec), bcast(kdec), jnp.asarray(sdec, F32))


def _block_diag(w):
    per = GATE_BLOCK // LRU_GROUP_DIM
    nb = LRU_GROUPS // per
    w4 = w.reshape(w.shape[0], nb, per, LRU_GROUP_DIM, LRU_GROUP_DIM)
    bd = jnp.einsum('lcipq,ij->lcipjq', w4, jnp.eye(per, dtype=w.dtype))
    return bd.reshape(w.shape[0], nb, GATE_BLOCK, GATE_BLOCK).astype(BF16)


def kernel(x, norm1_g, w_in, ret_gn_g, lru_conv_w, lru_conv_b, lru_wa, lru_ba, lru_wx, lru_bx,
           lru_lambda, lru_norm_g, w_out, norm2_g, ffn_w_gate, ffn_w_up, ffn_w_down, final_g):
    batch, seq, d = x.shape
    depth = w_in.shape[0]
    assert (seq, d) == (SEQ, D_MODEL) and batch % 2 == 0
    rows = batch * seq
    half = rows // 2
    half_tiles = half // TM
    vecs = lambda a: a.reshape(depth, 1, -1).astype(F32)
    mix_params = _retention_tables() + (
        vecs(ret_gn_g), lru_conv_w.astype(F32), vecs(lru_conv_b),
        _block_diag(lru_wa), vecs(lru_ba), _block_diag(lru_wx), vecs(lru_bx),
        vecs(lru_lambda), vecs(lru_norm_g))
    g1, g2, fg = vecs(norm1_g), vecs(norm2_g), final_g.reshape(1, d).astype(F32)

    w_in = _column_tiles(w_in, IN_TN)
    w_out = _column_tiles(w_out, OUT_TN)
    ffn_w_gate = _column_tiles(ffn_w_gate, FFN_TF)
    ffn_w_up = _column_tiles(ffn_w_up, FFN_TF)
    ffn_w_down = _row_tiles(ffn_w_down, FFN_TF)

    xf = x.reshape(rows, d)
    xa, xb = (xf, 0), (xf, half_tiles)
    for l in range(depth):
        last = l == depth - 1
        ffn_w = (ffn_w_gate, ffn_w_up, ffn_w_down, fg, l, last)
        p_a = _in_proj(*xa, half, g1, w_in, l)
        p_b, ret_a, lru_a = _in_proj(*xb, half, g1, w_in, l, mix=(p_a, mix_params))
        x1_a = _out_proj(*xa, ret_a, lru_a, w_out, l)
        x2_a, ret_b, lru_b = _ffn(x1_a, g2, *ffn_w, mix=(p_b, mix_params))
        x1_b = _out_proj(*xb, ret_b, lru_b, w_out, l)
        x2_b = _ffn(x1_b, g2, *ffn_w)
        xa, xb = (x2_a, 0), (x2_b, 0)
    return jnp.concatenate([xa[0], xb[0]], axis=0).reshape(batch, seq, d)
```

```python
import functools

import numpy as np
import jax
import jax.numpy as jnp
from jax import lax
from jax.experimental import pallas as pl
from jax.experimental.pallas import tpu as pltpu

D_MODEL = 2048
SEQ = 2048
RET_HEADS = 8
HEAD_DIM = 128
RET_WIDTH = RET_HEADS * HEAD_DIM
LRU_WIDTH = 1024
LRU_GROUPS = 16
LRU_GROUP_DIM = 64
CONV_W = 4
LRU_C = 8.0
IN_WIDTH = 4 * RET_WIDTH + 2 * LRU_WIDTH
CHUNK = 64
ROPE_BASE = 10000.0
EPS = 1e-6

F32 = jnp.float32
BF16 = jnp.bfloat16

SUBLANES = 8
BF16_ROWS = 16
TM = 1024
IN_TN = 768
FFN_TF = 256
OUT_TN = 1024
MIX_STEPS = IN_WIDTH // IN_TN
LRU_BLOCK = TM // MIX_STEPS
RET_GROUPS = 2
RET_GROUP_HEADS = RET_HEADS // RET_GROUPS
RET_GROUP_WIDTH = RET_GROUP_HEADS * HEAD_DIM
RET_BLOCK = LRU_BLOCK * RET_GROUPS
GATE_BLOCK = 256
VMEM_LIMIT = 52 * 1024 * 1024


def _params(*sem):
    return pltpu.CompilerParams(dimension_semantics=sem, vmem_limit_bytes=VMEM_LIMIT)


def _sigmoid(z):
    return 0.5 * jnp.tanh(0.5 * z) + 0.5


def _rms_scale(x, g):
    ms = jnp.mean(x * x, axis=-1, keepdims=True)
    return x * lax.rsqrt(ms + EPS) * g


def _retention_head(h, hg, q_ref, k_ref, v_ref, g_ref, cq_ref, sq_ref, ck_ref, sk_ref,
                    dmask_ref, qdec_ref, kdec_ref, sdec_ref, gn_ref, o_ref, s_ref):
    cq, sq, ck, sk = cq_ref[...], sq_ref[...], ck_ref[...], sk_ref[...]
    half = HEAD_DIM // 2
    sl = slice(h * HEAD_DIM, (h + 1) * HEAD_DIM)
    q = q_ref[:, sl].astype(F32)
    k = k_ref[:, sl].astype(F32)
    v = v_ref[:, sl]
    qr = q * cq + pltpu.roll(q, half, 1) * sq
    kr = k * ck + pltpu.roll(k, half, 1) * sk
    qb = qr.astype(BF16)
    kb = kr.astype(BF16)
    scores = lax.dot_general(qb, kb, (((1,), (1,)), ((), ())),
                             preferred_element_type=F32) * dmask_ref[h]
    o = jnp.dot(scores.astype(BF16), v, preferred_element_type=F32)
    sidx = hg * RET_GROUP_HEADS + h
    state = s_ref[sidx]
    o = o + qdec_ref[h] * jnp.dot(qb, state.astype(BF16), preferred_element_type=F32)
    kd = (kr * kdec_ref[h]).astype(BF16)
    s_ref[sidx] = sdec_ref[h] * state + lax.dot_general(
        kd, v, (((0,), (0,)), ((), ())), preferred_element_type=F32)
    mu = jnp.mean(o, axis=-1, keepdims=True)
    oc = o - mu
    var = jnp.mean(oc * oc, axis=-1, keepdims=True)
    on = oc * lax.rsqrt(var + EPS) * gn_ref[:, sl]
    gate = g_ref[:, sl].astype(F32)
    o_ref[:, sl] = (on * (gate * _sigmoid(gate))).astype(o_ref.dtype)


def _lru_cols(c, xb_ref, yb_ref, cw_ref, cb_ref, wa_ref, ba_ref, wx_ref, bx_ref, lam_ref,
              xpad_ref, a_ref, b_ref, y_ref, carry_ref):
    tt = xb_ref.shape[0]
    pad = SUBLANES
    sl = slice(c * GATE_BLOCK, (c + 1) * GATE_BLOCK)
    x = xb_ref[:, sl].astype(F32)
    xpad_ref[pad:pad + tt, sl] = x
    xc = cb_ref[:, sl] + cw_ref[CONV_W - 1:CONV_W, sl] * x
    for j in range(CONV_W - 1):
        off = pad - (CONV_W - 1) + j
        xc = xc + cw_ref[j:j + 1, sl] * xpad_ref[off:off + tt, sl]
    xpad_ref[0:pad, sl] = x[tt - pad:tt, :]

    lam = lam_ref[:, sl]
    softplus_neg_lam = jnp.maximum(-lam, 0.0) + jnp.log1p(jnp.exp(-jnp.abs(lam)))
    xc16 = xc.astype(BF16)
    r = _sigmoid(jnp.dot(xc16, wa_ref[c], preferred_element_type=F32) + ba_ref[:, sl])
    i = _sigmoid(jnp.dot(xc16, wx_ref[c], preferred_element_type=F32) + bx_ref[:, sl])
    neg_log_a = LRU_C * r * softplus_neg_lam
    a = jnp.exp(-neg_log_a)
    a_ref[:, sl] = a
    one_minus_a2 = jnp.tanh(neg_log_a) * (a * a + 1.0)
    root = jnp.where(one_minus_a2 > 0.0, one_minus_a2 * lax.rsqrt(one_minus_a2), 0.0)
    b_ref[:, sl] = root * (i * xc)

    row = lax.broadcasted_iota(jnp.int32, (SUBLANES, GATE_BLOCK), 0)
    carry = carry_ref[:, sl]
    for gidx in range(tt // SUBLANES):
        rows = slice(gidx * SUBLANES, (gidx + 1) * SUBLANES)
        a = a_ref[rows, sl]
        b = b_ref[rows, sl]
        for s in (1, 2, 4):
            keep = row >= s
            b = jnp.where(keep, a * pltpu.roll(b, s, 0) + b, b)
            a = jnp.where(keep, a * pltpu.roll(a, s, 0), a)
        h = a * carry + b
        b_ref[rows, sl] = h
        carry = h[SUBLANES - 1:SUBLANES, :]
    carry_ref[:, sl] = carry

    y = b_ref[:, sl] * jax.nn.gelu(yb_ref[:, sl].astype(F32))
    y_ref[:, sl] = y
    return y * y


def _lru_finish(ysq, ng_ref, o_ref, y_ref):
    ms = jnp.sum(ysq, axis=-1, keepdims=True) * (1.0 / LRU_WIDTH)
    o_ref[...] = (y_ref[...] * lax.rsqrt(ms + EPS) * ng_ref[...]).astype(o_ref.dtype)


N_MIX_IN = 23
N_MIX_OUT = 2
N_MIX_SCRATCH = 6
MIX_PIECES = LRU_WIDTH // GATE_BLOCK
assert MIX_PIECES == RET_GROUP_HEADS


def _row_chunks(n):
    step = TM // n // (2 * BF16_ROWS) * (2 * BF16_ROWS)
    return [slice(c * step, (c + 1) * step if c < n - 1 else TM) for c in range(n)]


MIX_CHUNKS = _row_chunks(MIX_PIECES + 1)


def _mixer_resets(s, active, mix_scratch):
    s_ref, xpad_ref, _, _, _, carry_ref = mix_scratch
    lru_blocks = SEQ // LRU_BLOCK
    hg = s % RET_GROUPS

    @pl.when(jnp.logical_and(active, s % lru_blocks == 0))
    def _():
        xpad_ref[0:SUBLANES, :] = jnp.zeros((SUBLANES, LRU_WIDTH), F32)
        carry_ref[...] = jnp.zeros_like(carry_ref)

    @pl.when(jnp.logical_and(active, (s // RET_GROUPS) % (SEQ // RET_BLOCK) == 0))
    def _():
        s_ref[pl.ds(hg * RET_GROUP_HEADS, RET_GROUP_HEADS)] = jnp.zeros(
            (RET_GROUP_HEADS, HEAD_DIM, HEAD_DIM), F32)


def _mixer_piece(c, s, ysq, mix_in, mix_out, mix_scratch):
    (q, k, v, g, xb, yb, cq, sq, ck, sk, dmask, qdec, kdec, sdec, gn,
     cw, cb, wa, ba, wx, bx, lam, ng) = mix_in
    o_ret, o_lru = mix_out
    s_ref, xpad_ref, a_ref, b_ref, y_ref, carry_ref = mix_scratch
    _retention_head(c, s % RET_GROUPS, q, k, v, g, cq, sq, ck, sk, dmask, qdec, kdec, sdec, gn,
                    o_ret, s_ref)
    ysq_c = _lru_cols(c, xb, yb, cw, cb, wa, ba, wx, bx, lam, xpad_ref, a_ref, b_ref, y_ref,
                      carry_ref)
    ysq = ysq_c if ysq is None else ysq + ysq_c
    if c == MIX_PIECES - 1:
        _lru_finish(ysq, ng, o_lru, y_ref)
    return ysq


def _mixer_specs(smap, layer):
    hg = lambda i, j: smap(i, j) % RET_GROUPS
    rblk = lambda i, j: smap(i, j) // RET_GROUPS
    tblk = lambda i, j: rblk(i, j) % (SEQ // RET_BLOCK)
    pcol = lambda c: pl.BlockSpec((RET_BLOCK, RET_GROUP_WIDTH),
                                  lambda i, j, c=c: (rblk(i, j), c * RET_GROUPS + hg(i, j)))
    lcol = lambda c: pl.BlockSpec((LRU_BLOCK, LRU_WIDTH), lambda i, j, c=c: (smap(i, j), c))
    table = pl.BlockSpec((RET_BLOCK, HEAD_DIM), lambda i, j: (tblk(i, j), 0))
    per_group = lambda *tail: pl.BlockSpec((RET_GROUP_HEADS,) + tail,
                                           lambda i, j: (hg(i, j),) + (0,) * len(tail))
    vec = pl.BlockSpec((None, 1, LRU_WIDTH), lambda i, j: (layer, 0, 0))
    gatew = pl.BlockSpec((None, LRU_WIDTH // GATE_BLOCK, GATE_BLOCK, GATE_BLOCK),
                         lambda i, j: (layer, 0, 0, 0))
    xcol = 4 * RET_WIDTH // LRU_WIDTH
    in_specs = [pcol(0), pcol(1), pcol(2), pcol(3), lcol(xcol), lcol(xcol + 1),
                table, table, table, table,
                per_group(RET_BLOCK, RET_BLOCK), per_group(RET_BLOCK, HEAD_DIM),
                per_group(RET_BLOCK, HEAD_DIM), per_group(1, HEAD_DIM),
                pl.BlockSpec((None, 1, RET_GROUP_WIDTH), lambda i, j: (layer, 0, hg(i, j))),
                pl.BlockSpec((None, CONV_W, LRU_WIDTH), lambda i, j: (layer, 0, 0)),
                vec, gatew, vec, gatew, vec, vec, vec]
    out_specs = [pl.BlockSpec((RET_BLOCK, RET_GROUP_WIDTH), lambda i, j: (rblk(i, j), hg(i, j))),
                 pl.BlockSpec((LRU_BLOCK, LRU_WIDTH), lambda i, j: (smap(i, j), 0))]
    scratch = [pltpu.VMEM((RET_HEADS, HEAD_DIM, HEAD_DIM), F32),
               pltpu.VMEM((LRU_BLOCK + SUBLANES, LRU_WIDTH), F32),
               pltpu.VMEM((LRU_BLOCK, LRU_WIDTH), F32),
               pltpu.VMEM((LRU_BLOCK, LRU_WIDTH), F32),
               pltpu.VMEM((LRU_BLOCK, LRU_WIDTH), F32),
               pltpu.VMEM((1, LRU_WIDTH), F32)]
    return in_specs, out_specs, scratch


def _mixer_out_shapes(rows):
    return [jax.ShapeDtypeStruct((rows, RET_WIDTH), BF16),
            jax.ShapeDtypeStruct((rows, LRU_WIDTH), BF16)]


def _mixer_operands(p, mix):
    return (p,) * 6 + tuple(mix)


def _in_proj_chunk(rows, wb, o_ref, h_ref):
    o_ref[rows, :] = jnp.dot(h_ref[rows, :], wb, preferred_element_type=F32).astype(o_ref.dtype)


def _in_proj_prologue(x_ref, g_ref, h_ref):
    @pl.when(pl.program_id(1) == 0)
    def _():
        h_ref[...] = _rms_scale(x_ref[...], g_ref[...]).astype(BF16)


def _in_proj_kernel(x_ref, g_ref, w_ref, o_ref, *rest):
    h_ref = rest[-1]
    _in_proj_prologue(x_ref, g_ref, h_ref)
    _in_proj_chunk(slice(0, TM), w_ref[...].astype(BF16), o_ref, h_ref)
    for z_ref in rest[:-1]:
        z_ref[...] = jnp.zeros_like(z_ref)


def _in_proj_mix_kernel(*refs):
    x_ref, g_ref, w_ref = refs[:3]
    mix_in = refs[3:3 + N_MIX_IN]
    o_ref = refs[3 + N_MIX_IN]
    mix_out = refs[4 + N_MIX_IN:4 + N_MIX_IN + N_MIX_OUT]
    h_ref = refs[-N_MIX_SCRATCH - 1]
    mix_scratch = refs[-N_MIX_SCRATCH:]
    s = pl.program_id(0) * MIX_STEPS + pl.program_id(1)
    _in_proj_prologue(x_ref, g_ref, h_ref)
    _mixer_resets(s, True, mix_scratch)
    wb = w_ref[...].astype(BF16)
    ysq = None
    for c in range(MIX_PIECES):
        _in_proj_chunk(MIX_CHUNKS[c], wb, o_ref, h_ref)
        ysq = _mixer_piece(c, s, ysq, mix_in, mix_out, mix_scratch)
    _in_proj_chunk(MIX_CHUNKS[MIX_PIECES], wb, o_ref, h_ref)


def _in_proj_specs(x_tile0, layer):
    d = D_MODEL
    in_specs = [pl.BlockSpec((TM, d), lambda i, j: (x_tile0 + i, 0), pipeline_mode=pl.Buffered(1)),
                pl.BlockSpec((None, 1, d), lambda i, j: (layer, 0, 0)),
                pl.BlockSpec((None, d, IN_TN), lambda i, j: (layer, 0, j))]
    out_spec = pl.BlockSpec((TM, IN_TN), lambda i, j: (i, j))
    return in_specs, out_spec


def _in_proj(x, x_tile0, rows, g, w, layer, mix=None, zero_rows=None):
    in_specs, out_spec = _in_proj_specs(x_tile0, layer)
    grid = (rows // TM, IN_WIDTH // IN_TN)
    p_shape = jax.ShapeDtypeStruct((rows, IN_WIDTH), BF16)
    h_scratch = pltpu.VMEM((TM, D_MODEL), BF16)
    if mix is None:
        out_specs, out_shape = [out_spec], [p_shape]
        if zero_rows is not None:
            slab = zero_rows // (grid[0] * grid[1])
            out_specs.append(pl.BlockSpec((slab, D_MODEL), lambda i, j: (i * grid[1] + j, 0)))
            out_shape.append(jax.ShapeDtypeStruct((zero_rows, D_MODEL), F32))
        outs = pl.pallas_call(
            _in_proj_kernel, grid=grid, in_specs=in_specs, out_specs=out_specs,
            out_shape=out_shape, scratch_shapes=[h_scratch],
            compiler_params=_params("parallel", "arbitrary"), name="in_proj",
        )(x, g, w)
        return outs[0] if zero_rows is None else outs
    p_other, mix_params = mix
    m_in, m_out, m_scratch = _mixer_specs(lambda i, j: i * MIX_STEPS + j, layer)
    return pl.pallas_call(
        _in_proj_mix_kernel, grid=grid,
        in_specs=in_specs + m_in, out_specs=[out_spec] + m_out,
        out_shape=[p_shape] + _mixer_out_shapes(p_other.shape[0]),
        scratch_shapes=[h_scratch] + m_scratch,
        compiler_params=_params("arbitrary", "arbitrary"), name="in_proj_mix",
    )(x, g, w, *_mixer_operands(p_other, mix_params))


def _out_proj_kernel(x_ref, r_ref, l_ref, w_ref, o_ref):
    acc = jnp.dot(r_ref[...], w_ref[0:RET_WIDTH, :].astype(BF16), preferred_element_type=F32)
    acc = acc + jnp.dot(l_ref[...], w_ref[RET_WIDTH:, :].astype(BF16), preferred_element_type=F32)
    o_ref[...] = x_ref[...] + acc


def _out_proj(x, x_tile0, o_ret, o_lru, w, layer):
    rows = o_ret.shape[0]
    d = D_MODEL
    return pl.pallas_call(
        _out_proj_kernel,
        grid=(d // OUT_TN, rows // TM),
        in_specs=[pl.BlockSpec((TM, OUT_TN), lambda j, i: (x_tile0 + i, j)),
                  pl.BlockSpec((TM, RET_WIDTH), lambda j, i: (i, 0)),
                  pl.BlockSpec((TM, LRU_WIDTH), lambda j, i: (i, 0)),
                  pl.BlockSpec((None, d, OUT_TN), lambda j, i: (layer, 0, j))],
        out_specs=pl.BlockSpec((TM, OUT_TN), lambda j, i: (i, j)),
        out_shape=jax.ShapeDtypeStruct((rows, d), F32),
        compiler_params=_params("parallel", "arbitrary"),
        name="out_proj",
    )(x, o_ret, o_lru, w)


def _ffn_prologue(x_hbm, g_ref, o_ref, h_ref):
    @pl.when(pl.program_id(1) == 0)
    def _():
        row0 = pl.multiple_of(pl.program_id(0) * TM, TM)
        pltpu.sync_copy(x_hbm.at[pl.ds(row0, TM), :], o_ref)
        h_ref[...] = _rms_scale(o_ref[...], g_ref[...]).astype(BF16)


def _ffn_chunk(rows, wg, wu, wd, o_ref, h_ref):
    h = h_ref[rows, :]
    gate = jnp.dot(h, wg, preferred_element_type=F32)
    up = jnp.dot(h, wu, preferred_element_type=F32)
    mid = (gate * _sigmoid(gate) * up).astype(BF16)
    o_ref[rows, :] += jnp.dot(mid, wd, preferred_element_type=F32)


def _ffn_weights(wg_ref, wu_ref, wd_ref):
    return wg_ref[...].astype(BF16), wu_ref[...].astype(BF16), wd_ref[...].astype(BF16)


def _ffn_epilogue(fg_ref, o_ref, final_norm):
    if final_norm:
        @pl.when(pl.program_id(1) == pl.num_programs(1) - 1)
        def _():
            o_ref[...] = _rms_scale(o_ref[...], fg_ref[...])


def _ffn_kernel(x_hbm, g_ref, wg_ref, wu_ref, wd_ref, fg_ref, *rest, final_norm):
    o_ref, h_ref = rest[-2:]
    _ffn_prologue(x_hbm, g_ref, o_ref, h_ref)
    _ffn_chunk(slice(0, TM), *_ffn_weights(wg_ref, wu_ref, wd_ref), o_ref, h_ref)
    _ffn_epilogue(fg_ref, o_ref, final_norm)


def _ffn_mix_kernel(*refs, final_norm, n_buf):
    x_hbm, g_ref, wg_ref, wu_ref, wd_ref, fg_ref = refs[:6]
    mix_in = refs[6:6 + N_MIX_IN]
    n_in = 6 + N_MIX_IN + n_buf
    o_ref = refs[n_in]
    mix_out = refs[n_in + 1:n_in + 1 + N_MIX_OUT]
    h_ref = refs[-N_MIX_SCRATCH - 1]
    mix_scratch = refs[-N_MIX_SCRATCH:]
    j = pl.program_id(1)
    active = j < MIX_STEPS
    s = pl.program_id(0) * MIX_STEPS + jnp.minimum(j, MIX_STEPS - 1)
    _ffn_prologue(x_hbm, g_ref, o_ref, h_ref)
    _mixer_resets(s, active, mix_scratch)

    @pl.when(active)
    def _():
        weights = _ffn_weights(wg_ref, wu_ref, wd_ref)
        ysq = None
        for c in range(MIX_PIECES):
            _ffn_chunk(MIX_CHUNKS[c], *weights, o_ref, h_ref)
            ysq = _mixer_piece(c, s, ysq, mix_in, mix_out, mix_scratch)
        _ffn_chunk(MIX_CHUNKS[MIX_PIECES], *weights, o_ref, h_ref)

    @pl.when(jnp.logical_not(active))
    def _():
        _ffn_chunk(slice(0, TM), *_ffn_weights(wg_ref, wu_ref, wd_ref), o_ref, h_ref)

    _ffn_epilogue(fg_ref, o_ref, final_norm)


def _ffn(x, g, wg, wu, wd, fg, layer, final_norm, mix=None, out_rows=None, out_tile0=0,
         out_buf=None):
    rows, d = x.shape
    out_rows = rows if out_rows is None else out_rows
    grid = (rows // TM, wg.shape[2] // FFN_TF)
    in_specs = [pl.BlockSpec(memory_space=pl.ANY),
                pl.BlockSpec((None, 1, d), lambda i, j: (layer, 0, 0)),
                pl.BlockSpec((None, d, FFN_TF), lambda i, j: (layer, 0, j)),
                pl.BlockSpec((None, d, FFN_TF), lambda i, j: (layer, 0, j)),
                pl.BlockSpec((None, FFN_TF, d), lambda i, j: (layer, j, 0)),
                pl.BlockSpec((1, d), lambda i, j: (0, 0))]
    out_spec = pl.BlockSpec((TM, d), lambda i, j: (out_tile0 + i, 0))
    out_shape = jax.ShapeDtypeStruct((out_rows, d), F32)
    h_scratch = pltpu.VMEM((TM, d), BF16)
    operands = (x, g, wg, wu, wd, fg)
    buf_specs, bufs = ([], ()) if out_buf is None else ([pl.BlockSpec(memory_space=pl.ANY)], (out_buf,))
    if mix is None:
        return pl.pallas_call(
            functools.partial(_ffn_kernel, final_norm=final_norm),
            grid=grid, in_specs=in_specs + buf_specs, out_specs=out_spec, out_shape=out_shape,
            scratch_shapes=[h_scratch],
            input_output_aliases={len(operands): 0} if bufs else {},
            compiler_params=_params("parallel", "arbitrary"),
            name="ffn_final" if final_norm else "ffn",
        )(*operands, *bufs)
    p_other, mix_params = mix
    m_in, m_out, m_scratch = _mixer_specs(
        lambda i, j: i * MIX_STEPS + jnp.minimum(j, MIX_STEPS - 1), layer)
    operands = operands + _mixer_operands(p_other, mix_params)
    return pl.pallas_call(
        functools.partial(_ffn_mix_kernel, final_norm=final_norm, n_buf=len(bufs)),
        grid=grid, in_specs=in_specs + m_in + buf_specs, out_specs=[out_spec] + m_out,
        out_shape=[out_shape] + _mixer_out_shapes(p_other.shape[0]),
        scratch_shapes=[h_scratch] + m_scratch,
        input_output_aliases={len(operands): 0} if bufs else {},
        compiler_params=_params("arbitrary", "arbitrary"),
        name="ffn_mix_final" if final_norm else "ffn_mix",
    )(*operands, *bufs)


def _retention_tables():
    cs = RET_BLOCK
    pos = np.arange(SEQ, dtype=np.float32)
    inv = (1.0 / (ROPE_BASE ** (np.arange(0, HEAD_DIM, 2, dtype=np.float32) / HEAD_DIM))).astype(np.float32)
    ang = jnp.asarray(pos[:, None] * inv[None, :])
    cos, sin = jnp.cos(ang), jnp.sin(ang)
    cfull = jnp.concatenate([cos, cos], axis=-1)
    ssign = jnp.concatenate([-sin, sin], axis=-1)
    kscale = HEAD_DIM ** -0.5
    gamma_log = np.log1p(-np.exp2(-5.0 - np.arange(RET_HEADS, dtype=np.float64)))
    idx = np.arange(cs)
    dist = np.abs(idx[:, None] - idx[None, :])
    visible = (idx[None, :] // CHUNK) <= (idx[:, None] // CHUNK)
    dmask = np.where(visible[None], np.exp(gamma_log[:, None, None] * dist[None]), 0.0)
    qdec = np.exp(gamma_log[:, None] * (idx + 1.0)[None, :])
    kdec = np.exp(gamma_log[:, None] * (cs - 1.0 - idx)[None, :])
    sdec = np.broadcast_to(np.exp(gamma_log * cs)[:, None, None], (RET_HEADS, 1, HEAD_DIM))
    bcast = lambda a: jnp.asarray(np.broadcast_to(a[:, :, None], (RET_HEADS, cs, HEAD_DIM)), F32)
    return (cfull, ssign, cfull * kscale, ssign * kscale,
            jnp.asarray(dmask, F32), bcast(qdec), bcast(kdec), jnp.asarray(sdec, F32))


def _block_diag(w):
    per = GATE_BLOCK // LRU_GROUP_DIM
    nb = LRU_GROUPS // per
    w4 = w.reshape(w.shape[0], nb, per, LRU_GROUP_DIM, LRU_GROUP_DIM)
    bd = jnp.einsum('lcipq,ij->lcipjq', w4, jnp.eye(per, dtype=w.dtype))
    return bd.reshape(w.shape[0], nb, GATE_BLOCK, GATE_BLOCK).astype(BF16)


def kernel(x, norm1_g, w_in, ret_gn_g, lru_conv_w, lru_conv_b, lru_wa, lru_ba, lru_wx, lru_bx,
           lru_lambda, lru_norm_g, w_out, norm2_g, ffn_w_gate, ffn_w_up, ffn_w_down, final_g):
    batch, seq, d = x.shape
    depth = w_in.shape[0]
    assert (seq, d) == (SEQ, D_MODEL) and batch % 2 == 0 and depth >= 1
    rows = batch * seq
    half = rows // 2
    half_tiles = half // TM
    vecs = lambda a: a.reshape(depth, 1, -1).astype(F32)
    mix_params = _retention_tables() + (
        vecs(ret_gn_g), lru_conv_w.astype(F32), vecs(lru_conv_b),
        _block_diag(lru_wa), vecs(lru_ba), _block_diag(lru_wx), vecs(lru_bx),
        vecs(lru_lambda), vecs(lru_norm_g))
    g1, g2, fg = vecs(norm1_g), vecs(norm2_g), final_g.reshape(1, d).astype(F32)

    xf = x.reshape(rows, d)
    xa, xb = (xf, 0), (xf, half_tiles)
    for l in range(depth):
        last = l == depth - 1
        ffn_w = (ffn_w_gate, ffn_w_up, ffn_w_down, fg, l, last)
        if last:
            p_a, out_buf = _in_proj(*xa, half, g1, w_in, l, zero_rows=rows)
            out_a = dict(out_rows=rows, out_buf=out_buf)
        else:
            p_a, out_a = _in_proj(*xa, half, g1, w_in, l), {}
        p_b, ret_a, lru_a = _in_proj(*xb, half, g1, w_in, l, mix=(p_a, mix_params))
        x1_a = _out_proj(*xa, ret_a, lru_a, w_out, l)
        x2_a, ret_b, lru_b = _ffn(x1_a, g2, *ffn_w, mix=(p_b, mix_params), **out_a)
        x1_b = _out_proj(*xb, ret_b, lru_b, w_out, l)
        if last:
            return _ffn(x1_b, g2, *ffn_w, out_rows=rows, out_tile0=half_tiles,
                        out_buf=x2_a).reshape(batch, seq, d)
        x2_b = _ffn(x1_b, g2, *ffn_w)
        xa, xb = (x2_a, 0), (x2_b, 0)
```

```python
import functools

import numpy as np
import jax
import jax.numpy as jnp
from jax import lax
from jax.experimental import pallas as pl
from jax.experimental.pallas import tpu as pltpu

D_MODEL = 2048
SEQ = 2048
RET_HEADS = 8
HEAD_DIM = 128
RET_WIDTH = RET_HEADS * HEAD_DIM
LRU_WIDTH = 1024
LRU_GROUPS = 16
LRU_GROUP_DIM = 64
CONV_W = 4
LRU_C = 8.0
IN_WIDTH = 4 * RET_WIDTH + 2 * LRU_WIDTH
CHUNK = 64
ROPE_BASE = 10000.0
EPS = 1e-6

F32 = jnp.float32
BF16 = jnp.bfloat16

SUBLANES = 8
BF16_ROWS = 16
TM = 1024
IN_TN = 768
FFN_TF = 256
OUT_TN = 1024
MIX_STEPS = IN_WIDTH // IN_TN
LRU_BLOCK = TM // MIX_STEPS
RET_GROUPS = 2
RET_GROUP_HEADS = RET_HEADS // RET_GROUPS
RET_GROUP_WIDTH = RET_GROUP_HEADS * HEAD_DIM
RET_BLOCK = LRU_BLOCK * RET_GROUPS
GATE_BLOCK = 256
VMEM_LIMIT = 52 * 1024 * 1024


def _params(*sem):
    return pltpu.CompilerParams(dimension_semantics=sem, vmem_limit_bytes=VMEM_LIMIT)


def _sigmoid(z):
    return 0.5 * jnp.tanh(0.5 * z) + 0.5


def _rms_scale(x, g):
    ms = jnp.mean(x * x, axis=-1, keepdims=True)
    return x * lax.rsqrt(ms + EPS) * g


def _retention_head(h, hg, q_ref, k_ref, v_ref, g_ref, cq_ref, sq_ref, ck_ref, sk_ref,
                    dmask_ref, qdec_ref, kdec_ref, sdec_ref, gn_ref, o_ref, s_ref):
    cq, sq, ck, sk = cq_ref[...], sq_ref[...], ck_ref[...], sk_ref[...]
    half = HEAD_DIM // 2
    sl = slice(h * HEAD_DIM, (h + 1) * HEAD_DIM)
    q = q_ref[:, sl].astype(F32)
    k = k_ref[:, sl].astype(F32)
    v = v_ref[:, sl]
    qr = q * cq + pltpu.roll(q, half, 1) * sq
    kr = k * ck + pltpu.roll(k, half, 1) * sk
    qb = qr.astype(BF16)
    kb = kr.astype(BF16)
    scores = lax.dot_general(qb, kb, (((1,), (1,)), ((), ())),
                             preferred_element_type=F32) * dmask_ref[h]
    o = jnp.dot(scores.astype(BF16), v, preferred_element_type=F32)
    sidx = hg * RET_GROUP_HEADS + h
    state = s_ref[sidx]
    o = o + qdec_ref[h] * jnp.dot(qb, state.astype(BF16), preferred_element_type=F32)
    kd = (kr * kdec_ref[h]).astype(BF16)
    s_ref[sidx] = sdec_ref[h] * state + lax.dot_general(
        kd, v, (((0,), (0,)), ((), ())), preferred_element_type=F32)
    mu = jnp.mean(o, axis=-1, keepdims=True)
    oc = o - mu
    var = jnp.mean(oc * oc, axis=-1, keepdims=True)
    on = oc * lax.rsqrt(var + EPS) * gn_ref[:, sl]
    gate = g_ref[:, sl].astype(F32)
    o_ref[:, sl] = (on * (gate * _sigmoid(gate))).astype(o_ref.dtype)


def _lru_cols(c, xb_ref, yb_ref, cw_ref, cb_ref, wa_ref, ba_ref, wx_ref, bx_ref, lam_ref,
              xpad_ref, a_ref, b_ref, y_ref, carry_ref):
    tt = xb_ref.shape[0]
    pad = SUBLANES
    sl = slice(c * GATE_BLOCK, (c + 1) * GATE_BLOCK)
    x = xb_ref[:, sl].astype(F32)
    xpad_ref[pad:pad + tt, sl] = x
    xc = cb_ref[:, sl] + cw_ref[CONV_W - 1:CONV_W, sl] * x
    for j in range(CONV_W - 1):
        off = pad - (CONV_W - 1) + j
        xc = xc + cw_ref[j:j + 1, sl] * xpad_ref[off:off + tt, sl]
    xpad_ref[0:pad, sl] = x[tt - pad:tt, :]

    lam = lam_ref[:, sl]
    softplus_neg_lam = jnp.maximum(-lam, 0.0) + jnp.log1p(jnp.exp(-jnp.abs(lam)))
    xc16 = xc.astype(BF16)
    r = _sigmoid(jnp.dot(xc16, wa_ref[c], preferred_element_type=F32) + ba_ref[:, sl])
    i = _sigmoid(jnp.dot(xc16, wx_ref[c], preferred_element_type=F32) + bx_ref[:, sl])
    neg_log_a = LRU_C * r * softplus_neg_lam
    a = jnp.exp(-neg_log_a)
    a_ref[:, sl] = a
    one_minus_a2 = jnp.tanh(neg_log_a) * (a * a + 1.0)
    root = jnp.where(one_minus_a2 > 0.0, one_minus_a2 * lax.rsqrt(one_minus_a2), 0.0)
    b_ref[:, sl] = root * (i * xc)

    row = lax.broadcasted_iota(jnp.int32, (SUBLANES, GATE_BLOCK), 0)
    carry = carry_ref[:, sl]
    for gidx in range(tt // SUBLANES):
        rows = slice(gidx * SUBLANES, (gidx + 1) * SUBLANES)
        a = a_ref[rows, sl]
        b = b_ref[rows, sl]
        for s in (1, 2, 4):
            keep = row >= s
            b = jnp.where(keep, a * pltpu.roll(b, s, 0) + b, b)
            a = jnp.where(keep, a * pltpu.roll(a, s, 0), a)
        h = a * carry + b
        b_ref[rows, sl] = h
        carry = h[SUBLANES - 1:SUBLANES, :]
    carry_ref[:, sl] = carry

    y = b_ref[:, sl] * jax.nn.gelu(yb_ref[:, sl].astype(F32))
    y_ref[:, sl] = y
    return y * y


def _lru_finish(ysq, ng_ref, o_ref, y_ref):
    ms = jnp.sum(ysq, axis=-1, keepdims=True) * (1.0 / LRU_WIDTH)
    o_ref[...] = (y_ref[...] * lax.rsqrt(ms + EPS) * ng_ref[...]).astype(o_ref.dtype)


N_MIX_IN = 23
N_MIX_OUT = 2
N_MIX_SCRATCH = 6
MIX_PIECES = LRU_WIDTH // GATE_BLOCK
assert MIX_PIECES == RET_GROUP_HEADS


def _row_chunks(n):
    step = TM // n // (2 * BF16_ROWS) * (2 * BF16_ROWS)
    return [slice(c * step, (c + 1) * step if c < n - 1 else TM) for c in range(n)]


MIX_CHUNKS = _row_chunks(MIX_PIECES + 1)
PLAIN_CHUNKS = _row_chunks(4)


def _mixer_resets(s, active, mix_scratch):
    s_ref, xpad_ref, _, _, _, carry_ref = mix_scratch
    lru_blocks = SEQ // LRU_BLOCK
    hg = s % RET_GROUPS

    @pl.when(jnp.logical_and(active, s % lru_blocks == 0))
    def _():
        xpad_ref[0:SUBLANES, :] = jnp.zeros((SUBLANES, LRU_WIDTH), F32)
        carry_ref[...] = jnp.zeros_like(carry_ref)

    @pl.when(jnp.logical_and(active, (s // RET_GROUPS) % (SEQ // RET_BLOCK) == 0))
    def _():
        s_ref[pl.ds(hg * RET_GROUP_HEADS, RET_GROUP_HEADS)] = jnp.zeros(
            (RET_GROUP_HEADS, HEAD_DIM, HEAD_DIM), F32)


def _mixer_piece(c, s, ysq, mix_in, mix_out, mix_scratch):
    (q, k, v, g, xb, yb, cq, sq, ck, sk, dmask, qdec, kdec, sdec, gn,
     cw, cb, wa, ba, wx, bx, lam, ng) = mix_in
    o_ret, o_lru = mix_out
    s_ref, xpad_ref, a_ref, b_ref, y_ref, carry_ref = mix_scratch
    _retention_head(c, s % RET_GROUPS, q, k, v, g, cq, sq, ck, sk, dmask, qdec, kdec, sdec, gn,
                    o_ret, s_ref)
    ysq_c = _lru_cols(c, xb, yb, cw, cb, wa, ba, wx, bx, lam, xpad_ref, a_ref, b_ref, y_ref,
                      carry_ref)
    ysq = ysq_c if ysq is None else ysq + ysq_c
    if c == MIX_PIECES - 1:
        _lru_finish(ysq, ng, o_lru, y_ref)
    return ysq


def _mixer_specs(smap, layer):
    hg = lambda i, j: smap(i, j) % RET_GROUPS
    rblk = lambda i, j: smap(i, j) // RET_GROUPS
    tblk = lambda i, j: rblk(i, j) % (SEQ // RET_BLOCK)
    pcol = lambda c: pl.BlockSpec((RET_BLOCK, RET_GROUP_WIDTH),
                                  lambda i, j, c=c: (rblk(i, j), c * RET_GROUPS + hg(i, j)))
    lcol = lambda c: pl.BlockSpec((LRU_BLOCK, LRU_WIDTH), lambda i, j, c=c: (smap(i, j), c))
    table = pl.BlockSpec((RET_BLOCK, HEAD_DIM), lambda i, j: (tblk(i, j), 0))
    per_group = lambda *tail: pl.BlockSpec((RET_GROUP_HEADS,) + tail,
                                           lambda i, j: (hg(i, j),) + (0,) * len(tail))
    vec = pl.BlockSpec((None, 1, LRU_WIDTH), lambda i, j: (layer, 0, 0))
    gatew = pl.BlockSpec((None, LRU_WIDTH // GATE_BLOCK, GATE_BLOCK, GATE_BLOCK),
                         lambda i, j: (layer, 0, 0, 0))
    xcol = 4 * RET_WIDTH // LRU_WIDTH
    in_specs = [pcol(0), pcol(1), pcol(2), pcol(3), lcol(xcol), lcol(xcol + 1),
                table, table, table, table,
                per_group(RET_BLOCK, RET_BLOCK), per_group(RET_BLOCK, HEAD_DIM),
                per_group(RET_BLOCK, HEAD_DIM), per_group(1, HEAD_DIM),
                pl.BlockSpec((None, 1, RET_GROUP_WIDTH), lambda i, j: (layer, 0, hg(i, j))),
                pl.BlockSpec((None, CONV_W, LRU_WIDTH), lambda i, j: (layer, 0, 0)),
                vec, gatew, vec, gatew, vec, vec, vec]
    out_specs = [pl.BlockSpec((RET_BLOCK, RET_GROUP_WIDTH), lambda i, j: (rblk(i, j), hg(i, j))),
                 pl.BlockSpec((LRU_BLOCK, LRU_WIDTH), lambda i, j: (smap(i, j), 0))]
    scratch = [pltpu.VMEM((RET_HEADS, HEAD_DIM, HEAD_DIM), F32),
               pltpu.VMEM((LRU_BLOCK + SUBLANES, LRU_WIDTH), F32),
               pltpu.VMEM((LRU_BLOCK, LRU_WIDTH), F32),
               pltpu.VMEM((LRU_BLOCK, LRU_WIDTH), F32),
               pltpu.VMEM((LRU_BLOCK, LRU_WIDTH), F32),
               pltpu.VMEM((1, LRU_WIDTH), F32)]
    return in_specs, out_specs, scratch


def _mixer_out_shapes(rows):
    return [jax.ShapeDtypeStruct((rows, RET_WIDTH), BF16),
            jax.ShapeDtypeStruct((rows, LRU_WIDTH), BF16)]


def _mixer_operands(p, mix):
    return (p,) * 6 + tuple(mix)


def _in_proj_chunk(rows, wb, o_ref, h_ref):
    o_ref[rows, :] = jnp.dot(h_ref[rows, :], wb, preferred_element_type=F32).astype(o_ref.dtype)


def _in_proj_prologue(x_ref, g_ref, h_ref):
    @pl.when(pl.program_id(1) == 0)
    def _():
        h_ref[...] = _rms_scale(x_ref[...], g_ref[...]).astype(BF16)


def _in_proj_kernel(x_ref, g_ref, w_ref, o_ref, *rest):
    h_ref = rest[-1]
    _in_proj_prologue(x_ref, g_ref, h_ref)
    _in_proj_chunk(slice(0, TM), w_ref[...].astype(BF16), o_ref, h_ref)
    for z_ref in rest[:-1]:
        z_ref[...] = jnp.zeros_like(z_ref)


def _in_proj_mix_kernel(*refs):
    x_ref, g_ref, w_ref = refs[:3]
    mix_in = refs[3:3 + N_MIX_IN]
    o_ref = refs[3 + N_MIX_IN]
    mix_out = refs[4 + N_MIX_IN:4 + N_MIX_IN + N_MIX_OUT]
    h_ref = refs[-N_MIX_SCRATCH - 1]
    mix_scratch = refs[-N_MIX_SCRATCH:]
    s = pl.program_id(0) * MIX_STEPS + pl.program_id(1)
    _in_proj_prologue(x_ref, g_ref, h_ref)
    _mixer_resets(s, True, mix_scratch)
    wb = w_ref[...].astype(BF16)
    ysq = None
    for c in range(MIX_PIECES):
        _in_proj_chunk(MIX_CHUNKS[c], wb, o_ref, h_ref)
        ysq = _mixer_piece(c, s, ysq, mix_in, mix_out, mix_scratch)
    _in_proj_chunk(MIX_CHUNKS[MIX_PIECES], wb, o_ref, h_ref)


def _in_proj_specs(x_tile0, layer):
    d = D_MODEL
    in_specs = [pl.BlockSpec((TM, d), lambda i, j: (x_tile0 + i, 0), pipeline_mode=pl.Buffered(1)),
                pl.BlockSpec((None, 1, d), lambda i, j: (layer, 0, 0)),
                pl.BlockSpec((None, d, IN_TN), lambda i, j: (layer, 0, j))]
    out_spec = pl.BlockSpec((TM, IN_TN), lambda i, j: (i, j))
    return in_specs, out_spec


def _in_proj(x, x_tile0, rows, g, w, layer, mix=None, zero_rows=None):
    in_specs, out_spec = _in_proj_specs(x_tile0, layer)
    grid = (rows // TM, IN_WIDTH // IN_TN)
    p_shape = jax.ShapeDtypeStruct((rows, IN_WIDTH), BF16)
    h_scratch = pltpu.VMEM((TM, D_MODEL), BF16)
    if mix is None:
        out_specs, out_shape = [out_spec], [p_shape]
        if zero_rows is not None:
            slab = zero_rows // (grid[0] * grid[1])
            out_specs.append(pl.BlockSpec((slab, D_MODEL), lambda i, j: (i * grid[1] + j, 0)))
            out_shape.append(jax.ShapeDtypeStruct((zero_rows, D_MODEL), F32))
        outs = pl.pallas_call(
            _in_proj_kernel, grid=grid, in_specs=in_specs, out_specs=out_specs,
            out_shape=out_shape, scratch_shapes=[h_scratch],
            compiler_params=_params("parallel", "arbitrary"), name="in_proj",
        )(x, g, w)
        return outs[0] if zero_rows is None else outs
    p_other, mix_params = mix
    m_in, m_out, m_scratch = _mixer_specs(lambda i, j: i * MIX_STEPS + j, layer)
    return pl.pallas_call(
        _in_proj_mix_kernel, grid=grid,
        in_specs=in_specs + m_in, out_specs=[out_spec] + m_out,
        out_shape=[p_shape] + _mixer_out_shapes(p_other.shape[0]),
        scratch_shapes=[h_scratch] + m_scratch,
        compiler_params=_params("arbitrary", "arbitrary"), name="in_proj_mix",
    )(x, g, w, *_mixer_operands(p_other, mix_params))


def _out_proj_kernel(x_ref, r_ref, l_ref, w_ref, o_ref):
    acc = jnp.dot(r_ref[...], w_ref[0:RET_WIDTH, :].astype(BF16), preferred_element_type=F32)
    acc = acc + jnp.dot(l_ref[...], w_ref[RET_WIDTH:, :].astype(BF16), preferred_element_type=F32)
    o_ref[...] = x_ref[...] + acc


def _out_proj(x, x_tile0, o_ret, o_lru, w, layer):
    rows = o_ret.shape[0]
    d = D_MODEL
    return pl.pallas_call(
        _out_proj_kernel,
        grid=(d // OUT_TN, rows // TM),
        in_specs=[pl.BlockSpec((TM, OUT_TN), lambda j, i: (x_tile0 + i, j)),
                  pl.BlockSpec((TM, RET_WIDTH), lambda j, i: (i, 0)),
                  pl.BlockSpec((TM, LRU_WIDTH), lambda j, i: (i, 0)),
                  pl.BlockSpec((None, d, OUT_TN), lambda j, i: (layer, 0, j))],
        out_specs=pl.BlockSpec((TM, OUT_TN), lambda j, i: (i, j)),
        out_shape=jax.ShapeDtypeStruct((rows, d), F32),
        compiler_params=_params("parallel", "arbitrary"),
        name="out_proj",
    )(x, o_ret, o_lru, w)


def _ffn_prologue(x_hbm, g_ref, o_ref, h_ref):
    @pl.when(pl.program_id(1) == 0)
    def _():
        row0 = pl.multiple_of(pl.program_id(0) * TM, TM)
        pltpu.sync_copy(x_hbm.at[pl.ds(row0, TM), :], o_ref)
        h_ref[...] = _rms_scale(o_ref[...], g_ref[...]).astype(BF16)


def _ffn_gate_up(rows, wg, wu, h_ref):
    h = h_ref[rows, :]
    return (jnp.dot(h, wg, preferred_element_type=F32), jnp.dot(h, wu, preferred_element_type=F32))


def _ffn_down(rows, gate_up, wd, o_ref):
    gate, up = gate_up
    mid = (gate * _sigmoid(gate) * up).astype(BF16)
    o_ref[rows, :] += jnp.dot(mid, wd, preferred_element_type=F32)


def _ffn_rows(chunks, weights, o_ref, h_ref, between=None, pipelined=True):
    wg, wu, wd = weights
    pending = None
    for c, rows in enumerate(chunks):
        gate_up = _ffn_gate_up(rows, wg, wu, h_ref)
        if not pipelined:
            _ffn_down(rows, gate_up, wd, o_ref)
        else:
            if pending is not None:
                _ffn_down(*pending, wd, o_ref)
            pending = (rows, gate_up)
        if between is not None and c < MIX_PIECES:
            between(c)
    if pending is not None:
        _ffn_down(*pending, wd, o_ref)


def _ffn_weights(wg_ref, wu_ref, wd_ref):
    return wg_ref[...].astype(BF16), wu_ref[...].astype(BF16), wd_ref[...].astype(BF16)


def _ffn_epilogue(fg_ref, o_ref, final_norm):
    if final_norm:
        @pl.when(pl.program_id(1) == pl.num_programs(1) - 1)
        def _():
            o_ref[...] = _rms_scale(o_ref[...], fg_ref[...])


def _ffn_kernel(x_hbm, g_ref, wg_ref, wu_ref, wd_ref, fg_ref, *rest, final_norm, variant):
    o_ref, h_ref = rest[-2:]
    _ffn_prologue(x_hbm, g_ref, o_ref, h_ref)
    chunks = PLAIN_CHUNKS if variant else [slice(0, TM)]
    _ffn_rows(chunks, _ffn_weights(wg_ref, wu_ref, wd_ref), o_ref, h_ref, pipelined=variant)
    _ffn_epilogue(fg_ref, o_ref, final_norm)


def _ffn_mix_kernel(*refs, final_norm, n_buf, variant):
    x_hbm, g_ref, wg_ref, wu_ref, wd_ref, fg_ref = refs[:6]
    mix_in = refs[6:6 + N_MIX_IN]
    n_in = 6 + N_MIX_IN + n_buf
    o_ref = refs[n_in]
    mix_out = refs[n_in + 1:n_in + 1 + N_MIX_OUT]
    h_ref = refs[-N_MIX_SCRATCH - 1]
    mix_scratch = refs[-N_MIX_SCRATCH:]
    j = pl.program_id(1)
    active = j < MIX_STEPS
    s = pl.program_id(0) * MIX_STEPS + jnp.minimum(j, MIX_STEPS - 1)
    _ffn_prologue(x_hbm, g_ref, o_ref, h_ref)
    _mixer_resets(s, active, mix_scratch)

    @pl.when(active)
    def _():
        ysq = [None]

        def piece(c):
            ysq[0] = _mixer_piece(c, s, ysq[0], mix_in, mix_out, mix_scratch)

        _ffn_rows(MIX_CHUNKS, _ffn_weights(wg_ref, wu_ref, wd_ref), o_ref, h_ref, between=piece,
                  pipelined=variant)

    @pl.when(jnp.logical_not(active))
    def _():
        chunks = PLAIN_CHUNKS if variant else [slice(0, TM)]
        _ffn_rows(chunks, _ffn_weights(wg_ref, wu_ref, wd_ref), o_ref, h_ref, pipelined=variant)

    _ffn_epilogue(fg_ref, o_ref, final_norm)


def _ffn(x, g, wg, wu, wd, fg, layer, final_norm, mix=None, out_rows=None, out_tile0=0,
         out_buf=None):
    rows, d = x.shape
    out_rows = rows if out_rows is None else out_rows
    grid = (rows // TM, wg.shape[2] // FFN_TF)
    in_specs = [pl.BlockSpec(memory_space=pl.ANY),
                pl.BlockSpec((None, 1, d), lambda i, j: (layer, 0, 0)),
                pl.BlockSpec((None, d, FFN_TF), lambda i, j: (layer, 0, j)),
                pl.BlockSpec((None, d, FFN_TF), lambda i, j: (layer, 0, j)),
                pl.BlockSpec((None, FFN_TF, d), lambda i, j: (layer, j, 0)),
                pl.BlockSpec((1, d), lambda i, j: (0, 0))]
    out_spec = pl.BlockSpec((TM, d), lambda i, j: (out_tile0 + i, 0))
    out_shape = jax.ShapeDtypeStruct((out_rows, d), F32)
    h_scratch = pltpu.VMEM((TM, d), BF16)
    operands = (x, g, wg, wu, wd, fg)
    buf_specs, bufs = ([], ()) if out_buf is None else ([pl.BlockSpec(memory_space=pl.ANY)], (out_buf,))
    if mix is None:
        return pl.pallas_call(
            functools.partial(_ffn_kernel, final_norm=final_norm, variant=final_norm),
            grid=grid, in_specs=in_specs + buf_specs, out_specs=out_spec, out_shape=out_shape,
            scratch_shapes=[h_scratch],
            input_output_aliases={len(operands): 0} if bufs else {},
            compiler_params=_params("parallel", "arbitrary"),
            name="ffn_final" if final_norm else "ffn",
        )(*operands, *bufs)
    p_other, mix_params = mix
    m_in, m_out, m_scratch = _mixer_specs(
        lambda i, j: i * MIX_STEPS + jnp.minimum(j, MIX_STEPS - 1), layer)
    operands = operands + _mixer_operands(p_other, mix_params)
    return pl.pallas_call(
        functools.partial(_ffn_mix_kernel, final_norm=final_norm, n_buf=len(bufs), variant=final_norm),
        grid=grid, in_specs=in_specs + m_in + buf_specs, out_specs=[out_spec] + m_out,
        out_shape=[out_shape] + _mixer_out_shapes(p_other.shape[0]),
        scratch_shapes=[h_scratch] + m_scratch,
        input_output_aliases={len(operands): 0} if bufs else {},
        compiler_params=_params("arbitrary", "arbitrary"),
        name="ffn_mix_final" if final_norm else "ffn_mix",
    )(*operands, *bufs)


def _retention_tables():
    cs = RET_BLOCK
    pos = np.arange(SEQ, dtype=np.float32)
    inv = (1.0 / (ROPE_BASE ** (np.arange(0, HEAD_DIM, 2, dtype=np.float32) / HEAD_DIM))).astype(np.float32)
    ang = jnp.asarray(pos[:, None] * inv[None, :])
    cos, sin = jnp.cos(ang), jnp.sin(ang)
    cfull = jnp.concatenate([cos, cos], axis=-1)
    ssign = jnp.concatenate([-sin, sin], axis=-1)
    kscale = HEAD_DIM ** -0.5
    gamma_log = np.log1p(-np.exp2(-5.0 - np.arange(RET_HEADS, dtype=np.float64)))
    idx = np.arange(cs)
    dist = np.abs(idx[:, None] - idx[None, :])
    visible = (idx[None, :] // CHUNK) <= (idx[:, None] // CHUNK)
    dmask = np.where(visible[None], np.exp(gamma_log[:, None, None] * dist[None]), 0.0)
    qdec = np.exp(gamma_log[:, None] * (idx + 1.0)[None, :])
    kdec = np.exp(gamma_log[:, None] * (cs - 1.0 - idx)[None, :])
    sdec = np.broadcast_to(np.exp(gamma_log * cs)[:, None, None], (RET_HEADS, 1, HEAD_DIM))
    bcast = lambda a: jnp.asarray(np.broadcast_to(a[:, :, None], (RET_HEADS, cs, HEAD_DIM)), F32)
    return (cfull, ssign, cfull * kscale, ssign * kscale,
            jnp.asarray(dmask, F32), bcast(qdec), bcast(kdec), jnp.asarray(sdec, F32))


def _block_diag(w):
    per = GATE_BLOCK // LRU_GROUP_DIM
    nb = LRU_GROUPS // per
    w4 = w.reshape(w.shape[0], nb, per, LRU_GROUP_DIM, LRU_GROUP_DIM)
    bd = jnp.einsum('lcipq,ij->lcipjq', w4, jnp.eye(per, dtype=w.dtype))
    return bd.reshape(w.shape[0], nb, GATE_BLOCK, GATE_BLOCK).astype(BF16)


def kernel(x, norm1_g, w_in, ret_gn_g, lru_conv_w, lru_conv_b, lru_wa, lru_ba, lru_wx, lru_bx,
           lru_lambda, lru_norm_g, w_out, norm2_g, ffn_w_gate, ffn_w_up, ffn_w_down, final_g):
    batch, seq, d = x.shape
    depth = w_in.shape[0]
    assert (seq, d) == (SEQ, D_MODEL) and batch % 2 == 0 and depth >= 1
    rows = batch * seq
    half = rows // 2
    half_tiles = half // TM
    vecs = lambda a: a.reshape(depth, 1, -1).astype(F32)
    mix_params = _retention_tables() + (
        vecs(ret_gn_g), lru_conv_w.astype(F32), vecs(lru_conv_b),
        _block_diag(lru_wa), vecs(lru_ba), _block_diag(lru_wx), vecs(lru_bx),
        vecs(lru_lambda), vecs(lru_norm_g))
    g1, g2, fg = vecs(norm1_g), vecs(norm2_g), final_g.reshape(1, d).astype(F32)

    xf = x.reshape(rows, d)
    xa, xb = (xf, 0), (xf, half_tiles)
    for l in range(depth):
        last = l == depth - 1
        ffn_w = (ffn_w_gate, ffn_w_up, ffn_w_down, fg, l, last)
        if last:
            p_a, out_buf = _in_proj(*xa, half, g1, w_in, l, zero_rows=rows)
            out_a = dict(out_rows=rows, out_buf=out_buf)
        else:
            p_a, out_a = _in_proj(*xa, half, g1, w_in, l), {}
        p_b, ret_a, lru_a = _in_proj(*xb, half, g1, w_in, l, mix=(p_a, mix_params))
        x1_a = _out_proj(*xa, ret_a, lru_a, w_out, l)
        x2_a, ret_b, lru_b = _ffn(x1_a, g2, *ffn_w, mix=(p_b, mix_params), **out_a)
        x1_b = _out_proj(*xb, ret_b, lru_b, w_out, l)
        if last:
            return _ffn(x1_b, g2, *ffn_w, out_rows=rows, out_tile0=half_tiles,
                        out_buf=x2_a).reshape(batch, seq, d)
        x2_b = _ffn(x1_b, g2, *ffn_w)
        xa, xb = (x2_a, 0), (x2_b, 0)
```

```python
import functools

import numpy as np
import jax
import jax.numpy as jnp
from jax import lax
from jax.experimental import pallas as pl
from jax.experimental.pallas import tpu as pltpu

D_MODEL = 2048
SEQ = 2048
RET_HEADS = 8
HEAD_DIM = 128
RET_WIDTH = RET_HEADS * HEAD_DIM
LRU_WIDTH = 1024
LRU_GROUPS = 16
LRU_GROUP_DIM = 64
CONV_W = 4
LRU_C = 8.0
IN_WIDTH = 4 * RET_WIDTH + 2 * LRU_WIDTH
CHUNK = 64
ROPE_BASE = 10000.0
EPS = 1e-6

F32 = jnp.float32
BF16 = jnp.bfloat16

SUBLANES = 8
BF16_ROWS = 16
TM = 1024
IN_TN = 768
IN_TN_PLAIN = 1024
FFN_TF = 256
OUT_TN = 1024
MIX_STEPS = IN_WIDTH // IN_TN
LRU_BLOCK = TM // MIX_STEPS
RET_GROUPS = 2
RET_GROUP_HEADS = RET_HEADS // RET_GROUPS
RET_GROUP_WIDTH = RET_GROUP_HEADS * HEAD_DIM
RET_BLOCK = LRU_BLOCK * RET_GROUPS
GATE_BLOCK = 256
VMEM_LIMIT = 52 * 1024 * 1024
VMEM_LIMIT_FFN = 57 * 1024 * 1024


def _params(*sem, vmem=VMEM_LIMIT):
    return pltpu.CompilerParams(dimension_semantics=sem, vmem_limit_bytes=vmem)


def _sigmoid(z):
    return 0.5 * jnp.tanh(0.5 * z) + 0.5


def _rms_scale(x, g):
    ms = jnp.mean(x * x, axis=-1, keepdims=True)
    return x * lax.rsqrt(ms + EPS) * g


def _retention_head(h, hg, q_ref, k_ref, v_ref, g_ref, cq_ref, sq_ref, ck_ref, sk_ref,
                    dmask_ref, qdec_ref, kdec_ref, sdec_ref, gn_ref, o_ref, s_ref):
    cq, sq, ck, sk = cq_ref[...], sq_ref[...], ck_ref[...], sk_ref[...]
    half = HEAD_DIM // 2
    sl = slice(h * HEAD_DIM, (h + 1) * HEAD_DIM)
    q = q_ref[:, sl].astype(F32)
    k = k_ref[:, sl].astype(F32)
    v = v_ref[:, sl]
    qr = q * cq + pltpu.roll(q, half, 1) * sq
    kr = k * ck + pltpu.roll(k, half, 1) * sk
    qb = qr.astype(BF16)
    kb = kr.astype(BF16)
    scores = lax.dot_general(qb, kb, (((1,), (1,)), ((), ())),
                             preferred_element_type=F32) * dmask_ref[h]
    o = jnp.dot(scores.astype(BF16), v, preferred_element_type=F32)
    sidx = hg * RET_GROUP_HEADS + h
    state = s_ref[sidx]
    o = o + qdec_ref[h] * jnp.dot(qb, state.astype(BF16), preferred_element_type=F32)
    kd = (kr * kdec_ref[h]).astype(BF16)
    s_ref[sidx] = sdec_ref[h] * state + lax.dot_general(
        kd, v, (((0,), (0,)), ((), ())), preferred_element_type=F32)
    mu = jnp.mean(o, axis=-1, keepdims=True)
    oc = o - mu
    var = jnp.mean(oc * oc, axis=-1, keepdims=True)
    on = oc * lax.rsqrt(var + EPS) * gn_ref[:, sl]
    gate = g_ref[:, sl].astype(F32)
    o_ref[:, sl] = (on * (gate * _sigmoid(gate))).astype(o_ref.dtype)


def _lru_cols(c, xb_ref, yb_ref, cw_ref, cb_ref, wa_ref, ba_ref, wx_ref, bx_ref, lam_ref,
              xpad_ref, a_ref, b_ref, y_ref, carry_ref):
    tt = xb_ref.shape[0]
    pad = SUBLANES
    sl = slice(c * GATE_BLOCK, (c + 1) * GATE_BLOCK)
    x = xb_ref[:, sl].astype(F32)
    xpad_ref[pad:pad + tt, sl] = x
    xc = cb_ref[:, sl] + cw_ref[CONV_W - 1:CONV_W, sl] * x
    for j in range(CONV_W - 1):
        off = pad - (CONV_W - 1) + j
        xc = xc + cw_ref[j:j + 1, sl] * xpad_ref[off:off + tt, sl]
    xpad_ref[0:pad, sl] = x[tt - pad:tt, :]

    lam = lam_ref[:, sl]
    softplus_neg_lam = jnp.maximum(-lam, 0.0) + jnp.log1p(jnp.exp(-jnp.abs(lam)))
    xc16 = xc.astype(BF16)
    r = _sigmoid(jnp.dot(xc16, wa_ref[c], preferred_element_type=F32) + ba_ref[:, sl])
    i = _sigmoid(jnp.dot(xc16, wx_ref[c], preferred_element_type=F32) + bx_ref[:, sl])
    neg_log_a = LRU_C * r * softplus_neg_lam
    a = jnp.exp(-neg_log_a)
    a_ref[:, sl] = a
    one_minus_a2 = jnp.tanh(neg_log_a) * (a * a + 1.0)
    root = jnp.where(one_minus_a2 > 0.0, one_minus_a2 * lax.rsqrt(one_minus_a2), 0.0)
    b_ref[:, sl] = root * (i * xc)

    row = lax.broadcasted_iota(jnp.int32, (SUBLANES, GATE_BLOCK), 0)
    carry = carry_ref[:, sl]
    for gidx in range(tt // SUBLANES):
        rows = slice(gidx * SUBLANES, (gidx + 1) * SUBLANES)
        a = a_ref[rows, sl]
        b = b_ref[rows, sl]
        for s in (1, 2, 4):
            keep = row >= s
            b = jnp.where(keep, a * pltpu.roll(b, s, 0) + b, b)
            a = jnp.where(keep, a * pltpu.roll(a, s, 0), a)
        h = a * carry + b
        b_ref[rows, sl] = h
        carry = h[SUBLANES - 1:SUBLANES, :]
    carry_ref[:, sl] = carry

    y = b_ref[:, sl] * jax.nn.gelu(yb_ref[:, sl].astype(F32))
    y_ref[:, sl] = y
    return y * y


def _lru_finish(ysq, ng_ref, o_ref, y_ref):
    ms = jnp.sum(ysq, axis=-1, keepdims=True) * (1.0 / LRU_WIDTH)
    o_ref[...] = (y_ref[...] * lax.rsqrt(ms + EPS) * ng_ref[...]).astype(o_ref.dtype)


N_MIX_IN = 23
N_MIX_OUT = 2
N_MIX_SCRATCH = 6
MIX_PIECES = LRU_WIDTH // GATE_BLOCK
assert MIX_PIECES == RET_GROUP_HEADS


def _row_chunks(n):
    step = TM // n // (2 * BF16_ROWS) * (2 * BF16_ROWS)
    return [slice(c * step, (c + 1) * step if c < n - 1 else TM) for c in range(n)]


MIX_CHUNKS = _row_chunks(MIX_PIECES + 1)


def _mixer_resets(s, active, mix_scratch):
    s_ref, xpad_ref, _, _, _, carry_ref = mix_scratch
    lru_blocks = SEQ // LRU_BLOCK
    hg = s % RET_GROUPS

    @pl.when(jnp.logical_and(active, s % lru_blocks == 0))
    def _():
        xpad_ref[0:SUBLANES, :] = jnp.zeros((SUBLANES, LRU_WIDTH), F32)
        carry_ref[...] = jnp.zeros_like(carry_ref)

    @pl.when(jnp.logical_and(active, (s // RET_GROUPS) % (SEQ // RET_BLOCK) == 0))
    def _():
        s_ref[pl.ds(hg * RET_GROUP_HEADS, RET_GROUP_HEADS)] = jnp.zeros(
            (RET_GROUP_HEADS, HEAD_DIM, HEAD_DIM), F32)


def _mixer_piece(c, s, ysq, mix_in, mix_out, mix_scratch):
    (q, k, v, g, xb, yb, cq, sq, ck, sk, dmask, qdec, kdec, sdec, gn,
     cw, cb, wa, ba, wx, bx, lam, ng) = mix_in
    o_ret, o_lru = mix_out
    s_ref, xpad_ref, a_ref, b_ref, y_ref, carry_ref = mix_scratch
    _retention_head(c, s % RET_GROUPS, q, k, v, g, cq, sq, ck, sk, dmask, qdec, kdec, sdec, gn,
                    o_ret, s_ref)
    ysq_c = _lru_cols(c, xb, yb, cw, cb, wa, ba, wx, bx, lam, xpad_ref, a_ref, b_ref, y_ref,
                      carry_ref)
    ysq = ysq_c if ysq is None else ysq + ysq_c
    if c == MIX_PIECES - 1:
        _lru_finish(ysq, ng, o_lru, y_ref)
    return ysq


def _mixer_specs(smap, layer):
    hg = lambda i, j: smap(i, j) % RET_GROUPS
    rblk = lambda i, j: smap(i, j) // RET_GROUPS
    tblk = lambda i, j: rblk(i, j) % (SEQ // RET_BLOCK)
    pcol = lambda c: pl.BlockSpec((RET_BLOCK, RET_GROUP_WIDTH),
                                  lambda i, j, c=c: (rblk(i, j), c * RET_GROUPS + hg(i, j)))
    lcol = lambda c: pl.BlockSpec((LRU_BLOCK, LRU_WIDTH), lambda i, j, c=c: (smap(i, j), c))
    table = pl.BlockSpec((RET_BLOCK, HEAD_DIM), lambda i, j: (tblk(i, j), 0))
    per_group = lambda *tail: pl.BlockSpec((RET_GROUP_HEADS,) + tail,
                                           lambda i, j: (hg(i, j),) + (0,) * len(tail))
    vec = pl.BlockSpec((None, 1, LRU_WIDTH), lambda i, j: (layer, 0, 0))
    gatew = pl.BlockSpec((None, LRU_WIDTH // GATE_BLOCK, GATE_BLOCK, GATE_BLOCK),
                         lambda i, j: (layer, 0, 0, 0))
    xcol = 4 * RET_WIDTH // LRU_WIDTH
    in_specs = [pcol(0), pcol(1), pcol(2), pcol(3), lcol(xcol), lcol(xcol + 1),
                table, table, table, table,
                per_group(RET_BLOCK, RET_BLOCK), per_group(RET_BLOCK, HEAD_DIM),
                per_group(RET_BLOCK, HEAD_DIM), per_group(1, HEAD_DIM),
                pl.BlockSpec((None, 1, RET_GROUP_WIDTH), lambda i, j: (layer, 0, hg(i, j))),
                pl.BlockSpec((None, CONV_W, LRU_WIDTH), lambda i, j: (layer, 0, 0)),
                vec, gatew, vec, gatew, vec, vec, vec]
    out_specs = [pl.BlockSpec((RET_BLOCK, RET_GROUP_WIDTH), lambda i, j: (rblk(i, j), hg(i, j))),
                 pl.BlockSpec((LRU_BLOCK, LRU_WIDTH), lambda i, j: (smap(i, j), 0))]
    scratch = [pltpu.VMEM((RET_HEADS, HEAD_DIM, HEAD_DIM), F32),
               pltpu.VMEM((LRU_BLOCK + SUBLANES, LRU_WIDTH), F32),
               pltpu.VMEM((LRU_BLOCK, LRU_WIDTH), F32),
               pltpu.VMEM((LRU_BLOCK, LRU_WIDTH), F32),
               pltpu.VMEM((LRU_BLOCK, LRU_WIDTH), F32),
               pltpu.VMEM((1, LRU_WIDTH), F32)]
    return in_specs, out_specs, scratch


def _mixer_out_shapes(rows):
    return [jax.ShapeDtypeStruct((rows, RET_WIDTH), BF16),
            jax.ShapeDtypeStruct((rows, LRU_WIDTH), BF16)]


def _mixer_operands(p, mix):
    return (p,) * 6 + tuple(mix)


def _in_proj_chunk(rows, wb, o_ref, h_ref):
    o_ref[rows, :] = jnp.dot(h_ref[rows, :], wb, preferred_element_type=F32).astype(o_ref.dtype)


def _in_proj_prologue(x_ref, g_ref, h_ref):
    @pl.when(pl.program_id(1) == 0)
    def _():
        h_ref[...] = _rms_scale(x_ref[...], g_ref[...]).astype(BF16)


def _in_proj_kernel(x_ref, g_ref, w_ref, o_ref, *rest):
    h_ref = rest[-1]
    _in_proj_prologue(x_ref, g_ref, h_ref)
    _in_proj_chunk(slice(0, TM), w_ref[...].astype(BF16), o_ref, h_ref)
    for z_ref in rest[:-1]:
        z_ref[...] = jnp.zeros_like(z_ref)


def _in_proj_mix_kernel(*refs):
    x_ref, g_ref, w_ref = refs[:3]
    mix_in = refs[3:3 + N_MIX_IN]
    o_ref = refs[3 + N_MIX_IN]
    mix_out = refs[4 + N_MIX_IN:4 + N_MIX_IN + N_MIX_OUT]
    h_ref = refs[-N_MIX_SCRATCH - 1]
    mix_scratch = refs[-N_MIX_SCRATCH:]
    s = pl.program_id(0) * MIX_STEPS + pl.program_id(1)
    _in_proj_prologue(x_ref, g_ref, h_ref)
    _mixer_resets(s, True, mix_scratch)
    wb = w_ref[...].astype(BF16)
    ysq = None
    for c in range(MIX_PIECES):
        _in_proj_chunk(MIX_CHUNKS[c], wb, o_ref, h_ref)
        ysq = _mixer_piece(c, s, ysq, mix_in, mix_out, mix_scratch)
    _in_proj_chunk(MIX_CHUNKS[MIX_PIECES], wb, o_ref, h_ref)


def _in_proj_specs(x_tile0, layer, tn):
    d = D_MODEL
    in_specs = [pl.BlockSpec((TM, d), lambda i, j: (x_tile0 + i, 0)),
                pl.BlockSpec((None, 1, d), lambda i, j: (layer, 0, 0)),
                pl.BlockSpec((None, d, tn), lambda i, j: (layer, 0, j))]
    out_spec = pl.BlockSpec((TM, tn), lambda i, j: (i, j))
    return in_specs, out_spec


def _in_proj(x, x_tile0, rows, g, w, layer, mix=None, zero_rows=None):
    tn = IN_TN if (mix is not None or zero_rows is not None) else IN_TN_PLAIN
    in_specs, out_spec = _in_proj_specs(x_tile0, layer, tn)
    grid = (rows // TM, IN_WIDTH // tn)
    p_shape = jax.ShapeDtypeStruct((rows, IN_WIDTH), BF16)
    h_scratch = pltpu.VMEM((TM, D_MODEL), BF16)
    if mix is None:
        out_specs, out_shape = [out_spec], [p_shape]
        if zero_rows is not None:
            slab = zero_rows // (grid[0] * grid[1])
            out_specs.append(pl.BlockSpec((slab, D_MODEL), lambda i, j: (i * grid[1] + j, 0)))
            out_shape.append(jax.ShapeDtypeStruct((zero_rows, D_MODEL), F32))
        outs = pl.pallas_call(
            _in_proj_kernel, grid=grid, in_specs=in_specs, out_specs=out_specs,
            out_shape=out_shape, scratch_shapes=[h_scratch],
            compiler_params=_params("parallel", "arbitrary"), name="in_proj",
        )(x, g, w)
        return outs[0] if zero_rows is None else outs
    p_other, mix_params = mix
    m_in, m_out, m_scratch = _mixer_specs(lambda i, j: i * MIX_STEPS + j, layer)
    return pl.pallas_call(
        _in_proj_mix_kernel, grid=grid,
        in_specs=in_specs + m_in, out_specs=[out_spec] + m_out,
        out_shape=[p_shape] + _mixer_out_shapes(p_other.shape[0]),
        scratch_shapes=[h_scratch] + m_scratch,
        compiler_params=_params("arbitrary", "arbitrary"), name="in_proj_mix",
    )(x, g, w, *_mixer_operands(p_other, mix_params))


def _out_proj_kernel(x_ref, r_ref, l_ref, w_ref, o_ref):
    acc = jnp.dot(r_ref[...], w_ref[0:RET_WIDTH, :].astype(BF16), preferred_element_type=F32)
    acc = acc + jnp.dot(l_ref[...], w_ref[RET_WIDTH:, :].astype(BF16), preferred_element_type=F32)
    o_ref[...] = x_ref[...] + acc


def _out_proj(x, x_tile0, o_ret, o_lru, w, layer):
    rows = o_ret.shape[0]
    d = D_MODEL
    return pl.pallas_call(
        _out_proj_kernel,
        grid=(d // OUT_TN, rows // TM),
        in_specs=[pl.BlockSpec((TM, OUT_TN), lambda j, i: (x_tile0 + i, j)),
                  pl.BlockSpec((TM, RET_WIDTH), lambda j, i: (i, 0)),
                  pl.BlockSpec((TM, LRU_WIDTH), lambda j, i: (i, 0)),
                  pl.BlockSpec((None, d, OUT_TN), lambda j, i: (layer, 0, j))],
        out_specs=pl.BlockSpec((TM, OUT_TN), lambda j, i: (i, j)),
        out_shape=jax.ShapeDtypeStruct((rows, d), F32),
        compiler_params=_params("parallel", "arbitrary"),
        name="out_proj",
    )(x, o_ret, o_lru, w)


def _ffn_prologue(x_ref, g_ref, o_ref, h_ref, x_in_hbm):
    @pl.when(pl.program_id(1) == 0)
    def _():
        if x_in_hbm:
            row0 = pl.multiple_of(pl.program_id(0) * TM, TM)
            pltpu.sync_copy(x_ref.at[pl.ds(row0, TM), :], o_ref)
        else:
            o_ref[...] = x_ref[...]
        h_ref[...] = _rms_scale(o_ref[...], g_ref[...]).astype(BF16)


def _ffn_gate_up(rows, wg, wu, h_ref):
    h = h_ref[rows, :]
    return (jnp.dot(h, wg, preferred_element_type=F32), jnp.dot(h, wu, preferred_element_type=F32))


def _ffn_down(rows, gate_up, wd, o_ref):
    gate, up = gate_up
    mid = (gate * _sigmoid(gate) * up).astype(BF16)
    o_ref[rows, :] += jnp.dot(mid, wd, preferred_element_type=F32)


def _ffn_rows(chunks, weights, o_ref, h_ref, between=None):
    wg, wu, wd = weights
    pending = None
    for c, rows in enumerate(chunks):
        gate_up = _ffn_gate_up(rows, wg, wu, h_ref)
        if pending is not None:
            _ffn_down(*pending, wd, o_ref)
        pending = (rows, gate_up)
        if between is not None and c < MIX_PIECES:
            between(c)
    _ffn_down(*pending, wd, o_ref)


def _ffn_weights(wg_ref, wu_ref, wd_ref):
    return wg_ref[...].astype(BF16), wu_ref[...].astype(BF16), wd_ref[...].astype(BF16)


def _ffn_epilogue(fg_ref, o_ref, final_norm):
    if final_norm:
        @pl.when(pl.program_id(1) == pl.num_programs(1) - 1)
        def _():
            o_ref[...] = _rms_scale(o_ref[...], fg_ref[...])


def _ffn_kernel(x_ref, g_ref, wg_ref, wu_ref, wd_ref, fg_ref, *rest, final_norm):
    o_ref, h_ref = rest[-2:]
    _ffn_prologue(x_ref, g_ref, o_ref, h_ref, x_in_hbm=False)
    _ffn_rows([slice(0, TM)], _ffn_weights(wg_ref, wu_ref, wd_ref), o_ref, h_ref)
    _ffn_epilogue(fg_ref, o_ref, final_norm)


def _ffn_mix_kernel(*refs, final_norm, n_buf):
    x_hbm, g_ref, wg_ref, wu_ref, wd_ref, fg_ref = refs[:6]
    mix_in = refs[6:6 + N_MIX_IN]
    n_in = 6 + N_MIX_IN + n_buf
    o_ref = refs[n_in]
    mix_out = refs[n_in + 1:n_in + 1 + N_MIX_OUT]
    h_ref = refs[-N_MIX_SCRATCH - 1]
    mix_scratch = refs[-N_MIX_SCRATCH:]
    j = pl.program_id(1)
    active = j < MIX_STEPS
    s = pl.program_id(0) * MIX_STEPS + jnp.minimum(j, MIX_STEPS - 1)
    _ffn_prologue(x_hbm, g_ref, o_ref, h_ref, x_in_hbm=True)
    _mixer_resets(s, active, mix_scratch)

    @pl.when(active)
    def _():
        ysq = [None]

        def piece(c):
            ysq[0] = _mixer_piece(c, s, ysq[0], mix_in, mix_out, mix_scratch)

        _ffn_rows(MIX_CHUNKS, _ffn_weights(wg_ref, wu_ref, wd_ref), o_ref, h_ref, between=piece)

    @pl.when(jnp.logical_not(active))
    def _():
        _ffn_rows([slice(0, TM)], _ffn_weights(wg_ref, wu_ref, wd_ref), o_ref, h_ref)

    _ffn_epilogue(fg_ref, o_ref, final_norm)


def _ffn(x, g, wg, wu, wd, fg, layer, final_norm, mix=None, out_rows=None, out_tile0=0,
         out_buf=None):
    rows, d = x.shape
    out_rows = rows if out_rows is None else out_rows
    grid = (rows // TM, wg.shape[2] // FFN_TF)
    x_spec = pl.BlockSpec((TM, d), lambda i, j: (i, 0)) if mix is None else pl.BlockSpec(memory_space=pl.ANY)
    in_specs = [x_spec,
                pl.BlockSpec((None, 1, d), lambda i, j: (layer, 0, 0)),
                pl.BlockSpec((None, d, FFN_TF), lambda i, j: (layer, 0, j)),
                pl.BlockSpec((None, d, FFN_TF), lambda i, j: (layer, 0, j)),
                pl.BlockSpec((None, FFN_TF, d), lambda i, j: (layer, j, 0)),
                pl.BlockSpec((1, d), lambda i, j: (0, 0))]
    out_spec = pl.BlockSpec((TM, d), lambda i, j: (out_tile0 + i, 0))
    out_shape = jax.ShapeDtypeStruct((out_rows, d), F32)
    h_scratch = pltpu.VMEM((TM, d), BF16)
    operands = (x, g, wg, wu, wd, fg)
    buf_specs, bufs = ([], ()) if out_buf is None else ([pl.BlockSpec(memory_space=pl.ANY)], (out_buf,))
    if mix is None:
        return pl.pallas_call(
            functools.partial(_ffn_kernel, final_norm=final_norm),
            grid=grid, in_specs=in_specs + buf_specs, out_specs=out_spec, out_shape=out_shape,
            scratch_shapes=[h_scratch],
            input_output_aliases={len(operands): 0} if bufs else {},
            compiler_params=_params("parallel", "arbitrary", vmem=VMEM_LIMIT_FFN),
            name="ffn_final" if final_norm else "ffn",
        )(*operands, *bufs)
    p_other, mix_params = mix
    m_in, m_out, m_scratch = _mixer_specs(
        lambda i, j: i * MIX_STEPS + jnp.minimum(j, MIX_STEPS - 1), layer)
    operands = operands + _mixer_operands(p_other, mix_params)
    return pl.pallas_call(
        functools.partial(_ffn_mix_kernel, final_norm=final_norm, n_buf=len(bufs)),
        grid=grid, in_specs=in_specs + m_in + buf_specs, out_specs=[out_spec] + m_out,
        out_shape=[out_shape] + _mixer_out_shapes(p_other.shape[0]),
        scratch_shapes=[h_scratch] + m_scratch,
        input_output_aliases={len(operands): 0} if bufs else {},
        compiler_params=_params("arbitrary", "arbitrary"),
        name="ffn_mix_final" if final_norm else "ffn_mix",
    )(*operands, *bufs)


def _retention_tables():
    cs = RET_BLOCK
    pos = np.arange(SEQ, dtype=np.float32)
    inv = (1.0 / (ROPE_BASE ** (np.arange(0, HEAD_DIM, 2, dtype=np.float32) / HEAD_DIM))).astype(np.float32)
    ang = jnp.asarray(pos[:, None] * inv[None, :])
    cos, sin = jnp.cos(ang), jnp.sin(ang)
    cfull = jnp.concatenate([cos, cos], axis=-1)
    ssign = jnp.concatenate([-sin, sin], axis=-1)
    kscale = HEAD_DIM ** -0.5
    gamma_log = np.log1p(-np.exp2(-5.0 - np.arange(RET_HEADS, dtype=np.float64)))
    idx = np.arange(cs)
    dist = np.abs(idx[:, None] - idx[None, :])
    visible = (idx[None, :] // CHUNK) <= (idx[:, None] // CHUNK)
    dmask = np.where(visible[None], np.exp(gamma_log[:, None, None] * dist[None]), 0.0)
    qdec = np.exp(gamma_log[:, None] * (idx + 1.0)[None, :])
    kdec = np.exp(gamma_log[:, None] * (cs - 1.0 - idx)[None, :])
    sdec = np.broadcast_to(np.exp(gamma_log * cs)[:, None, None], (RET_HEADS, 1, HEAD_DIM))
    bcast = lambda a: jnp.asarray(np.broadcast_to(a[:, :, None], (RET_HEADS, cs, HEAD_DIM)), F32)
    return (cfull, ssign, cfull * kscale, ssign * kscale,
            jnp.asarray(dmask, F32), bcast(qdec), bcast(kdec), jnp.asarray(sdec, F32))


def _block_diag(w):
    per = GATE_BLOCK // LRU_GROUP_DIM
    nb = LRU_GROUPS // per
    w4 = w.reshape(w.shape[0], nb, per, LRU_GROUP_DIM, LRU_GROUP_DIM)
    bd = jnp.einsum('lcipq,ij->lcipjq', w4, jnp.eye(per, dtype=w.dtype))
    return bd.reshape(w.shape[0], nb, GATE_BLOCK, GATE_BLOCK).astype(BF16)


def kernel(x, norm1_g, w_in, ret_gn_g, lru_conv_w, lru_conv_b, lru_wa, lru_ba, lru_wx, lru_bx,
           lru_lambda, lru_norm_g, w_out, norm2_g, ffn_w_gate, ffn_w_up, ffn_w_down, final_g):
    batch, seq, d = x.shape
    depth = w_in.shape[0]
    assert (seq, d) == (SEQ, D_MODEL) and batch % 2 == 0 and depth >= 1
    rows = batch * seq
    half = rows // 2
    half_tiles = half // TM
    vecs = lambda a: a.reshape(depth, 1, -1).astype(F32)
    mix_params = _retention_tables() + (
        vecs(ret_gn_g), lru_conv_w.astype(F32), vecs(lru_conv_b),
        _block_diag(lru_wa), vecs(lru_ba), _block_diag(lru_wx), vecs(lru_bx),
        vecs(lru_lambda), vecs(lru_norm_g))
    g1, g2, fg = vecs(norm1_g), vecs(norm2_g), final_g.reshape(1, d).astype(F32)

    xf = x.reshape(rows, d)
    xa, xb = (xf, 0), (xf, half_tiles)
    for l in range(depth):
        last = l == depth - 1
        ffn_w = (ffn_w_gate, ffn_w_up, ffn_w_down, fg, l, last)
        if last:
            p_a, out_buf = _in_proj(*xa, half, g1, w_in, l, zero_rows=rows)
            out_a = dict(out_rows=rows, out_buf=out_buf)
        else:
            p_a, out_a = _in_proj(*xa, half, g1, w_in, l), {}
        p_b, ret_a, lru_a = _in_proj(*xb, half, g1, w_in, l, mix=(p_a, mix_params))
        x1_a = _out_proj(*xa, ret_a, lru_a, w_out, l)
        x2_a, ret_b, lru_b = _ffn(x1_a, g2, *ffn_w, mix=(p_b, mix_params), **out_a)
        x1_b = _out_proj(*xb, ret_b, lru_b, w_out, l)
        if last:
            return _ffn(x1_b, g2, *ffn_w, out_rows=rows, out_tile0=half_tiles,
                        out_buf=x2_a).reshape(batch, seq, d)
        x2_b = _ffn(x1_b, g2, *ffn_w)
        xa, xb = (x2_a, 0), (x2_b, 0)
```

```python
import functools

import numpy as np
import jax
import jax.numpy as jnp
from jax import lax
from jax.experimental import pallas as pl
from jax.experimental.pallas import tpu as pltpu

D_MODEL = 2048
SEQ = 2048
RET_HEADS = 8
HEAD_DIM = 128
RET_WIDTH = RET_HEADS * HEAD_DIM
LRU_WIDTH = 1024
LRU_GROUPS = 16
LRU_GROUP_DIM = 64
CONV_W = 4
LRU_C = 8.0
IN_WIDTH = 4 * RET_WIDTH + 2 * LRU_WIDTH
CHUNK = 64
ROPE_BASE = 10000.0
EPS = 1e-6

F32 = jnp.float32
BF16 = jnp.bfloat16

SUBLANES = 8
BF16_ROWS = 16
TM = 1024
IN_TN = 768
IN_TN_PLAIN = 1024
FFN_TF = 256
OUT_TN = 1024
MIX_STEPS = IN_WIDTH // IN_TN
LRU_BLOCK = TM // MIX_STEPS
RET_GROUPS = 2
RET_GROUP_HEADS = RET_HEADS // RET_GROUPS
RET_GROUP_WIDTH = RET_GROUP_HEADS * HEAD_DIM
RET_BLOCK = LRU_BLOCK * RET_GROUPS
GATE_BLOCK = 256
VMEM_LIMIT = 52 * 1024 * 1024
VMEM_LIMIT_FFN = 57 * 1024 * 1024


def _params(*sem, vmem=VMEM_LIMIT):
    return pltpu.CompilerParams(dimension_semantics=sem, vmem_limit_bytes=vmem)


def _sigmoid(z):
    return 0.5 * jnp.tanh(0.5 * z) + 0.5


def _rms_scale(x, g):
    ms = jnp.mean(x * x, axis=-1, keepdims=True)
    return x * lax.rsqrt(ms + EPS) * g


def _retention_head(h, hg, q_ref, k_ref, v_ref, g_ref, cq_ref, sq_ref, ck_ref, sk_ref,
                    dmask_ref, qdec_ref, kdec_ref, sdec_ref, gn_ref, o_ref, s_ref, merged):
    cq, sq, ck, sk = cq_ref[...], sq_ref[...], ck_ref[...], sk_ref[...]
    half = HEAD_DIM // 2
    sl = slice(h * HEAD_DIM, (h + 1) * HEAD_DIM)
    q = q_ref[:, sl].astype(F32)
    k = k_ref[:, sl].astype(F32)
    v = v_ref[:, sl]
    qr = q * cq + pltpu.roll(q, half, 1) * sq
    kr = k * ck + pltpu.roll(k, half, 1) * sk
    qb = qr.astype(BF16)
    kb = kr.astype(BF16)
    scores = lax.dot_general(qb, kb, (((1,), (1,)), ((), ())),
                             preferred_element_type=F32) * dmask_ref[h]
    sidx = hg * RET_GROUP_HEADS + h
    state = s_ref[sidx]
    if merged:
        lhs = jnp.concatenate([scores.astype(BF16), (qr * qdec_ref[h]).astype(BF16)], axis=1)
        rhs = jnp.concatenate([v, state.astype(BF16)], axis=0)
        o = jnp.dot(lhs, rhs, preferred_element_type=F32)
    else:
        o = jnp.dot(scores.astype(BF16), v, preferred_element_type=F32)
        o = o + qdec_ref[h] * jnp.dot(qb, state.astype(BF16), preferred_element_type=F32)
    kd = (kr * kdec_ref[h]).astype(BF16)
    s_ref[sidx] = sdec_ref[h] * state + lax.dot_general(
        kd, v, (((0,), (0,)), ((), ())), preferred_element_type=F32)
    mu = jnp.mean(o, axis=-1, keepdims=True)
    oc = o - mu
    var = jnp.mean(oc * oc, axis=-1, keepdims=True)
    on = oc * lax.rsqrt(var + EPS) * gn_ref[:, sl]
    gate = g_ref[:, sl].astype(F32)
    o_ref[:, sl] = (on * (gate * _sigmoid(gate))).astype(o_ref.dtype)


def _lru_cols(c, xb_ref, yb_ref, cw_ref, cb_ref, wa_ref, ba_ref, wx_ref, bx_ref, lam_ref,
              xpad_ref, a_ref, b_ref, y_ref, carry_ref, wab_ref, merged):
    tt = xb_ref.shape[0]
    pad = SUBLANES
    sl = slice(c * GATE_BLOCK, (c + 1) * GATE_BLOCK)
    x = xb_ref[:, sl].astype(F32)
    xpad_ref[pad:pad + tt, sl] = x
    xc = cb_ref[:, sl] + cw_ref[CONV_W - 1:CONV_W, sl] * x
    for j in range(CONV_W - 1):
        off = pad - (CONV_W - 1) + j
        xc = xc + cw_ref[j:j + 1, sl] * xpad_ref[off:off + tt, sl]
    xpad_ref[0:pad, sl] = x[tt - pad:tt, :]

    lam = lam_ref[:, sl]
    softplus_neg_lam = jnp.maximum(-lam, 0.0) + jnp.log1p(jnp.exp(-jnp.abs(lam)))
    xc16 = xc.astype(BF16)
    if merged:
        z = jnp.dot(xc16, wab_ref[c], preferred_element_type=F32)
        r = _sigmoid(z[:, :GATE_BLOCK] + ba_ref[:, sl])
        i = _sigmoid(z[:, GATE_BLOCK:] + bx_ref[:, sl])
    else:
        r = _sigmoid(jnp.dot(xc16, wa_ref[c], preferred_element_type=F32) + ba_ref[:, sl])
        i = _sigmoid(jnp.dot(xc16, wx_ref[c], preferred_element_type=F32) + bx_ref[:, sl])
    neg_log_a = LRU_C * r * softplus_neg_lam
    a = jnp.exp(-neg_log_a)
    a_ref[:, sl] = a
    one_minus_a2 = jnp.tanh(neg_log_a) * (a * a + 1.0)
    root = jnp.where(one_minus_a2 > 0.0, one_minus_a2 * lax.rsqrt(one_minus_a2), 0.0)
    b_ref[:, sl] = root * (i * xc)

    row = lax.broadcasted_iota(jnp.int32, (SUBLANES, GATE_BLOCK), 0)
    carry = carry_ref[:, sl]
    for gidx in range(tt // SUBLANES):
        rows = slice(gidx * SUBLANES, (gidx + 1) * SUBLANES)
        a = a_ref[rows, sl]
        b = b_ref[rows, sl]
        for s in (1, 2, 4):
            keep = row >= s
            b = jnp.where(keep, a * pltpu.roll(b, s, 0) + b, b)
            a = jnp.where(keep, a * pltpu.roll(a, s, 0), a)
        h = a * carry + b
        b_ref[rows, sl] = h
        carry = h[SUBLANES - 1:SUBLANES, :]
    carry_ref[:, sl] = carry

    y = b_ref[:, sl] * jax.nn.gelu(yb_ref[:, sl].astype(F32))
    y_ref[:, sl] = y
    return y * y


def _lru_finish(ysq, ng_ref, o_ref, y_ref):
    ms = jnp.sum(ysq, axis=-1, keepdims=True) * (1.0 / LRU_WIDTH)
    o_ref[...] = (y_ref[...] * lax.rsqrt(ms + EPS) * ng_ref[...]).astype(o_ref.dtype)


N_MIX_IN = 24
N_MIX_OUT = 2
N_MIX_SCRATCH = 6
MIX_PIECES = LRU_WIDTH // GATE_BLOCK
assert MIX_PIECES == RET_GROUP_HEADS


def _row_chunks(n):
    step = TM // n // (2 * BF16_ROWS) * (2 * BF16_ROWS)
    return [slice(c * step, (c + 1) * step if c < n - 1 else TM) for c in range(n)]


MIX_CHUNKS = _row_chunks(MIX_PIECES + 1)


def _mixer_resets(s, active, mix_scratch):
    s_ref, xpad_ref, _, _, _, carry_ref = mix_scratch
    lru_blocks = SEQ // LRU_BLOCK
    hg = s % RET_GROUPS

    @pl.when(jnp.logical_and(active, s % lru_blocks == 0))
    def _():
        xpad_ref[0:SUBLANES, :] = jnp.zeros((SUBLANES, LRU_WIDTH), F32)
        carry_ref[...] = jnp.zeros_like(carry_ref)

    @pl.when(jnp.logical_and(active, (s // RET_GROUPS) % (SEQ // RET_BLOCK) == 0))
    def _():
        s_ref[pl.ds(hg * RET_GROUP_HEADS, RET_GROUP_HEADS)] = jnp.zeros(
            (RET_GROUP_HEADS, HEAD_DIM, HEAD_DIM), F32)


def _mixer_piece(c, s, ysq, mix_in, mix_out, mix_scratch, merged):
    (q, k, v, g, xb, yb, cq, sq, ck, sk, dmask, qdec, kdec, sdec, gn,
     cw, cb, wa, ba, wx, bx, lam, ng, wab) = mix_in
    o_ret, o_lru = mix_out
    s_ref, xpad_ref, a_ref, b_ref, y_ref, carry_ref = mix_scratch
    _retention_head(c, s % RET_GROUPS, q, k, v, g, cq, sq, ck, sk, dmask, qdec, kdec, sdec, gn,
                    o_ret, s_ref, merged)
    ysq_c = _lru_cols(c, xb, yb, cw, cb, wa, ba, wx, bx, lam, xpad_ref, a_ref, b_ref, y_ref,
                      carry_ref, wab, merged)
    ysq = ysq_c if ysq is None else ysq + ysq_c
    if c == MIX_PIECES - 1:
        _lru_finish(ysq, ng, o_lru, y_ref)
    return ysq


def _mixer_specs(smap, layer):
    hg = lambda i, j: smap(i, j) % RET_GROUPS
    rblk = lambda i, j: smap(i, j) // RET_GROUPS
    tblk = lambda i, j: rblk(i, j) % (SEQ // RET_BLOCK)
    pcol = lambda c: pl.BlockSpec((RET_BLOCK, RET_GROUP_WIDTH),
                                  lambda i, j, c=c: (rblk(i, j), c * RET_GROUPS + hg(i, j)))
    lcol = lambda c: pl.BlockSpec((LRU_BLOCK, LRU_WIDTH), lambda i, j, c=c: (smap(i, j), c))
    table = pl.BlockSpec((RET_BLOCK, HEAD_DIM), lambda i, j: (tblk(i, j), 0))
    per_group = lambda *tail: pl.BlockSpec((RET_GROUP_HEADS,) + tail,
                                           lambda i, j: (hg(i, j),) + (0,) * len(tail))
    vec = pl.BlockSpec((None, 1, LRU_WIDTH), lambda i, j: (layer, 0, 0))
    gatew = pl.BlockSpec((None, LRU_WIDTH // GATE_BLOCK, GATE_BLOCK, GATE_BLOCK),
                         lambda i, j: (layer, 0, 0, 0))
    xcol = 4 * RET_WIDTH // LRU_WIDTH
    in_specs = [pcol(0), pcol(1), pcol(2), pcol(3), lcol(xcol), lcol(xcol + 1),
                table, table, table, table,
                per_group(RET_BLOCK, RET_BLOCK), per_group(RET_BLOCK, HEAD_DIM),
                per_group(RET_BLOCK, HEAD_DIM), per_group(1, HEAD_DIM),
                pl.BlockSpec((None, 1, RET_GROUP_WIDTH), lambda i, j: (layer, 0, hg(i, j))),
                pl.BlockSpec((None, CONV_W, LRU_WIDTH), lambda i, j: (layer, 0, 0)),
                vec, gatew, vec, gatew, vec, vec, vec,
                pl.BlockSpec((None, LRU_WIDTH // GATE_BLOCK, GATE_BLOCK, 2 * GATE_BLOCK),
                             lambda i, j: (layer, 0, 0, 0))]
    out_specs = [pl.BlockSpec((RET_BLOCK, RET_GROUP_WIDTH), lambda i, j: (rblk(i, j), hg(i, j))),
                 pl.BlockSpec((LRU_BLOCK, LRU_WIDTH), lambda i, j: (smap(i, j), 0))]
    scratch = [pltpu.VMEM((RET_HEADS, HEAD_DIM, HEAD_DIM), F32),
               pltpu.VMEM((LRU_BLOCK + SUBLANES, LRU_WIDTH), F32),
               pltpu.VMEM((LRU_BLOCK, LRU_WIDTH), F32),
               pltpu.VMEM((LRU_BLOCK, LRU_WIDTH), F32),
               pltpu.VMEM((LRU_BLOCK, LRU_WIDTH), F32),
               pltpu.VMEM((1, LRU_WIDTH), F32)]
    return in_specs, out_specs, scratch


def _mixer_out_shapes(rows):
    return [jax.ShapeDtypeStruct((rows, RET_WIDTH), BF16),
            jax.ShapeDtypeStruct((rows, LRU_WIDTH), BF16)]


def _mixer_operands(p, mix):
    return (p,) * 6 + tuple(mix)


def _in_proj_chunk(rows, wb, o_ref, h_ref):
    o_ref[rows, :] = jnp.dot(h_ref[rows, :], wb, preferred_element_type=F32).astype(o_ref.dtype)


def _in_proj_prologue(x_ref, g_ref, h_ref):
    @pl.when(pl.program_id(1) == 0)
    def _():
        h_ref[...] = _rms_scale(x_ref[...], g_ref[...]).astype(BF16)


def _in_proj_kernel(x_ref, g_ref, w_ref, o_ref, *rest):
    h_ref = rest[-1]
    _in_proj_prologue(x_ref, g_ref, h_ref)
    _in_proj_chunk(slice(0, TM), w_ref[...].astype(BF16), o_ref, h_ref)
    for z_ref in rest[:-1]:
        z_ref[...] = jnp.zeros_like(z_ref)


def _in_proj_mix_kernel(*refs, merged):
    x_ref, g_ref, w_ref = refs[:3]
    mix_in = refs[3:3 + N_MIX_IN]
    o_ref = refs[3 + N_MIX_IN]
    mix_out = refs[4 + N_MIX_IN:4 + N_MIX_IN + N_MIX_OUT]
    h_ref = refs[-N_MIX_SCRATCH - 1]
    mix_scratch = refs[-N_MIX_SCRATCH:]
    s = pl.program_id(0) * MIX_STEPS + pl.program_id(1)
    _in_proj_prologue(x_ref, g_ref, h_ref)
    _mixer_resets(s, True, mix_scratch)
    wb = w_ref[...].astype(BF16)
    ysq = None
    for c in range(MIX_PIECES):
        _in_proj_chunk(MIX_CHUNKS[c], wb, o_ref, h_ref)
        ysq = _mixer_piece(c, s, ysq, mix_in, mix_out, mix_scratch, merged)
    _in_proj_chunk(MIX_CHUNKS[MIX_PIECES], wb, o_ref, h_ref)


def _in_proj_specs(x_tile0, layer, tn):
    d = D_MODEL
    in_specs = [pl.BlockSpec((TM, d), lambda i, j: (x_tile0 + i, 0)),
                pl.BlockSpec((None, 1, d), lambda i, j: (layer, 0, 0)),
                pl.BlockSpec((None, d, tn), lambda i, j: (layer, 0, j))]
    out_spec = pl.BlockSpec((TM, tn), lambda i, j: (i, j))
    return in_specs, out_spec


def _in_proj(x, x_tile0, rows, g, w, layer, mix=None, zero_rows=None):
    tn = IN_TN if (mix is not None or zero_rows is not None) else IN_TN_PLAIN
    in_specs, out_spec = _in_proj_specs(x_tile0, layer, tn)
    grid = (rows // TM, IN_WIDTH // tn)
    p_shape = jax.ShapeDtypeStruct((rows, IN_WIDTH), BF16)
    h_scratch = pltpu.VMEM((TM, D_MODEL), BF16)
    if mix is None:
        out_specs, out_shape = [out_spec], [p_shape]
        if zero_rows is not None:
            slab = zero_rows // (grid[0] * grid[1])
            out_specs.append(pl.BlockSpec((slab, D_MODEL), lambda i, j: (i * grid[1] + j, 0)))
            out_shape.append(jax.ShapeDtypeStruct((zero_rows, D_MODEL), F32))
        outs = pl.pallas_call(
            _in_proj_kernel, grid=grid, in_specs=in_specs, out_specs=out_specs,
            out_shape=out_shape, scratch_shapes=[h_scratch],
            compiler_params=_params("parallel", "arbitrary"), name="in_proj",
        )(x, g, w)
        return outs[0] if zero_rows is None else outs
    p_other, mix_params = mix
    m_in, m_out, m_scratch = _mixer_specs(lambda i, j: i * MIX_STEPS + j, layer)
    return pl.pallas_call(
        functools.partial(_in_proj_mix_kernel, merged=layer > 0), grid=grid,
        in_specs=in_specs + m_in, out_specs=[out_spec] + m_out,
        out_shape=[p_shape] + _mixer_out_shapes(p_other.shape[0]),
        scratch_shapes=[h_scratch] + m_scratch,
        compiler_params=_params("arbitrary", "arbitrary"), name="in_proj_mix",
    )(x, g, w, *_mixer_operands(p_other, mix_params))


def _out_proj_kernel(x_ref, r_ref, l_ref, w_ref, o_ref):
    acc = jnp.dot(r_ref[...], w_ref[0:RET_WIDTH, :].astype(BF16), preferred_element_type=F32)
    acc = acc + jnp.dot(l_ref[...], w_ref[RET_WIDTH:, :].astype(BF16), preferred_element_type=F32)
    o_ref[...] = x_ref[...] + acc


def _out_proj(x, x_tile0, o_ret, o_lru, w, layer):
    rows = o_ret.shape[0]
    d = D_MODEL
    return pl.pallas_call(
        _out_proj_kernel,
        grid=(d // OUT_TN, rows // TM),
        in_specs=[pl.BlockSpec((TM, OUT_TN), lambda j, i: (x_tile0 + i, j)),
                  pl.BlockSpec((TM, RET_WIDTH), lambda j, i: (i, 0)),
                  pl.BlockSpec((TM, LRU_WIDTH), lambda j, i: (i, 0)),
                  pl.BlockSpec((None, d, OUT_TN), lambda j, i: (layer, 0, j))],
        out_specs=pl.BlockSpec((TM, OUT_TN), lambda j, i: (i, j)),
        out_shape=jax.ShapeDtypeStruct((rows, d), F32),
        compiler_params=_params("parallel", "arbitrary"),
        name="out_proj",
    )(x, o_ret, o_lru, w)


def _ffn_prologue(x_ref, g_ref, o_ref, h_ref, x_in_hbm):
    @pl.when(pl.program_id(1) == 0)
    def _():
        if x_in_hbm:
            row0 = pl.multiple_of(pl.program_id(0) * TM, TM)
            pltpu.sync_copy(x_ref.at[pl.ds(row0, TM), :], o_ref)
        else:
            o_ref[...] = x_ref[...]
        h_ref[...] = _rms_scale(o_ref[...], g_ref[...]).astype(BF16)


def _ffn_gate_up(rows, wg, wu, h_ref):
    h = h_ref[rows, :]
    return (jnp.dot(h, wg, preferred_element_type=F32), jnp.dot(h, wu, preferred_element_type=F32))


def _ffn_down(rows, gate_up, wd, o_ref):
    gate, up = gate_up
    mid = (gate * _sigmoid(gate) * up).astype(BF16)
    o_ref[rows, :] += jnp.dot(mid, wd, preferred_element_type=F32)


def _ffn_rows(chunks, weights, o_ref, h_ref, between=None):
    wg, wu, wd = weights
    pending = None
    for c, rows in enumerate(chunks):
        gate_up = _ffn_gate_up(rows, wg, wu, h_ref)
        if pending is not None:
            _ffn_down(*pending, wd, o_ref)
        pending = (rows, gate_up)
        if between is not None and c < MIX_PIECES:
            between(c)
    _ffn_down(*pending, wd, o_ref)


def _ffn_weights(wg_ref, wu_ref, wd_ref):
    return wg_ref[...].astype(BF16), wu_ref[...].astype(BF16), wd_ref[...].astype(BF16)


def _ffn_epilogue(fg_ref, o_ref, final_norm):
    if final_norm:
        @pl.when(pl.program_id(1) == pl.num_programs(1) - 1)
        def _():
            o_ref[...] = _rms_scale(o_ref[...], fg_ref[...])


def _ffn_kernel(x_ref, g_ref, wg_ref, wu_ref, wd_ref, fg_ref, *rest, final_norm):
    o_ref, h_ref = rest[-2:]
    _ffn_prologue(x_ref, g_ref, o_ref, h_ref, x_in_hbm=False)
    _ffn_rows([slice(0, TM)], _ffn_weights(wg_ref, wu_ref, wd_ref), o_ref, h_ref)
    _ffn_epilogue(fg_ref, o_ref, final_norm)


def _ffn_mix_kernel(*refs, final_norm, n_buf):
    x_hbm, g_ref, wg_ref, wu_ref, wd_ref, fg_ref = refs[:6]
    mix_in = refs[6:6 + N_MIX_IN]
    n_in = 6 + N_MIX_IN + n_buf
    o_ref = refs[n_in]
    mix_out = refs[n_in + 1:n_in + 1 + N_MIX_OUT]
    h_ref = refs[-N_MIX_SCRATCH - 1]
    mix_scratch = refs[-N_MIX_SCRATCH:]
    j = pl.program_id(1)
    active = j < MIX_STEPS
    s = pl.program_id(0) * MIX_STEPS + jnp.minimum(j, MIX_STEPS - 1)
    _ffn_prologue(x_hbm, g_ref, o_ref, h_ref, x_in_hbm=True)
    _mixer_resets(s, active, mix_scratch)

    @pl.when(active)
    def _():
        ysq = [None]

        def piece(c):
            ysq[0] = _mixer_piece(c, s, ysq[0], mix_in, mix_out, mix_scratch, final_norm)

        _ffn_rows(MIX_CHUNKS, _ffn_weights(wg_ref, wu_ref, wd_ref), o_ref, h_ref, between=piece)

    @pl.when(jnp.logical_not(active))
    def _():
        _ffn_rows([slice(0, TM)], _ffn_weights(wg_ref, wu_ref, wd_ref), o_ref, h_ref)

    _ffn_epilogue(fg_ref, o_ref, final_norm)


def _ffn(x, g, wg, wu, wd, fg, layer, final_norm, mix=None, out_rows=None, out_tile0=0,
         out_buf=None):
    rows, d = x.shape
    out_rows = rows if out_rows is None else out_rows
    grid = (rows // TM, wg.shape[2] // FFN_TF)
    x_spec = pl.BlockSpec((TM, d), lambda i, j: (i, 0)) if mix is None else pl.BlockSpec(memory_space=pl.ANY)
    in_specs = [x_spec,
                pl.BlockSpec((None, 1, d), lambda i, j: (layer, 0, 0)),
                pl.BlockSpec((None, d, FFN_TF), lambda i, j: (layer, 0, j)),
                pl.BlockSpec((None, d, FFN_TF), lambda i, j: (layer, 0, j)),
                pl.BlockSpec((None, FFN_TF, d), lambda i, j: (layer, j, 0)),
                pl.BlockSpec((1, d), lambda i, j: (0, 0))]
    out_spec = pl.BlockSpec((TM, d), lambda i, j: (out_tile0 + i, 0))
    out_shape = jax.ShapeDtypeStruct((out_rows, d), F32)
    h_scratch = pltpu.VMEM((TM, d), BF16)
    operands = (x, g, wg, wu, wd, fg)
    buf_specs, bufs = ([], ()) if out_buf is None else ([pl.BlockSpec(memory_space=pl.ANY)], (out_buf,))
    if mix is None:
        return pl.pallas_call(
            functools.partial(_ffn_kernel, final_norm=final_norm),
            grid=grid, in_specs=in_specs + buf_specs, out_specs=out_spec, out_shape=out_shape,
            scratch_shapes=[h_scratch],
            input_output_aliases={len(operands): 0} if bufs else {},
            compiler_params=_params("parallel", "arbitrary", vmem=VMEM_LIMIT_FFN),
            name="ffn_final" if final_norm else "ffn",
        )(*operands, *bufs)
    p_other, mix_params = mix
    m_in, m_out, m_scratch = _mixer_specs(
        lambda i, j: i * MIX_STEPS + jnp.minimum(j, MIX_STEPS - 1), layer)
    operands = operands + _mixer_operands(p_other, mix_params)
    return pl.pallas_call(
        functools.partial(_ffn_mix_kernel, final_norm=final_norm, n_buf=len(bufs)),
        grid=grid, in_specs=in_specs + m_in + buf_specs, out_specs=[out_spec] + m_out,
        out_shape=[out_shape] + _mixer_out_shapes(p_other.shape[0]),
        scratch_shapes=[h_scratch] + m_scratch,
        input_output_aliases={len(operands): 0} if bufs else {},
        compiler_params=_params("arbitrary", "arbitrary"),
        name="ffn_mix_final" if final_norm else "ffn_mix",
    )(*operands, *bufs)


def _retention_tables():
    cs = RET_BLOCK
    pos = np.arange(SEQ, dtype=np.float32)
    inv = (1.0 / (ROPE_BASE ** (np.arange(0, HEAD_DIM, 2, dtype=np.float32) / HEAD_DIM))).astype(np.float32)
    ang = jnp.asarray(pos[:, None] * inv[None, :])
    cos, sin = jnp.cos(ang), jnp.sin(ang)
    cfull = jnp.concatenate([cos, cos], axis=-1)
    ssign = jnp.concatenate([-sin, sin], axis=-1)
    kscale = HEAD_DIM ** -0.5
    gamma_log = np.log1p(-np.exp2(-5.0 - np.arange(RET_HEADS, dtype=np.float64)))
    idx = np.arange(cs)
    dist = np.abs(idx[:, None] - idx[None, :])
    visible = (idx[None, :] // CHUNK) <= (idx[:, None] // CHUNK)
    dmask = np.where(visible[None], np.exp(gamma_log[:, None, None] * dist[None]), 0.0)
    qdec = np.exp(gamma_log[:, None] * (idx + 1.0)[None, :])
    kdec = np.exp(gamma_log[:, None] * (cs - 1.0 - idx)[None, :])
    sdec = np.broadcast_to(np.exp(gamma_log * cs)[:, None, None], (RET_HEADS, 1, HEAD_DIM))
    bcast = lambda a: jnp.asarray(np.broadcast_to(a[:, :, None], (RET_HEADS, cs, HEAD_DIM)), F32)
    return (cfull, ssign, cfull * kscale, ssign * kscale,
            jnp.asarray(dmask, F32), bcast(qdec), bcast(kdec), jnp.asarray(sdec, F32))


def _block_diag(w):
    per = GATE_BLOCK // LRU_GROUP_DIM
    nb = LRU_GROUPS // per
    w4 = w.reshape(w.shape[0], nb, per, LRU_GROUP_DIM, LRU_GROUP_DIM)
    bd = jnp.einsum('lcipq,ij->lcipjq', w4, jnp.eye(per, dtype=w.dtype))
    return bd.reshape(w.shape[0], nb, GATE_BLOCK, GATE_BLOCK).astype(BF16)


def kernel(x, norm1_g, w_in, ret_gn_g, lru_conv_w, lru_conv_b, lru_wa, lru_ba, lru_wx, lru_bx,
           lru_lambda, lru_norm_g, w_out, norm2_g, ffn_w_gate, ffn_w_up, ffn_w_down, final_g):
    batch, seq, d = x.shape
    depth = w_in.shape[0]
    assert (seq, d) == (SEQ, D_MODEL) and batch % 2 == 0 and depth >= 1
    rows = batch * seq
    half = rows // 2
    half_tiles = half // TM
    vecs = lambda a: a.reshape(depth, 1, -1).astype(F32)
    mix_params = _retention_tables() + (
        vecs(ret_gn_g), lru_conv_w.astype(F32), vecs(lru_conv_b),
        _block_diag(lru_wa), vecs(lru_ba), _block_diag(lru_wx), vecs(lru_bx),
        vecs(lru_lambda), vecs(lru_norm_g),
        jnp.concatenate([_block_diag(lru_wa), _block_diag(lru_wx)], axis=-1))
    g1, g2, fg = vecs(norm1_g), vecs(norm2_g), final_g.reshape(1, d).astype(F32)

    xf = x.reshape(rows, d)
    xa, xb = (xf, 0), (xf, half_tiles)
    for l in range(depth):
        last = l == depth - 1
        ffn_w = (ffn_w_gate, ffn_w_up, ffn_w_down, fg, l, last)
        if last:
            p_a, out_buf = _in_proj(*xa, half, g1, w_in, l, zero_rows=rows)
            out_a = dict(out_rows=rows, out_buf=out_buf)
        else:
            p_a, out_a = _in_proj(*xa, half, g1, w_in, l), {}
        p_b, ret_a, lru_a = _in_proj(*xb, half, g1, w_in, l, mix=(p_a, mix_params))
        x1_a = _out_proj(*xa, ret_a, lru_a, w_out, l)
        x2_a, ret_b, lru_b = _ffn(x1_a, g2, *ffn_w, mix=(p_b, mix_params), **out_a)
        x1_b = _out_proj(*xb, ret_b, lru_b, w_out, l)
        if last:
            return _ffn(x1_b, g2, *ffn_w, out_rows=rows, out_tile0=half_tiles,
                        out_buf=x2_a).reshape(batch, seq, d)
        x2_b = _ffn(x1_b, g2, *ffn_w)
        xa, xb = (x2_a, 0), (x2_b, 0)
```

```python
import functools

import numpy as np
import jax
import jax.numpy as jnp
from jax import lax
from jax.experimental import pallas as pl
from jax.experimental.pallas import tpu as pltpu

D_MODEL = 2048
SEQ = 2048
RET_HEADS = 8
HEAD_DIM = 128
RET_WIDTH = RET_HEADS * HEAD_DIM
LRU_WIDTH = 1024
LRU_GROUPS = 16
LRU_GROUP_DIM = 64
CONV_W = 4
LRU_C = 8.0
IN_WIDTH = 4 * RET_WIDTH + 2 * LRU_WIDTH
CHUNK = 64
ROPE_BASE = 10000.0
EPS = 1e-6

F32 = jnp.float32
BF16 = jnp.bfloat16

SUBLANES = 8
BF16_ROWS = 16
TM = 1024
IN_TN = 768
IN_TN_PLAIN = 1024
FFN_TF = 256
OUT_TN = 1024
MIX_STEPS = IN_WIDTH // IN_TN
LRU_BLOCK = TM // MIX_STEPS
RET_GROUPS = 2
RET_GROUP_HEADS = RET_HEADS // RET_GROUPS
RET_GROUP_WIDTH = RET_GROUP_HEADS * HEAD_DIM
RET_BLOCK = LRU_BLOCK * RET_GROUPS
GATE_BLOCK = 256
VMEM_LIMIT = 52 * 1024 * 1024
VMEM_LIMIT_FFN = 57 * 1024 * 1024


def _params(*sem, vmem=VMEM_LIMIT):
    return pltpu.CompilerParams(dimension_semantics=sem, vmem_limit_bytes=vmem)


def _sigmoid(z):
    return 0.5 * jnp.tanh(0.5 * z) + 0.5


def _rms_scale(x, g):
    ms = jnp.mean(x * x, axis=-1, keepdims=True)
    return x * lax.rsqrt(ms + EPS) * g


def _retention_head(h, hg, q_ref, k_ref, v_ref, g_ref, cos_ref, sin_ref,
                    dmask_ref, qdec_ref, kdec_ref, sdec_ref, gn_ref, o_ref, s_ref):
    cos, sin = cos_ref[...], sin_ref[...]
    half = HEAD_DIM // 2
    sl = slice(h * HEAD_DIM, (h + 1) * HEAD_DIM)
    q = q_ref[:, sl].astype(F32)
    k = k_ref[:, sl].astype(F32)
    v = v_ref[:, sl]
    qr = q * cos + pltpu.roll(q, half, 1) * sin
    kr = k * cos + pltpu.roll(k, half, 1) * sin
    qb = qr.astype(BF16)
    kb = kr.astype(BF16)
    sidx = hg * RET_GROUP_HEADS + h
    scores = lax.dot_general(qb, kb, (((1,), (1,)), ((), ())),
                             preferred_element_type=F32) * dmask_ref[sidx]
    o = jnp.dot(scores.astype(BF16), v, preferred_element_type=F32)
    state = s_ref[sidx]
    o = o + qdec_ref[sidx] * jnp.dot(qb, state.astype(BF16), preferred_element_type=F32)
    kd = (kr * kdec_ref[sidx]).astype(BF16)
    s_ref[sidx] = sdec_ref[sidx] * state + lax.dot_general(
        kd, v, (((0,), (0,)), ((), ())), preferred_element_type=F32)
    mu = jnp.mean(o, axis=-1, keepdims=True)
    oc = o - mu
    var = jnp.mean(oc * oc, axis=-1, keepdims=True)
    on = oc * lax.rsqrt(var + EPS) * gn_ref[:, sl]
    gate = g_ref[:, sl].astype(F32)
    o_ref[:, sl] = (on * (gate * _sigmoid(gate))).astype(o_ref.dtype)


def _lru_cols(c, xb_ref, yb_ref, cw_ref, cb_ref, wa_ref, ba_ref, wx_ref, bx_ref, lam_ref,
              xpad_ref, a_ref, b_ref, y_ref, carry_ref):
    tt = xb_ref.shape[0]
    pad = SUBLANES
    sl = slice(c * GATE_BLOCK, (c + 1) * GATE_BLOCK)
    x = xb_ref[:, sl].astype(F32)
    xpad_ref[pad:pad + tt, sl] = x
    xc = cb_ref[:, sl] + cw_ref[CONV_W - 1:CONV_W, sl] * x
    for j in range(CONV_W - 1):
        off = pad - (CONV_W - 1) + j
        xc = xc + cw_ref[j:j + 1, sl] * xpad_ref[off:off + tt, sl]
    xpad_ref[0:pad, sl] = x[tt - pad:tt, :]

    lam = lam_ref[:, sl]
    softplus_neg_lam = jnp.maximum(-lam, 0.0) + jnp.log1p(jnp.exp(-jnp.abs(lam)))
    xc16 = xc.astype(BF16)
    r = _sigmoid(jnp.dot(xc16, wa_ref[c], preferred_element_type=F32) + ba_ref[:, sl])
    i = _sigmoid(jnp.dot(xc16, wx_ref[c], preferred_element_type=F32) + bx_ref[:, sl])
    neg_log_a = LRU_C * r * softplus_neg_lam
    a = jnp.exp(-neg_log_a)
    a_ref[:, sl] = a
    one_minus_a2 = jnp.tanh(neg_log_a) * (a * a + 1.0)
    root = jnp.where(one_minus_a2 > 0.0, one_minus_a2 * lax.rsqrt(one_minus_a2), 0.0)
    b_ref[:, sl] = root * (i * xc)

    row = lax.broadcasted_iota(jnp.int32, (SUBLANES, GATE_BLOCK), 0)
    carry = carry_ref[:, sl]
    for gidx in range(tt // SUBLANES):
        rows = slice(gidx * SUBLANES, (gidx + 1) * SUBLANES)
        a = a_ref[rows, sl]
        b = b_ref[rows, sl]
        for s in (1, 2, 4):
            keep = row >= s
            b = jnp.where(keep, a * pltpu.roll(b, s, 0) + b, b)
            a = jnp.where(keep, a * pltpu.roll(a, s, 0), a)
        h = a * carry + b
        b_ref[rows, sl] = h
        carry = h[SUBLANES - 1:SUBLANES, :]
    carry_ref[:, sl] = carry

    y = b_ref[:, sl] * jax.nn.gelu(yb_ref[:, sl].astype(F32))
    y_ref[:, sl] = y
    return y * y


def _lru_finish(ysq, ng_ref, o_ref, y_ref):
    ms = jnp.sum(ysq, axis=-1, keepdims=True) * (1.0 / LRU_WIDTH)
    o_ref[...] = (y_ref[...] * lax.rsqrt(ms + EPS) * ng_ref[...]).astype(o_ref.dtype)


N_MIX_IN = 21
N_MIX_OUT = 2
N_MIX_SCRATCH = 6
MIX_PIECES = LRU_WIDTH // GATE_BLOCK
assert MIX_PIECES == RET_GROUP_HEADS


def _row_chunks(n):
    step = TM // n // (2 * BF16_ROWS) * (2 * BF16_ROWS)
    return [slice(c * step, (c + 1) * step if c < n - 1 else TM) for c in range(n)]


MIX_CHUNKS = _row_chunks(MIX_PIECES + 1)


def _mixer_resets(s, active, mix_scratch):
    s_ref, xpad_ref, _, _, _, carry_ref = mix_scratch
    lru_blocks = SEQ // LRU_BLOCK
    hg = s % RET_GROUPS

    @pl.when(jnp.logical_and(active, s % lru_blocks == 0))
    def _():
        xpad_ref[0:SUBLANES, :] = jnp.zeros((SUBLANES, LRU_WIDTH), F32)
        carry_ref[...] = jnp.zeros_like(carry_ref)

    @pl.when(jnp.logical_and(active, (s // RET_GROUPS) % (SEQ // RET_BLOCK) == 0))
    def _():
        s_ref[pl.ds(hg * RET_GROUP_HEADS, RET_GROUP_HEADS)] = jnp.zeros(
            (RET_GROUP_HEADS, HEAD_DIM, HEAD_DIM), F32)


def _mixer_piece(c, s, ysq, mix_in, mix_out, mix_scratch):
    (q, k, v, g, xb, yb, cos, sin, dmask, qdec, kdec, sdec, gn,
     cw, cb, wa, ba, wx, bx, lam, ng) = mix_in
    o_ret, o_lru = mix_out
    s_ref, xpad_ref, a_ref, b_ref, y_ref, carry_ref = mix_scratch
    _retention_head(c, s % RET_GROUPS, q, k, v, g, cos, sin, dmask, qdec, kdec, sdec, gn,
                    o_ret, s_ref)
    ysq_c = _lru_cols(c, xb, yb, cw, cb, wa, ba, wx, bx, lam, xpad_ref, a_ref, b_ref, y_ref,
                      carry_ref)
    ysq = ysq_c if ysq is None else ysq + ysq_c
    if c == MIX_PIECES - 1:
        _lru_finish(ysq, ng, o_lru, y_ref)
    return ysq


def _mixer_specs(smap, layer):
    hg = lambda i, j: smap(i, j) % RET_GROUPS
    rblk = lambda i, j: smap(i, j) // RET_GROUPS
    tblk = lambda i, j: rblk(i, j) % (SEQ // RET_BLOCK)
    pcol = lambda c: pl.BlockSpec((RET_BLOCK, RET_GROUP_WIDTH),
                                  lambda i, j, c=c: (rblk(i, j), c * RET_GROUPS + hg(i, j)))
    lcol = lambda c: pl.BlockSpec((LRU_BLOCK, LRU_WIDTH), lambda i, j, c=c: (smap(i, j), c))
    table = pl.BlockSpec((RET_BLOCK, HEAD_DIM), lambda i, j: (tblk(i, j), 0))
    per_head = lambda *tail: pl.BlockSpec((RET_HEADS,) + tail, lambda i, j: (0,) * (1 + len(tail)),
                                          pipeline_mode=pl.Buffered(1))
    vec = pl.BlockSpec((None, 1, LRU_WIDTH), lambda i, j: (layer, 0, 0))
    gatew = pl.BlockSpec((None, LRU_WIDTH // GATE_BLOCK, GATE_BLOCK, GATE_BLOCK),
                         lambda i, j: (layer, 0, 0, 0), pipeline_mode=pl.Buffered(1))
    xcol = 4 * RET_WIDTH // LRU_WIDTH
    in_specs = [pcol(0), pcol(1), pcol(2), pcol(3), lcol(xcol), lcol(xcol + 1),
                table, table,
                per_head(RET_BLOCK, RET_BLOCK), per_head(RET_BLOCK, HEAD_DIM),
                per_head(RET_BLOCK, HEAD_DIM), per_head(1, HEAD_DIM),
                pl.BlockSpec((None, 1, RET_GROUP_WIDTH), lambda i, j: (layer, 0, hg(i, j))),
                pl.BlockSpec((None, CONV_W, LRU_WIDTH), lambda i, j: (layer, 0, 0)),
                vec, gatew, vec, gatew, vec, vec, vec]
    out_specs = [pl.BlockSpec((RET_BLOCK, RET_GROUP_WIDTH), lambda i, j: (rblk(i, j), hg(i, j))),
                 pl.BlockSpec((LRU_BLOCK, LRU_WIDTH), lambda i, j: (smap(i, j), 0))]
    scratch = [pltpu.VMEM((RET_HEADS, HEAD_DIM, HEAD_DIM), F32),
               pltpu.VMEM((LRU_BLOCK + SUBLANES, LRU_WIDTH), F32),
               pltpu.VMEM((LRU_BLOCK, LRU_WIDTH), F32),
               pltpu.VMEM((LRU_BLOCK, LRU_WIDTH), F32),
               pltpu.VMEM((LRU_BLOCK, LRU_WIDTH), F32),
               pltpu.VMEM((1, LRU_WIDTH), F32)]
    return in_specs, out_specs, scratch


def _mixer_out_shapes(rows):
    return [jax.ShapeDtypeStruct((rows, RET_WIDTH), BF16),
            jax.ShapeDtypeStruct((rows, LRU_WIDTH), BF16)]


def _mixer_operands(p, mix):
    return (p,) * 6 + tuple(mix)


def _in_proj_chunk(rows, wb, o_ref, h_ref):
    o_ref[rows, :] = jnp.dot(h_ref[rows, :], wb, preferred_element_type=F32).astype(o_ref.dtype)


def _in_proj_prologue(x_ref, g_ref, h_ref):
    @pl.when(pl.program_id(1) == 0)
    def _():
        h_ref[...] = _rms_scale(x_ref[...], g_ref[...]).astype(BF16)


def _in_proj_kernel(x_ref, g_ref, w_ref, o_ref, *rest):
    h_ref = rest[-1]
    _in_proj_prologue(x_ref, g_ref, h_ref)
    _in_proj_chunk(slice(0, TM), w_ref[...].astype(BF16), o_ref, h_ref)
    for z_ref in rest[:-1]:
        z_ref[...] = jnp.zeros_like(z_ref)


def _in_proj_mix_kernel(*refs):
    x_ref, g_ref, w_ref = refs[:3]
    mix_in = refs[3:3 + N_MIX_IN]
    o_ref = refs[3 + N_MIX_IN]
    mix_out = refs[4 + N_MIX_IN:4 + N_MIX_IN + N_MIX_OUT]
    h_ref = refs[-N_MIX_SCRATCH - 1]
    mix_scratch = refs[-N_MIX_SCRATCH:]
    s = pl.program_id(0) * MIX_STEPS + pl.program_id(1)
    _in_proj_prologue(x_ref, g_ref, h_ref)
    _mixer_resets(s, True, mix_scratch)
    wb = w_ref[...].astype(BF16)
    ysq = None
    for c in range(MIX_PIECES):
        _in_proj_chunk(MIX_CHUNKS[c], wb, o_ref, h_ref)
        ysq = _mixer_piece(c, s, ysq, mix_in, mix_out, mix_scratch)
    _in_proj_chunk(MIX_CHUNKS[MIX_PIECES], wb, o_ref, h_ref)


def _in_proj_specs(x_tile0, layer, tn):
    d = D_MODEL
    in_specs = [pl.BlockSpec((TM, d), lambda i, j: (x_tile0 + i, 0)),
                pl.BlockSpec((None, 1, d), lambda i, j: (layer, 0, 0)),
                pl.BlockSpec((None, d, tn), lambda i, j: (layer, 0, j))]
    out_spec = pl.BlockSpec((TM, tn), lambda i, j: (i, j))
    return in_specs, out_spec


def _in_proj(x, x_tile0, rows, g, w, layer, mix=None, zero_rows=None):
    tn = IN_TN if (mix is not None or zero_rows is not None) else IN_TN_PLAIN
    in_specs, out_spec = _in_proj_specs(x_tile0, layer, tn)
    grid = (rows // TM, IN_WIDTH // tn)
    p_shape = jax.ShapeDtypeStruct((rows, IN_WIDTH), BF16)
    h_scratch = pltpu.VMEM((TM, D_MODEL), BF16)
    if mix is None:
        out_specs, out_shape = [out_spec], [p_shape]
        if zero_rows is not None:
            slab = zero_rows // (grid[0] * grid[1])
            out_specs.append(pl.BlockSpec((slab, D_MODEL), lambda i, j: (i * grid[1] + j, 0)))
            out_shape.append(jax.ShapeDtypeStruct((zero_rows, D_MODEL), F32))
        outs = pl.pallas_call(
            _in_proj_kernel, grid=grid, in_specs=in_specs, out_specs=out_specs,
            out_shape=out_shape, scratch_shapes=[h_scratch],
            compiler_params=_params("parallel", "arbitrary"), name="in_proj",
        )(x, g, w)
        return outs[0] if zero_rows is None else outs
    p_other, mix_params = mix
    m_in, m_out, m_scratch = _mixer_specs(lambda i, j: i * MIX_STEPS + j, layer)
    return pl.pallas_call(
        _in_proj_mix_kernel, grid=grid,
        in_specs=in_specs + m_in, out_specs=[out_spec] + m_out,
        out_shape=[p_shape] + _mixer_out_shapes(p_other.shape[0]),
        scratch_shapes=[h_scratch] + m_scratch,
        compiler_params=_params("arbitrary", "arbitrary"), name="in_proj_mix",
    )(x, g, w, *_mixer_operands(p_other, mix_params))


def _out_proj_kernel(x_ref, r_ref, l_ref, w_ref, o_ref):
    acc = jnp.dot(r_ref[...], w_ref[0:RET_WIDTH, :].astype(BF16), preferred_element_type=F32)
    acc = acc + jnp.dot(l_ref[...], w_ref[RET_WIDTH:, :].astype(BF16), preferred_element_type=F32)
    o_ref[...] = x_ref[...] + acc


def _out_proj(x, x_tile0, o_ret, o_lru, w, layer):
    rows = o_ret.shape[0]
    d = D_MODEL
    return pl.pallas_call(
        _out_proj_kernel,
        grid=(d // OUT_TN, rows // TM),
        in_specs=[pl.BlockSpec((TM, OUT_TN), lambda j, i: (x_tile0 + i, j)),
                  pl.BlockSpec((TM, RET_WIDTH), lambda j, i: (i, 0)),
                  pl.BlockSpec((TM, LRU_WIDTH), lambda j, i: (i, 0)),
                  pl.BlockSpec((None, d, OUT_TN), lambda j, i: (layer, 0, j))],
        out_specs=pl.BlockSpec((TM, OUT_TN), lambda j, i: (i, j)),
        out_shape=jax.ShapeDtypeStruct((rows, d), F32),
        compiler_params=_params("parallel", "arbitrary"),
        name="out_proj",
    )(x, o_ret, o_lru, w)


def _ffn_prologue(x_ref, g_ref, o_ref, h_ref, x_in_hbm):
    @pl.when(pl.program_id(1) == 0)
    def _():
        if x_in_hbm:
            row0 = pl.multiple_of(pl.program_id(0) * TM, TM)
            pltpu.sync_copy(x_ref.at[pl.ds(row0, TM), :], o_ref)
        else:
            o_ref[...] = x_ref[...]
        h_ref[...] = _rms_scale(o_ref[...], g_ref[...]).astype(BF16)


def _ffn_gate_up(rows, wg, wu, h_ref):
    h = h_ref[rows, :]
    return (jnp.dot(h, wg, preferred_element_type=F32), jnp.dot(h, wu, preferred_element_type=F32))


def _ffn_down(rows, gate_up, wd, o_ref):
    gate, up = gate_up
    mid = (gate * _sigmoid(gate) * up).astype(BF16)
    o_ref[rows, :] += jnp.dot(mid, wd, preferred_element_type=F32)


def _ffn_rows(chunks, weights, o_ref, h_ref, between=None):
    wg, wu, wd = weights
    pending = None
    for c, rows in enumerate(chunks):
        gate_up = _ffn_gate_up(rows, wg, wu, h_ref)
        if pending is not None:
            _ffn_down(*pending, wd, o_ref)
        pending = (rows, gate_up)
        if between is not None and c < MIX_PIECES:
            between(c)
    _ffn_down(*pending, wd, o_ref)


def _ffn_weights(wg_ref, wu_ref, wd_ref):
    return wg_ref[...].astype(BF16), wu_ref[...].astype(BF16), wd_ref[...].astype(BF16)


def _ffn_epilogue(fg_ref, o_ref, final_norm):
    if final_norm:
        @pl.when(pl.program_id(1) == pl.num_programs(1) - 1)
        def _():
            o_ref[...] = _rms_scale(o_ref[...], fg_ref[...])


def _ffn_kernel(x_ref, g_ref, wg_ref, wu_ref, wd_ref, fg_ref, *rest, final_norm):
    o_ref, h_ref = rest[-2:]
    _ffn_prologue(x_ref, g_ref, o_ref, h_ref, x_in_hbm=False)
    _ffn_rows([slice(0, TM)], _ffn_weights(wg_ref, wu_ref, wd_ref), o_ref, h_ref)
    _ffn_epilogue(fg_ref, o_ref, final_norm)


def _ffn_mix_kernel(*refs, final_norm, n_buf):
    x_hbm, g_ref, wg_ref, wu_ref, wd_ref, fg_ref = refs[:6]
    mix_in = refs[6:6 + N_MIX_IN]
    n_in = 6 + N_MIX_IN + n_buf
    o_ref = refs[n_in]
    mix_out = refs[n_in + 1:n_in + 1 + N_MIX_OUT]
    h_ref = refs[-N_MIX_SCRATCH - 1]
    mix_scratch = refs[-N_MIX_SCRATCH:]
    j = pl.program_id(1)
    active = j < MIX_STEPS
    s = pl.program_id(0) * MIX_STEPS + jnp.minimum(j, MIX_STEPS - 1)
    _ffn_prologue(x_hbm, g_ref, o_ref, h_ref, x_in_hbm=True)
    _mixer_resets(s, active, mix_scratch)

    @pl.when(active)
    def _():
        ysq = [None]

        def piece(c):
            ysq[0] = _mixer_piece(c, s, ysq[0], mix_in, mix_out, mix_scratch)

        _ffn_rows(MIX_CHUNKS, _ffn_weights(wg_ref, wu_ref, wd_ref), o_ref, h_ref, between=piece)

    @pl.when(jnp.logical_not(active))
    def _():
        _ffn_rows([slice(0, TM)], _ffn_weights(wg_ref, wu_ref, wd_ref), o_ref, h_ref)

    _ffn_epilogue(fg_ref, o_ref, final_norm)


def _ffn(x, g, wg, wu, wd, fg, layer, final_norm, mix=None, out_rows=None, out_tile0=0,
         out_buf=None):
    rows, d = x.shape
    out_rows = rows if out_rows is None else out_rows
    grid = (rows // TM, wg.shape[2] // FFN_TF)
    x_spec = pl.BlockSpec((TM, d), lambda i, j: (i, 0)) if mix is None else pl.BlockSpec(memory_space=pl.ANY)
    in_specs = [x_spec,
                pl.BlockSpec((None, 1, d), lambda i, j: (layer, 0, 0)),
                pl.BlockSpec((None, d, FFN_TF), lambda i, j: (layer, 0, j)),
                pl.BlockSpec((None, d, FFN_TF), lambda i, j: (layer, 0, j)),
                pl.BlockSpec((None, FFN_TF, d), lambda i, j: (layer, j, 0)),
                pl.BlockSpec((1, d), lambda i, j: (0, 0))]
    out_spec = pl.BlockSpec((TM, d), lambda i, j: (out_tile0 + i, 0))
    out_shape = jax.ShapeDtypeStruct((out_rows, d), F32)
    h_scratch = pltpu.VMEM((TM, d), BF16)
    operands = (x, g, wg, wu, wd, fg)
    buf_specs, bufs = ([], ()) if out_buf is None else ([pl.BlockSpec(memory_space=pl.ANY)], (out_buf,))
    if mix is None:
        return pl.pallas_call(
            functools.partial(_ffn_kernel, final_norm=final_norm),
            grid=grid, in_specs=in_specs + buf_specs, out_specs=out_spec, out_shape=out_shape,
            scratch_shapes=[h_scratch],
            input_output_aliases={len(operands): 0} if bufs else {},
            compiler_params=_params("parallel", "arbitrary", vmem=VMEM_LIMIT_FFN),
            name="ffn_final" if final_norm else "ffn",
        )(*operands, *bufs)
    p_other, mix_params = mix
    m_in, m_out, m_scratch = _mixer_specs(
        lambda i, j: i * MIX_STEPS + jnp.minimum(j, MIX_STEPS - 1), layer)
    operands = operands + _mixer_operands(p_other, mix_params)
    return pl.pallas_call(
        functools.partial(_ffn_mix_kernel, final_norm=final_norm, n_buf=len(bufs)),
        grid=grid, in_specs=in_specs + m_in + buf_specs, out_specs=[out_spec] + m_out,
        out_shape=[out_shape] + _mixer_out_shapes(p_other.shape[0]),
        scratch_shapes=[h_scratch] + m_scratch,
        input_output_aliases={len(operands): 0} if bufs else {},
        compiler_params=_params("arbitrary", "arbitrary"),
        name="ffn_mix_final" if final_norm else "ffn_mix",
    )(*operands, *bufs)


def _retention_tables():
    cs = RET_BLOCK
    pos = np.arange(SEQ, dtype=np.float32)
    inv = (1.0 / (ROPE_BASE ** (np.arange(0, HEAD_DIM, 2, dtype=np.float32) / HEAD_DIM))).astype(np.float32)
    ang = jnp.asarray(pos[:, None] * inv[None, :])
    cos, sin = jnp.cos(ang), jnp.sin(ang)
    cfull = jnp.concatenate([cos, cos], axis=-1)
    ssign = jnp.concatenate([-sin, sin], axis=-1)
    kscale = HEAD_DIM ** -0.5
    gamma_log = np.log1p(-np.exp2(-5.0 - np.arange(RET_HEADS, dtype=np.float64)))
    idx = np.arange(cs)
    dist = np.abs(idx[:, None] - idx[None, :])
    visible = (idx[None, :] // CHUNK) <= (idx[:, None] // CHUNK)
    dmask = kscale * np.where(visible[None], np.exp(gamma_log[:, None, None] * dist[None]), 0.0)
    qdec = np.exp(gamma_log[:, None] * (idx + 1.0)[None, :])
    kdec = kscale * np.exp(gamma_log[:, None] * (cs - 1.0 - idx)[None, :])
    sdec = np.broadcast_to(np.exp(gamma_log * cs)[:, None, None], (RET_HEADS, 1, HEAD_DIM))
    bcast = lambda a: jnp.asarray(np.broadcast_to(a[:, :, None], (RET_HEADS, cs, HEAD_DIM)), F32)
    return (cfull, ssign, jnp.asarray(dmask, F32), bcast(qdec), bcast(kdec), jnp.asarray(sdec, F32))


def _block_diag(w):
    per = GATE_BLOCK // LRU_GROUP_DIM
    nb = LRU_GROUPS // per
    w4 = w.reshape(w.shape[0], nb, per, LRU_GROUP_DIM, LRU_GROUP_DIM)
    bd = jnp.einsum('lcipq,ij->lcipjq', w4, jnp.eye(per, dtype=w.dtype))
    return bd.reshape(w.shape[0], nb, GATE_BLOCK, GATE_BLOCK).astype(BF16)


def kernel(x, norm1_g, w_in, ret_gn_g, lru_conv_w, lru_conv_b, lru_wa, lru_ba, lru_wx, lru_bx,
           lru_lambda, lru_norm_g, w_out, norm2_g, ffn_w_gate, ffn_w_up, ffn_w_down, final_g):
    batch, seq, d = x.shape
    depth = w_in.shape[0]
    assert (seq, d) == (SEQ, D_MODEL) and batch % 2 == 0 and depth >= 1
    rows = batch * seq
    half = rows // 2
    half_tiles = half // TM
    vecs = lambda a: a.reshape(depth, 1, -1).astype(F32)
    mix_params = _retention_tables() + (
        vecs(ret_gn_g), lru_conv_w.astype(F32), vecs(lru_conv_b),
        _block_diag(lru_wa), vecs(lru_ba), _block_diag(lru_wx), vecs(lru_bx),
        vecs(lru_lambda), vecs(lru_norm_g))
    g1, g2, fg = vecs(norm1_g), vecs(norm2_g), final_g.reshape(1, d).astype(F32)

    xf = x.reshape(rows, d)
    xa, xb = (xf, 0), (xf, half_tiles)
    for l in range(depth):
        last = l == depth - 1
        ffn_w = (ffn_w_gate, ffn_w_up, ffn_w_down, fg, l, last)
        if last:
            p_a, out_buf = _in_proj(*xa, half, g1, w_in, l, zero_rows=rows)
            out_a = dict(out_rows=rows, out_buf=out_buf)
        else:
            p_a, out_a = _in_proj(*xa, half, g1, w_in, l), {}
        p_b, ret_a, lru_a = _in_proj(*xb, half, g1, w_in, l, mix=(p_a, mix_params))
        x1_a = _out_proj(*xa, ret_a, lru_a, w_out, l)
        x2_a, ret_b, lru_b = _ffn(x1_a, g2, *ffn_w, mix=(p_b, mix_params), **out_a)
        x1_b = _out_proj(*xb, ret_b, lru_b, w_out, l)
        if last:
            return _ffn(x1_b, g2, *ffn_w, out_rows=rows, out_tile0=half_tiles,
                        out_buf=x2_a).reshape(batch, seq, d)
        x2_b = _ffn(x1_b, g2, *ffn_w)
        xa, xb = (x2_a, 0), (x2_b, 0)
```

```python
import functools

import numpy as np
import jax
import jax.numpy as jnp
from jax import lax
from jax.experimental import pallas as pl
from jax.experimental.pallas import tpu as pltpu

D_MODEL = 2048
SEQ = 2048
RET_HEADS = 8
HEAD_DIM = 128
RET_WIDTH = RET_HEADS * HEAD_DIM
LRU_WIDTH = 1024
LRU_GROUPS = 16
LRU_GROUP_DIM = 64
CONV_W = 4
LRU_C = 8.0
IN_WIDTH = 4 * RET_WIDTH + 2 * LRU_WIDTH
CHUNK = 64
ROPE_BASE = 10000.0
EPS = 1e-6

F32 = jnp.float32
BF16 = jnp.bfloat16

SUBLANES = 8
BF16_ROWS = 16
TM = 1024
IN_TN = 768
IN_TN_PLAIN = 1024
FFN_TF = 256
OUT_TN = 1024
MIX_STEPS = IN_WIDTH // IN_TN
LRU_BLOCK = TM // MIX_STEPS
RET_GROUPS = 2
RET_GROUP_HEADS = RET_HEADS // RET_GROUPS
RET_GROUP_WIDTH = RET_GROUP_HEADS * HEAD_DIM
RET_BLOCK = LRU_BLOCK * RET_GROUPS
GATE_BLOCK = 256
VMEM_LIMIT = 52 * 1024 * 1024
VMEM_LIMIT_FFN = 57 * 1024 * 1024
ZERO_SLABS_PER_TILE = 16


def _params(*sem, vmem=VMEM_LIMIT):
    return pltpu.CompilerParams(dimension_semantics=sem, vmem_limit_bytes=vmem)


def _sigmoid(z):
    return 0.5 * jnp.tanh(0.5 * z) + 0.5


def _rms_scale(x, g):
    ms = jnp.mean(x * x, axis=-1, keepdims=True)
    return x * lax.rsqrt(ms + EPS) * g


def _retention_head(h, hg, q_ref, k_ref, v_ref, g_ref, cos_ref, sin_ref,
                    dmask_ref, qdec_ref, kdec_ref, sdec_ref, gn_ref, o_ref, s_ref):
    cos, sin = cos_ref[...], sin_ref[...]
    half = HEAD_DIM // 2
    sl = slice(h * HEAD_DIM, (h + 1) * HEAD_DIM)
    q = q_ref[:, sl].astype(F32)
    k = k_ref[:, sl].astype(F32)
    v = v_ref[:, sl]
    qr = q * cos + pltpu.roll(q, half, 1) * sin
    kr = k * cos + pltpu.roll(k, half, 1) * sin
    qb = qr.astype(BF16)
    kb = kr.astype(BF16)
    sidx = hg * RET_GROUP_HEADS + h
    scores = lax.dot_general(qb, kb, (((1,), (1,)), ((), ())),
                             preferred_element_type=F32) * dmask_ref[sidx]
    o = jnp.dot(scores.astype(BF16), v, preferred_element_type=F32)
    state = s_ref[sidx]
    o = o + qdec_ref[sidx] * jnp.dot(qb, state.astype(BF16), preferred_element_type=F32)
    kd = (kr * kdec_ref[sidx]).astype(BF16)
    s_ref[sidx] = sdec_ref[sidx] * state + lax.dot_general(
        kd, v, (((0,), (0,)), ((), ())), preferred_element_type=F32)
    mu = jnp.mean(o, axis=-1, keepdims=True)
    oc = o - mu
    var = jnp.mean(oc * oc, axis=-1, keepdims=True)
    on = oc * lax.rsqrt(var + EPS) * gn_ref[:, sl]
    gate = g_ref[:, sl].astype(F32)
    o_ref[:, sl] = (on * (gate * _sigmoid(gate))).astype(o_ref.dtype)


def _lru_cols(c, xb_ref, yb_ref, cw_ref, cb_ref, wa_ref, ba_ref, wx_ref, bx_ref, lam_ref,
              xpad_ref, a_ref, b_ref, y_ref, carry_ref):
    tt = xb_ref.shape[0]
    pad = SUBLANES
    sl = slice(c * GATE_BLOCK, (c + 1) * GATE_BLOCK)
    x = xb_ref[:, sl].astype(F32)
    xpad_ref[pad:pad + tt, sl] = x
    xc = cb_ref[:, sl] + cw_ref[CONV_W - 1:CONV_W, sl] * x
    for j in range(CONV_W - 1):
        off = pad - (CONV_W - 1) + j
        xc = xc + cw_ref[j:j + 1, sl] * xpad_ref[off:off + tt, sl]
    xpad_ref[0:pad, sl] = x[tt - pad:tt, :]

    lam = lam_ref[:, sl]
    softplus_neg_lam = jnp.maximum(-lam, 0.0) + jnp.log1p(jnp.exp(-jnp.abs(lam)))
    xc16 = xc.astype(BF16)
    r = _sigmoid(jnp.dot(xc16, wa_ref[c], preferred_element_type=F32) + ba_ref[:, sl])
    i = _sigmoid(jnp.dot(xc16, wx_ref[c], preferred_element_type=F32) + bx_ref[:, sl])
    neg_log_a = LRU_C * r * softplus_neg_lam
    a = jnp.exp(-neg_log_a)
    a_ref[:, sl] = a
    one_minus_a2 = jnp.tanh(neg_log_a) * (a * a + 1.0)
    root = jnp.where(one_minus_a2 > 0.0, one_minus_a2 * lax.rsqrt(one_minus_a2), 0.0)
    b_ref[:, sl] = root * (i * xc)

    row = lax.broadcasted_iota(jnp.int32, (SUBLANES, GATE_BLOCK), 0)
    carry = carry_ref[:, sl]
    for gidx in range(tt // SUBLANES):
        rows = slice(gidx * SUBLANES, (gidx + 1) * SUBLANES)
        a = a_ref[rows, sl]
        b = b_ref[rows, sl]
        for s in (1, 2, 4):
            keep = row >= s
            b = jnp.where(keep, a * pltpu.roll(b, s, 0) + b, b)
            a = jnp.where(keep, a * pltpu.roll(a, s, 0), a)
        h = a * carry + b
        b_ref[rows, sl] = h
        carry = h[SUBLANES - 1:SUBLANES, :]
    carry_ref[:, sl] = carry

    y = b_ref[:, sl] * jax.nn.gelu(yb_ref[:, sl].astype(F32))
    y_ref[:, sl] = y
    return y * y


def _lru_finish(ysq, ng_ref, o_ref, y_ref):
    ms = jnp.sum(ysq, axis=-1, keepdims=True) * (1.0 / LRU_WIDTH)
    o_ref[...] = (y_ref[...] * lax.rsqrt(ms + EPS) * ng_ref[...]).astype(o_ref.dtype)


N_MIX_IN = 21
N_MIX_OUT = 2
N_MIX_SCRATCH = 6
MIX_PIECES = LRU_WIDTH // GATE_BLOCK
assert MIX_PIECES == RET_GROUP_HEADS


def _row_chunks(n):
    step = TM // n // (2 * BF16_ROWS) * (2 * BF16_ROWS)
    return [slice(c * step, (c + 1) * step if c < n - 1 else TM) for c in range(n)]


MIX_CHUNKS = _row_chunks(MIX_PIECES + 1)


def _mixer_resets(s, active, mix_scratch):
    s_ref, xpad_ref, _, _, _, carry_ref = mix_scratch
    lru_blocks = SEQ // LRU_BLOCK
    hg = s % RET_GROUPS

    @pl.when(jnp.logical_and(active, s % lru_blocks == 0))
    def _():
        xpad_ref[0:SUBLANES, :] = jnp.zeros((SUBLANES, LRU_WIDTH), F32)
        carry_ref[...] = jnp.zeros_like(carry_ref)

    @pl.when(jnp.logical_and(active, (s // RET_GROUPS) % (SEQ // RET_BLOCK) == 0))
    def _():
        s_ref[pl.ds(hg * RET_GROUP_HEADS, RET_GROUP_HEADS)] = jnp.zeros(
            (RET_GROUP_HEADS, HEAD_DIM, HEAD_DIM), F32)


def _mixer_piece(c, s, ysq, mix_in, mix_out, mix_scratch):
    (q, k, v, g, xb, yb, cos, sin, dmask, qdec, kdec, sdec, gn,
     cw, cb, wa, ba, wx, bx, lam, ng) = mix_in
    o_ret, o_lru = mix_out
    s_ref, xpad_ref, a_ref, b_ref, y_ref, carry_ref = mix_scratch
    _retention_head(c, s % RET_GROUPS, q, k, v, g, cos, sin, dmask, qdec, kdec, sdec, gn,
                    o_ret, s_ref)
    ysq_c = _lru_cols(c, xb, yb, cw, cb, wa, ba, wx, bx, lam, xpad_ref, a_ref, b_ref, y_ref,
                      carry_ref)
    ysq = ysq_c if ysq is None else ysq + ysq_c
    if c == MIX_PIECES - 1:
        _lru_finish(ysq, ng, o_lru, y_ref)
    return ysq


def _mixer_specs(smap, layer):
    hg = lambda i, j: smap(i, j) % RET_GROUPS
    rblk = lambda i, j: smap(i, j) // RET_GROUPS
    tblk = lambda i, j: rblk(i, j) % (SEQ // RET_BLOCK)
    pcol = lambda c: pl.BlockSpec((RET_BLOCK, RET_GROUP_WIDTH),
                                  lambda i, j, c=c: (rblk(i, j), c * RET_GROUPS + hg(i, j)))
    lcol = lambda c: pl.BlockSpec((LRU_BLOCK, LRU_WIDTH), lambda i, j, c=c: (smap(i, j), c))
    table = pl.BlockSpec((RET_BLOCK, HEAD_DIM), lambda i, j: (tblk(i, j), 0))
    per_head = lambda *tail: pl.BlockSpec((RET_HEADS,) + tail, lambda i, j: (0,) * (1 + len(tail)),
                                          pipeline_mode=pl.Buffered(1))
    vec = pl.BlockSpec((None, 1, LRU_WIDTH), lambda i, j: (layer, 0, 0))
    gatew = pl.BlockSpec((None, LRU_WIDTH // GATE_BLOCK, GATE_BLOCK, GATE_BLOCK),
                         lambda i, j: (layer, 0, 0, 0), pipeline_mode=pl.Buffered(1))
    xcol = 4 * RET_WIDTH // LRU_WIDTH
    in_specs = [pcol(0), pcol(1), pcol(2), pcol(3), lcol(xcol), lcol(xcol + 1),
                table, table,
                per_head(RET_BLOCK, RET_BLOCK), per_head(RET_BLOCK, HEAD_DIM),
                per_head(RET_BLOCK, HEAD_DIM), per_head(1, HEAD_DIM),
                pl.BlockSpec((None, 1, RET_GROUP_WIDTH), lambda i, j: (layer, 0, hg(i, j))),
                pl.BlockSpec((None, CONV_W, LRU_WIDTH), lambda i, j: (layer, 0, 0)),
                vec, gatew, vec, gatew, vec, vec, vec]
    out_specs = [pl.BlockSpec((RET_BLOCK, RET_GROUP_WIDTH), lambda i, j: (rblk(i, j), hg(i, j))),
                 pl.BlockSpec((LRU_BLOCK, LRU_WIDTH), lambda i, j: (smap(i, j), 0))]
    scratch = [pltpu.VMEM((RET_HEADS, HEAD_DIM, HEAD_DIM), F32),
               pltpu.VMEM((LRU_BLOCK + SUBLANES, LRU_WIDTH), F32),
               pltpu.VMEM((LRU_BLOCK, LRU_WIDTH), F32),
               pltpu.VMEM((LRU_BLOCK, LRU_WIDTH), F32),
               pltpu.VMEM((LRU_BLOCK, LRU_WIDTH), F32),
               pltpu.VMEM((1, LRU_WIDTH), F32)]
    return in_specs, out_specs, scratch


def _mixer_out_shapes(rows):
    return [jax.ShapeDtypeStruct((rows, RET_WIDTH), BF16),
            jax.ShapeDtypeStruct((rows, LRU_WIDTH), BF16)]


def _mixer_operands(p, mix):
    return (p,) * 6 + tuple(mix)


def _in_proj_chunk(rows, wb, o_ref, h_ref):
    o_ref[rows, :] = jnp.dot(h_ref[rows, :], wb, preferred_element_type=F32).astype(o_ref.dtype)


def _in_proj_prologue(x_ref, g_ref, h_ref):
    @pl.when(pl.program_id(1) == 0)
    def _():
        h_ref[...] = _rms_scale(x_ref[...], g_ref[...]).astype(BF16)


def _in_proj_kernel(x_ref, g_ref, w_ref, o_ref, h_ref):
    _in_proj_prologue(x_ref, g_ref, h_ref)
    _in_proj_chunk(slice(0, TM), w_ref[...].astype(BF16), o_ref, h_ref)


def _in_proj_mix_kernel(*refs):
    x_ref, g_ref, w_ref = refs[:3]
    mix_in = refs[3:3 + N_MIX_IN]
    o_ref = refs[3 + N_MIX_IN]
    mix_out = refs[4 + N_MIX_IN:4 + N_MIX_IN + N_MIX_OUT]
    h_ref = refs[-N_MIX_SCRATCH - 1]
    mix_scratch = refs[-N_MIX_SCRATCH:]
    s = pl.program_id(0) * MIX_STEPS + pl.program_id(1)
    _in_proj_prologue(x_ref, g_ref, h_ref)
    _mixer_resets(s, True, mix_scratch)
    wb = w_ref[...].astype(BF16)
    ysq = None
    for c in range(MIX_PIECES):
        _in_proj_chunk(MIX_CHUNKS[c], wb, o_ref, h_ref)
        ysq = _mixer_piece(c, s, ysq, mix_in, mix_out, mix_scratch)
    _in_proj_chunk(MIX_CHUNKS[MIX_PIECES], wb, o_ref, h_ref)


def _in_proj_specs(x_tile0, layer, tn):
    d = D_MODEL
    in_specs = [pl.BlockSpec((TM, d), lambda i, j: (x_tile0 + i, 0)),
                pl.BlockSpec((None, 1, d), lambda i, j: (layer, 0, 0)),
                pl.BlockSpec((None, d, tn), lambda i, j: (layer, 0, j))]
    out_spec = pl.BlockSpec((TM, tn), lambda i, j: (i, j))
    return in_specs, out_spec


def _in_proj(x, x_tile0, rows, g, w, layer, mix=None):
    tn = IN_TN if mix is not None else IN_TN_PLAIN
    in_specs, out_spec = _in_proj_specs(x_tile0, layer, tn)
    grid = (rows // TM, IN_WIDTH // tn)
    p_shape = jax.ShapeDtypeStruct((rows, IN_WIDTH), BF16)
    h_scratch = pltpu.VMEM((TM, D_MODEL), BF16)
    if mix is None:
        return pl.pallas_call(
            _in_proj_kernel, grid=grid, in_specs=in_specs, out_specs=out_spec,
            out_shape=p_shape, scratch_shapes=[h_scratch],
            compiler_params=_params("parallel", "arbitrary"), name="in_proj",
        )(x, g, w)
    p_other, mix_params, mix_layer = mix
    m_in, m_out, m_scratch = _mixer_specs(lambda i, j: i * MIX_STEPS + j, mix_layer)
    return pl.pallas_call(
        _in_proj_mix_kernel, grid=grid,
        in_specs=in_specs + m_in, out_specs=[out_spec] + m_out,
        out_shape=[p_shape] + _mixer_out_shapes(p_other.shape[0]),
        scratch_shapes=[h_scratch] + m_scratch,
        compiler_params=_params("arbitrary", "arbitrary"), name="in_proj_mix",
    )(x, g, w, *_mixer_operands(p_other, mix_params))


def _out_proj_kernel(x_ref, r_ref, l_ref, w_ref, o_ref):
    acc = jnp.dot(r_ref[...], w_ref[0:RET_WIDTH, :].astype(BF16), preferred_element_type=F32)
    acc = acc + jnp.dot(l_ref[...], w_ref[RET_WIDTH:, :].astype(BF16), preferred_element_type=F32)
    o_ref[...] = x_ref[...] + acc


def _out_proj(x, x_tile0, o_ret, o_lru, w, layer):
    rows = o_ret.shape[0]
    d = D_MODEL
    return pl.pallas_call(
        _out_proj_kernel,
        grid=(d // OUT_TN, rows // TM),
        in_specs=[pl.BlockSpec((TM, OUT_TN), lambda j, i: (x_tile0 + i, j)),
                  pl.BlockSpec((TM, RET_WIDTH), lambda j, i: (i, 0)),
                  pl.BlockSpec((TM, LRU_WIDTH), lambda j, i: (i, 0)),
                  pl.BlockSpec((None, d, OUT_TN), lambda j, i: (layer, 0, j))],
        out_specs=pl.BlockSpec((TM, OUT_TN), lambda j, i: (i, j)),
        out_shape=jax.ShapeDtypeStruct((rows, d), F32),
        compiler_params=_params("parallel", "arbitrary"),
        name="out_proj",
    )(x, o_ret, o_lru, w)


def _ffn_prologue(x_ref, g_ref, o_ref, h_ref, x_in_hbm):
    @pl.when(pl.program_id(1) == 0)
    def _():
        if x_in_hbm:
            row0 = pl.multiple_of(pl.program_id(0) * TM, TM)
            pltpu.sync_copy(x_ref.at[pl.ds(row0, TM), :], o_ref)
        else:
            o_ref[...] = x_ref[...]
        h_ref[...] = _rms_scale(o_ref[...], g_ref[...]).astype(BF16)


def _ffn_gate_up(rows, wg, wu, h_ref):
    h = h_ref[rows, :]
    return (jnp.dot(h, wg, preferred_element_type=F32), jnp.dot(h, wu, preferred_element_type=F32))


def _ffn_down(rows, gate_up, wd, o_ref):
    gate, up = gate_up
    mid = (gate * _sigmoid(gate) * up).astype(BF16)
    o_ref[rows, :] += jnp.dot(mid, wd, preferred_element_type=F32)


def _ffn_rows(chunks, weights, o_ref, h_ref, between=None):
    wg, wu, wd = weights
    pending = None
    for c, rows in enumerate(chunks):
        gate_up = _ffn_gate_up(rows, wg, wu, h_ref)
        if pending is not None:
            _ffn_down(*pending, wd, o_ref)
        pending = (rows, gate_up)
        if between is not None and c < MIX_PIECES:
            between(c)
    _ffn_down(*pending, wd, o_ref)


def _ffn_weights(wg_ref, wu_ref, wd_ref):
    return wg_ref[...].astype(BF16), wu_ref[...].astype(BF16), wd_ref[...].astype(BF16)


def _ffn_epilogue(fg_ref, o_ref, final_norm):
    if final_norm:
        @pl.when(pl.program_id(1) == pl.num_programs(1) - 1)
        def _():
            o_ref[...] = _rms_scale(o_ref[...], fg_ref[...])


def _ffn_kernel(x_ref, g_ref, wg_ref, wu_ref, wd_ref, fg_ref, *rest, final_norm, n_buf, x_in_hbm):
    o_ref, h_ref = rest[n_buf], rest[-1]
    _ffn_prologue(x_ref, g_ref, o_ref, h_ref, x_in_hbm=x_in_hbm)
    _ffn_rows([slice(0, TM)], _ffn_weights(wg_ref, wu_ref, wd_ref), o_ref, h_ref)
    _ffn_epilogue(fg_ref, o_ref, final_norm)
    for z_ref in rest[n_buf + 1:-1]:
        z_ref[...] = jnp.zeros_like(z_ref)


def _ffn_mix_kernel(*refs, final_norm, n_buf):
    x_hbm, g_ref, wg_ref, wu_ref, wd_ref, fg_ref = refs[:6]
    mix_in = refs[6:6 + N_MIX_IN]
    n_in = 6 + N_MIX_IN + n_buf
    o_ref = refs[n_in]
    mix_out = refs[n_in + 1:n_in + 1 + N_MIX_OUT]
    h_ref = refs[-N_MIX_SCRATCH - 1]
    mix_scratch = refs[-N_MIX_SCRATCH:]
    j = pl.program_id(1)
    active = j < MIX_STEPS
    s = pl.program_id(0) * MIX_STEPS + jnp.minimum(j, MIX_STEPS - 1)
    _ffn_prologue(x_hbm, g_ref, o_ref, h_ref, x_in_hbm=True)
    _mixer_resets(s, active, mix_scratch)

    @pl.when(active)
    def _():
        ysq = [None]

        def piece(c):
            ysq[0] = _mixer_piece(c, s, ysq[0], mix_in, mix_out, mix_scratch)

        _ffn_rows(MIX_CHUNKS, _ffn_weights(wg_ref, wu_ref, wd_ref), o_ref, h_ref, between=piece)

    @pl.when(jnp.logical_not(active))
    def _():
        _ffn_rows([slice(0, TM)], _ffn_weights(wg_ref, wu_ref, wd_ref), o_ref, h_ref)

    _ffn_epilogue(fg_ref, o_ref, final_norm)


def _ffn(x, g, wg, wu, wd, fg, layer, final_norm, mix=None, out_rows=None, out_tile0=0,
         out_buf=None, zero_rows=None):
    rows, d = x.shape
    out_rows = rows if out_rows is None else out_rows
    grid = (rows // TM, wg.shape[2] // FFN_TF)
    x_in_hbm = mix is not None or zero_rows is not None
    x_spec = pl.BlockSpec(memory_space=pl.ANY) if x_in_hbm else pl.BlockSpec((TM, d), lambda i, j: (i, 0))
    in_specs = [x_spec,
                pl.BlockSpec((None, 1, d), lambda i, j: (layer, 0, 0)),
                pl.BlockSpec((None, d, FFN_TF), lambda i, j: (layer, 0, j)),
                pl.BlockSpec((None, d, FFN_TF), lambda i, j: (layer, 0, j)),
                pl.BlockSpec((None, FFN_TF, d), lambda i, j: (layer, j, 0)),
                pl.BlockSpec((1, d), lambda i, j: (0, 0))]
    out_spec = pl.BlockSpec((TM, d), lambda i, j: (out_tile0 + i, 0))
    out_shape = jax.ShapeDtypeStruct((out_rows, d), F32)
    h_scratch = pltpu.VMEM((TM, d), BF16)
    operands = (x, g, wg, wu, wd, fg)
    buf_specs, bufs = ([], ()) if out_buf is None else ([pl.BlockSpec(memory_space=pl.ANY)], (out_buf,))
    if mix is None:
        out_specs, out_shapes = [out_spec], [out_shape]
        if zero_rows is not None:
            per_tile = ZERO_SLABS_PER_TILE
            slab = zero_rows // (grid[0] * per_tile)
            out_specs.append(pl.BlockSpec(
                (slab, d), lambda i, j: (i * per_tile + jnp.minimum(j, per_tile - 1), 0)))
            out_shapes.append(jax.ShapeDtypeStruct((zero_rows, d), F32))
        outs = pl.pallas_call(
            functools.partial(_ffn_kernel, final_norm=final_norm, n_buf=len(bufs), x_in_hbm=x_in_hbm),
            grid=grid, in_specs=in_specs + buf_specs, out_specs=out_specs, out_shape=out_shapes,
            scratch_shapes=[h_scratch],
            input_output_aliases={len(operands): 0} if bufs else {},
            compiler_params=_params("parallel", "arbitrary", vmem=VMEM_LIMIT_FFN),
            name="ffn_final" if final_norm else "ffn",
        )(*operands, *bufs)
        return outs[0] if zero_rows is None else outs
    p_other, mix_params = mix
    m_in, m_out, m_scratch = _mixer_specs(
        lambda i, j: i * MIX_STEPS + jnp.minimum(j, MIX_STEPS - 1), layer)
    operands = operands + _mixer_operands(p_other, mix_params)
    return pl.pallas_call(
        functools.partial(_ffn_mix_kernel, final_norm=final_norm, n_buf=len(bufs)),
        grid=grid, in_specs=in_specs + m_in + buf_specs, out_specs=[out_spec] + m_out,
        out_shape=[out_shape] + _mixer_out_shapes(p_other.shape[0]),
        scratch_shapes=[h_scratch] + m_scratch,
        input_output_aliases={len(operands): 0} if bufs else {},
        compiler_params=_params("arbitrary", "arbitrary"),
        name="ffn_mix_final" if final_norm else "ffn_mix",
    )(*operands, *bufs)


def _retention_tables():
    cs = RET_BLOCK
    pos = np.arange(SEQ, dtype=np.float32)
    inv = (1.0 / (ROPE_BASE ** (np.arange(0, HEAD_DIM, 2, dtype=np.float32) / HEAD_DIM))).astype(np.float32)
    ang = jnp.asarray(pos[:, None] * inv[None, :])
    cos, sin = jnp.cos(ang), jnp.sin(ang)
    cfull = jnp.concatenate([cos, cos], axis=-1)
    ssign = jnp.concatenate([-sin, sin], axis=-1)
    kscale = HEAD_DIM ** -0.5
    gamma_log = np.log1p(-np.exp2(-5.0 - np.arange(RET_HEADS, dtype=np.float64)))
    idx = np.arange(cs)
    dist = np.abs(idx[:, None] - idx[None, :])
    visible = (idx[None, :] // CHUNK) <= (idx[:, None] // CHUNK)
    dmask = kscale * np.where(visible[None], np.exp(gamma_log[:, None, None] * dist[None]), 0.0)
    qdec = np.exp(gamma_log[:, None] * (idx + 1.0)[None, :])
    kdec = kscale * np.exp(gamma_log[:, None] * (cs - 1.0 - idx)[None, :])
    sdec = np.broadcast_to(np.exp(gamma_log * cs)[:, None, None], (RET_HEADS, 1, HEAD_DIM))
    bcast = lambda a: jnp.asarray(np.broadcast_to(a[:, :, None], (RET_HEADS, cs, HEAD_DIM)), F32)
    return (cfull, ssign, jnp.asarray(dmask, F32), bcast(qdec), bcast(kdec), jnp.asarray(sdec, F32))


def _block_diag(w):
    per = GATE_BLOCK // LRU_GROUP_DIM
    nb = LRU_GROUPS // per
    w4 = w.reshape(w.shape[0], nb, per, LRU_GROUP_DIM, LRU_GROUP_DIM)
    bd = jnp.einsum('lcipq,ij->lcipjq', w4, jnp.eye(per, dtype=w.dtype))
    return bd.reshape(w.shape[0], nb, GATE_BLOCK, GATE_BLOCK).astype(BF16)


def kernel(x, norm1_g, w_in, ret_gn_g, lru_conv_w, lru_conv_b, lru_wa, lru_ba, lru_wx, lru_bx,
           lru_lambda, lru_norm_g, w_out, norm2_g, ffn_w_gate, ffn_w_up, ffn_w_down, final_g):
    batch, seq, d = x.shape
    depth = w_in.shape[0]
    assert (seq, d) == (SEQ, D_MODEL) and batch % 2 == 0
    rows = batch * seq
    half = rows // 2
    half_tiles = half // TM
    vecs = lambda a: a.reshape(depth, 1, -1).astype(F32)
    mix_params = _retention_tables() + (
        vecs(ret_gn_g), lru_conv_w.astype(F32), vecs(lru_conv_b),
        _block_diag(lru_wa), vecs(lru_ba), _block_diag(lru_wx), vecs(lru_bx),
        vecs(lru_lambda), vecs(lru_norm_g))
    g1, g2, fg = vecs(norm1_g), vecs(norm2_g), final_g.reshape(1, d).astype(F32)

    assert depth >= 2
    xf = x.reshape(rows, d)
    xa, xb = (xf, 0), (xf, half_tiles)
    p_a = _in_proj(*xa, half, g1, w_in, 0)
    for l in range(depth):
        last = l == depth - 1
        ffn_w = (ffn_w_gate, ffn_w_up, ffn_w_down, fg, l, last)
        p_b, ret_a, lru_a = _in_proj(*xb, half, g1, w_in, l, mix=(p_a, mix_params, l))
        x1_a = _out_proj(*xa, ret_a, lru_a, w_out, l)
        if last:
            x2_a, ret_b, lru_b = _ffn(x1_a, g2, *ffn_w, mix=(p_b, mix_params), out_rows=rows,
                                      out_buf=out_buf)
            x1_b = _out_proj(*xb, ret_b, lru_b, w_out, l)
            return _ffn(x1_b, g2, *ffn_w, out_rows=rows, out_tile0=half_tiles,
                        out_buf=x2_a).reshape(batch, seq, d)
        x2_a = _ffn(x1_a, g2, *ffn_w)
        xa = (x2_a, 0)
        p_a, ret_b, lru_b = _in_proj(*xa, half, g1, w_in, l + 1, mix=(p_b, mix_params, l))
        x1_b = _out_proj(*xb, ret_b, lru_b, w_out, l)
        zero = dict(zero_rows=rows) if l == depth - 2 else {}
        x2_b = _ffn(x1_b, g2, *ffn_w, **zero)
        if zero:
            x2_b, out_buf = x2_b
        xb = (x2_b, 0)
```

```python
import functools

import numpy as np
import jax
import jax.numpy as jnp
from jax import lax
from jax.experimental import pallas as pl
from jax.experimental.pallas import tpu as pltpu

D_MODEL = 2048
SEQ = 2048
RET_HEADS = 8
HEAD_DIM = 128
RET_WIDTH = RET_HEADS * HEAD_DIM
LRU_WIDTH = 1024
LRU_GROUPS = 16
LRU_GROUP_DIM = 64
CONV_W = 4
LRU_C = 8.0
IN_WIDTH = 4 * RET_WIDTH + 2 * LRU_WIDTH
CHUNK = 64
ROPE_BASE = 10000.0
EPS = 1e-6

F32 = jnp.float32
BF16 = jnp.bfloat16

SUBLANES = 8
BF16_ROWS = 16
TM = 1024
IN_TN = 768
IN_TN_PLAIN = 1024
FFN_TF = 256
OUT_TN = 1024
MIX_STEPS = IN_WIDTH // IN_TN
LRU_BLOCK = TM // MIX_STEPS
RET_GROUPS = 2
RET_GROUP_HEADS = RET_HEADS // RET_GROUPS
RET_GROUP_WIDTH = RET_GROUP_HEADS * HEAD_DIM
RET_BLOCK = LRU_BLOCK * RET_GROUPS
GATE_BLOCK = 256
VMEM_LIMIT = 52 * 1024 * 1024
VMEM_LIMIT_FFN = 57 * 1024 * 1024
ZERO_SLABS_PER_TILE = 16


def _params(*sem, vmem=VMEM_LIMIT):
    return pltpu.CompilerParams(dimension_semantics=sem, vmem_limit_bytes=vmem)


def _sigmoid(z):
    return 0.5 * jnp.tanh(0.5 * z) + 0.5


def _rms_scale(x, g):
    ms = jnp.mean(x * x, axis=-1, keepdims=True)
    return x * lax.rsqrt(ms + EPS) * g


def _retention_head(h, hg, q_ref, k_ref, v_ref, g_ref, rot_ref, decay_ref, head_ref, o_ref, s_ref):
    cos, sin = rot_ref[:, :HEAD_DIM], rot_ref[:, HEAD_DIM:]
    half = HEAD_DIM // 2
    sl = slice(h * HEAD_DIM, (h + 1) * HEAD_DIM)
    q = q_ref[:, sl].astype(F32)
    k = k_ref[:, sl].astype(F32)
    v = v_ref[:, sl]
    qr = q * cos + pltpu.roll(q, half, 1) * sin
    kr = k * cos + pltpu.roll(k, half, 1) * sin
    qb = qr.astype(BF16)
    kb = kr.astype(BF16)
    sidx = hg * RET_GROUP_HEADS + h
    dmask = decay_ref[sidx, :, 0:RET_BLOCK]
    qdec = decay_ref[sidx, :, RET_BLOCK:RET_BLOCK + HEAD_DIM]
    kdec = decay_ref[sidx, :, RET_BLOCK + HEAD_DIM:RET_BLOCK + 2 * HEAD_DIM]
    gn, sdec = head_ref[sidx, 0:1, :], head_ref[sidx, 1:2, :]
    scores = lax.dot_general(qb, kb, (((1,), (1,)), ((), ())),
                             preferred_element_type=F32) * dmask
    o = jnp.dot(scores.astype(BF16), v, preferred_element_type=F32)
    state = s_ref[sidx]
    o = o + qdec * jnp.dot(qb, state.astype(BF16), preferred_element_type=F32)
    kd = (kr * kdec).astype(BF16)
    s_ref[sidx] = sdec * state + lax.dot_general(
        kd, v, (((0,), (0,)), ((), ())), preferred_element_type=F32)
    mu = jnp.mean(o, axis=-1, keepdims=True)
    oc = o - mu
    var = jnp.mean(oc * oc, axis=-1, keepdims=True)
    on = oc * lax.rsqrt(var + EPS) * gn
    gate = g_ref[:, sl].astype(F32)
    o_ref[:, sl] = (on * (gate * _sigmoid(gate))).astype(o_ref.dtype)


def _lru_cols(c, xy_ref, vec_ref, gates_ref, xpad_ref, a_ref, b_ref, y_ref, carry_ref):
    tt = xy_ref.shape[0]
    pad = SUBLANES
    sl = slice(c * GATE_BLOCK, (c + 1) * GATE_BLOCK)
    ysl = slice(LRU_WIDTH + c * GATE_BLOCK, LRU_WIDTH + (c + 1) * GATE_BLOCK)
    vrow = lambda r: vec_ref[r:r + 1, sl]
    x = xy_ref[:, sl].astype(F32)
    xpad_ref[pad:pad + tt, sl] = x
    xc = vrow(VEC_CONV_B) + vrow(VEC_CONV_W + CONV_W - 1) * x
    for j in range(CONV_W - 1):
        off = pad - (CONV_W - 1) + j
        xc = xc + vrow(VEC_CONV_W + j) * xpad_ref[off:off + tt, sl]
    xpad_ref[0:pad, sl] = x[tt - pad:tt, :]

    lam = vrow(VEC_LAMBDA)
    softplus_neg_lam = jnp.maximum(-lam, 0.0) + jnp.log1p(jnp.exp(-jnp.abs(lam)))
    xc16 = xc.astype(BF16)
    n_blocks = LRU_WIDTH // GATE_BLOCK
    r = _sigmoid(jnp.dot(xc16, gates_ref[c], preferred_element_type=F32) + vrow(VEC_BA))
    i = _sigmoid(jnp.dot(xc16, gates_ref[n_blocks + c], preferred_element_type=F32) + vrow(VEC_BX))
    neg_log_a = LRU_C * r * softplus_neg_lam
    a = jnp.exp(-neg_log_a)
    a_ref[:, sl] = a
    one_minus_a2 = jnp.tanh(neg_log_a) * (a * a + 1.0)
    root = jnp.where(one_minus_a2 > 0.0, one_minus_a2 * lax.rsqrt(one_minus_a2), 0.0)
    b_ref[:, sl] = root * (i * xc)

    row = lax.broadcasted_iota(jnp.int32, (SUBLANES, GATE_BLOCK), 0)
    carry = carry_ref[:, sl]
    for gidx in range(tt // SUBLANES):
        rows = slice(gidx * SUBLANES, (gidx + 1) * SUBLANES)
        a = a_ref[rows, sl]
        b = b_ref[rows, sl]
        for s in (1, 2, 4):
            keep = row >= s
            b = jnp.where(keep, a * pltpu.roll(b, s, 0) + b, b)
            a = jnp.where(keep, a * pltpu.roll(a, s, 0), a)
        h = a * carry + b
        b_ref[rows, sl] = h
        carry = h[SUBLANES - 1:SUBLANES, :]
    carry_ref[:, sl] = carry

    y = b_ref[:, sl] * jax.nn.gelu(xy_ref[:, ysl].astype(F32))
    y_ref[:, sl] = y
    return y * y


def _lru_finish(ysq, vec_ref, o_ref, y_ref):
    ms = jnp.sum(ysq, axis=-1, keepdims=True) * (1.0 / LRU_WIDTH)
    ng = vec_ref[VEC_NORM_G:VEC_NORM_G + 1, :]
    o_ref[...] = (y_ref[...] * lax.rsqrt(ms + EPS) * ng).astype(o_ref.dtype)


VEC_CONV_W, VEC_CONV_B, VEC_BA, VEC_BX, VEC_LAMBDA, VEC_NORM_G = 0, CONV_W, CONV_W + 1, CONV_W + 2, CONV_W + 3, CONV_W + 4
VEC_ROWS = 16
N_MIX_IN = 10
N_MIX_OUT = 2
N_MIX_SCRATCH = 6
MIX_PIECES = LRU_WIDTH // GATE_BLOCK
assert MIX_PIECES == RET_GROUP_HEADS


def _row_chunks(n):
    step = TM // n // (2 * BF16_ROWS) * (2 * BF16_ROWS)
    return [slice(c * step, (c + 1) * step if c < n - 1 else TM) for c in range(n)]


MIX_CHUNKS = _row_chunks(MIX_PIECES + 1)


def _mixer_resets(s, active, mix_scratch):
    s_ref, xpad_ref, _, _, _, carry_ref = mix_scratch
    lru_blocks = SEQ // LRU_BLOCK
    hg = s % RET_GROUPS

    @pl.when(jnp.logical_and(active, s % lru_blocks == 0))
    def _():
        xpad_ref[0:SUBLANES, :] = jnp.zeros((SUBLANES, LRU_WIDTH), F32)
        carry_ref[...] = jnp.zeros_like(carry_ref)

    @pl.when(jnp.logical_and(active, (s // RET_GROUPS) % (SEQ // RET_BLOCK) == 0))
    def _():
        s_ref[pl.ds(hg * RET_GROUP_HEADS, RET_GROUP_HEADS)] = jnp.zeros(
            (RET_GROUP_HEADS, HEAD_DIM, HEAD_DIM), F32)


def _mixer_piece(c, s, ysq, mix_in, mix_out, mix_scratch):
    q, k, v, g, xy, rot, decay, head, vec, gates = mix_in
    o_ret, o_lru = mix_out
    s_ref, xpad_ref, a_ref, b_ref, y_ref, carry_ref = mix_scratch
    _retention_head(c, s % RET_GROUPS, q, k, v, g, rot, decay, head, o_ret, s_ref)
    ysq_c = _lru_cols(c, xy, vec, gates, xpad_ref, a_ref, b_ref, y_ref, carry_ref)
    ysq = ysq_c if ysq is None else ysq + ysq_c
    if c == MIX_PIECES - 1:
        _lru_finish(ysq, vec, o_lru, y_ref)
    return ysq


def _mixer_specs(smap, layer):
    hg = lambda i, j: smap(i, j) % RET_GROUPS
    rblk = lambda i, j: smap(i, j) // RET_GROUPS
    tblk = lambda i, j: rblk(i, j) % (SEQ // RET_BLOCK)
    pcol = lambda c: pl.BlockSpec((RET_BLOCK, RET_GROUP_WIDTH),
                                  lambda i, j, c=c: (rblk(i, j), c * RET_GROUPS + hg(i, j)))
    resident = dict(pipeline_mode=pl.Buffered(1))
    in_specs = [pcol(0), pcol(1), pcol(2), pcol(3),
                pl.BlockSpec((LRU_BLOCK, 2 * LRU_WIDTH),
                             lambda i, j: (smap(i, j), 4 * RET_WIDTH // (2 * LRU_WIDTH))),
                pl.BlockSpec((RET_BLOCK, 2 * HEAD_DIM), lambda i, j: (tblk(i, j), 0)),
                pl.BlockSpec((RET_HEADS, RET_BLOCK, RET_BLOCK + 2 * HEAD_DIM),
                             lambda i, j: (0, 0, 0), **resident),
                pl.BlockSpec((None, RET_HEADS, SUBLANES, HEAD_DIM),
                             lambda i, j: (layer, 0, 0, 0), **resident),
                pl.BlockSpec((None, VEC_ROWS, LRU_WIDTH), lambda i, j: (layer, 0, 0), **resident),
                pl.BlockSpec((None, 2 * LRU_WIDTH // GATE_BLOCK, GATE_BLOCK, GATE_BLOCK),
                             lambda i, j: (layer, 0, 0, 0), **resident)]
    out_specs = [pl.BlockSpec((RET_BLOCK, RET_GROUP_WIDTH), lambda i, j: (rblk(i, j), hg(i, j))),
                 pl.BlockSpec((LRU_BLOCK, LRU_WIDTH), lambda i, j: (smap(i, j), 0))]
    scratch = [pltpu.VMEM((RET_HEADS, HEAD_DIM, HEAD_DIM), F32),
               pltpu.VMEM((LRU_BLOCK + SUBLANES, LRU_WIDTH), F32),
               pltpu.VMEM((LRU_BLOCK, LRU_WIDTH), F32),
               pltpu.VMEM((LRU_BLOCK, LRU_WIDTH), F32),
               pltpu.VMEM((LRU_BLOCK, LRU_WIDTH), F32),
               pltpu.VMEM((1, LRU_WIDTH), F32)]
    return in_specs, out_specs, scratch


def _mixer_out_shapes(rows):
    return [jax.ShapeDtypeStruct((rows, RET_WIDTH), BF16),
            jax.ShapeDtypeStruct((rows, LRU_WIDTH), BF16)]


def _mixer_operands(p, mix):
    return (p,) * 5 + tuple(mix)


def _in_proj_chunk(rows, wb, o_ref, h_ref):
    o_ref[rows, :] = jnp.dot(h_ref[rows, :], wb, preferred_element_type=F32).astype(o_ref.dtype)


def _in_proj_prologue(x_ref, g_ref, h_ref):
    @pl.when(pl.program_id(1) == 0)
    def _():
        h_ref[...] = _rms_scale(x_ref[...], g_ref[...]).astype(BF16)


def _in_proj_kernel(x_ref, g_ref, w_ref, o_ref, h_ref):
    _in_proj_prologue(x_ref, g_ref, h_ref)
    _in_proj_chunk(slice(0, TM), w_ref[...].astype(BF16), o_ref, h_ref)


def _in_proj_mix_kernel(*refs):
    x_ref, g_ref, w_ref = refs[:3]
    mix_in = refs[3:3 + N_MIX_IN]
    o_ref = refs[3 + N_MIX_IN]
    mix_out = refs[4 + N_MIX_IN:4 + N_MIX_IN + N_MIX_OUT]
    h_ref = refs[-N_MIX_SCRATCH - 1]
    mix_scratch = refs[-N_MIX_SCRATCH:]
    s = pl.program_id(0) * MIX_STEPS + pl.program_id(1)
    _in_proj_prologue(x_ref, g_ref, h_ref)
    _mixer_resets(s, True, mix_scratch)
    wb = w_ref[...].astype(BF16)
    ysq = None
    for c in range(MIX_PIECES):
        _in_proj_chunk(MIX_CHUNKS[c], wb, o_ref, h_ref)
        ysq = _mixer_piece(c, s, ysq, mix_in, mix_out, mix_scratch)
    _in_proj_chunk(MIX_CHUNKS[MIX_PIECES], wb, o_ref, h_ref)


def _in_proj_specs(x_tile0, layer, tn):
    d = D_MODEL
    in_specs = [pl.BlockSpec((TM, d), lambda i, j: (x_tile0 + i, 0)),
                pl.BlockSpec((None, 1, d), lambda i, j: (layer, 0, 0)),
                pl.BlockSpec((None, d, tn), lambda i, j: (layer, 0, j))]
    out_spec = pl.BlockSpec((TM, tn), lambda i, j: (i, j))
    return in_specs, out_spec


def _in_proj(x, x_tile0, rows, g, w, layer, mix=None):
    tn = IN_TN if mix is not None else IN_TN_PLAIN
    in_specs, out_spec = _in_proj_specs(x_tile0, layer, tn)
    grid = (rows // TM, IN_WIDTH // tn)
    p_shape = jax.ShapeDtypeStruct((rows, IN_WIDTH), BF16)
    h_scratch = pltpu.VMEM((TM, D_MODEL), BF16)
    if mix is None:
        return pl.pallas_call(
            _in_proj_kernel, grid=grid, in_specs=in_specs, out_specs=out_spec,
            out_shape=p_shape, scratch_shapes=[h_scratch],
            compiler_params=_params("parallel", "arbitrary"), name="in_proj",
        )(x, g, w)
    p_other, mix_params, mix_layer = mix
    m_in, m_out, m_scratch = _mixer_specs(lambda i, j: i * MIX_STEPS + j, mix_layer)
    return pl.pallas_call(
        _in_proj_mix_kernel, grid=grid,
        in_specs=in_specs + m_in, out_specs=[out_spec] + m_out,
        out_shape=[p_shape] + _mixer_out_shapes(p_other.shape[0]),
        scratch_shapes=[h_scratch] + m_scratch,
        compiler_params=_params("arbitrary", "arbitrary"), name="in_proj_mix",
    )(x, g, w, *_mixer_operands(p_other, mix_params))


def _out_proj_kernel(x_ref, r_ref, l_ref, w_ref, o_ref):
    acc = jnp.dot(r_ref[...], w_ref[0:RET_WIDTH, :].astype(BF16), preferred_element_type=F32)
    acc = acc + jnp.dot(l_ref[...], w_ref[RET_WIDTH:, :].astype(BF16), preferred_element_type=F32)
    o_ref[...] = x_ref[...] + acc


def _out_proj(x, x_tile0, o_ret, o_lru, w, layer):
    rows = o_ret.shape[0]
    d = D_MODEL
    return pl.pallas_call(
        _out_proj_kernel,
        grid=(d // OUT_TN, rows // TM),
        in_specs=[pl.BlockSpec((TM, OUT_TN), lambda j, i: (x_tile0 + i, j)),
                  pl.BlockSpec((TM, RET_WIDTH), lambda j, i: (i, 0)),
                  pl.BlockSpec((TM, LRU_WIDTH), lambda j, i: (i, 0)),
                  pl.BlockSpec((None, d, OUT_TN), lambda j, i: (layer, 0, j))],
        out_specs=pl.BlockSpec((TM, OUT_TN), lambda j, i: (i, j)),
        out_shape=jax.ShapeDtypeStruct((rows, d), F32),
        compiler_params=_params("parallel", "arbitrary"),
        name="out_proj",
    )(x, o_ret, o_lru, w)


def _ffn_prologue(x_ref, g_ref, o_ref, h_ref, x_in_hbm):
    @pl.when(pl.program_id(1) == 0)
    def _():
        if x_in_hbm:
            row0 = pl.multiple_of(pl.program_id(0) * TM, TM)
            pltpu.sync_copy(x_ref.at[pl.ds(row0, TM), :], o_ref)
        else:
            o_ref[...] = x_ref[...]
        h_ref[...] = _rms_scale(o_ref[...], g_ref[...]).astype(BF16)


def _ffn_gate_up(rows, wg, wu, h_ref):
    h = h_ref[rows, :]
    return (jnp.dot(h, wg, preferred_element_type=F32), jnp.dot(h, wu, preferred_element_type=F32))


def _ffn_down(rows, gate_up, wd, o_ref):
    gate, up = gate_up
    mid = (gate * _sigmoid(gate) * up).astype(BF16)
    o_ref[rows, :] += jnp.dot(mid, wd, preferred_element_type=F32)


def _ffn_rows(chunks, weights, o_ref, h_ref, between=None):
    wg, wu, wd = weights
    pending = None
    for c, rows in enumerate(chunks):
        gate_up = _ffn_gate_up(rows, wg, wu, h_ref)
        if pending is not None:
            _ffn_down(*pending, wd, o_ref)
        pending = (rows, gate_up)
        if between is not None and c < MIX_PIECES:
            between(c)
    _ffn_down(*pending, wd, o_ref)


def _ffn_weights(wg_ref, wu_ref, wd_ref):
    return wg_ref[...].astype(BF16), wu_ref[...].astype(BF16), wd_ref[...].astype(BF16)


def _ffn_epilogue(fg_ref, o_ref, final_norm):
    if final_norm:
        @pl.when(pl.program_id(1) == pl.num_programs(1) - 1)
        def _():
            o_ref[...] = _rms_scale(o_ref[...], fg_ref[...])


def _ffn_kernel(x_ref, g_ref, wg_ref, wu_ref, wd_ref, fg_ref, *rest, final_norm, n_buf, x_in_hbm):
    o_ref, h_ref = rest[n_buf], rest[-1]
    _ffn_prologue(x_ref, g_ref, o_ref, h_ref, x_in_hbm=x_in_hbm)
    _ffn_rows([slice(0, TM)], _ffn_weights(wg_ref, wu_ref, wd_ref), o_ref, h_ref)
    _ffn_epilogue(fg_ref, o_ref, final_norm)
    for z_ref in rest[n_buf + 1:-1]:
        z_ref[...] = jnp.zeros_like(z_ref)


def _ffn_mix_kernel(*refs, final_norm, n_buf):
    x_hbm, g_ref, wg_ref, wu_ref, wd_ref, fg_ref = refs[:6]
    mix_in = refs[6:6 + N_MIX_IN]
    n_in = 6 + N_MIX_IN + n_buf
    o_ref = refs[n_in]
    mix_out = refs[n_in + 1:n_in + 1 + N_MIX_OUT]
    h_ref = refs[-N_MIX_SCRATCH - 1]
    mix_scratch = refs[-N_MIX_SCRATCH:]
    j = pl.program_id(1)
    active = j < MIX_STEPS
    s = pl.program_id(0) * MIX_STEPS + jnp.minimum(j, MIX_STEPS - 1)
    _ffn_prologue(x_hbm, g_ref, o_ref, h_ref, x_in_hbm=True)
    _mixer_resets(s, active, mix_scratch)

    @pl.when(active)
    def _():
        ysq = [None]

        def piece(c):
            ysq[0] = _mixer_piece(c, s, ysq[0], mix_in, mix_out, mix_scratch)

        _ffn_rows(MIX_CHUNKS, _ffn_weights(wg_ref, wu_ref, wd_ref), o_ref, h_ref, between=piece)

    @pl.when(jnp.logical_not(active))
    def _():
        _ffn_rows([slice(0, TM)], _ffn_weights(wg_ref, wu_ref, wd_ref), o_ref, h_ref)

    _ffn_epilogue(fg_ref, o_ref, final_norm)


def _ffn(x, g, wg, wu, wd, fg, layer, final_norm, mix=None, out_rows=None, out_tile0=0,
         out_buf=None, zero_rows=None):
    rows, d = x.shape
    out_rows = rows if out_rows is None else out_rows
    grid = (rows // TM, wg.shape[2] // FFN_TF)
    x_in_hbm = mix is not None or zero_rows is not None
    x_spec = pl.BlockSpec(memory_space=pl.ANY) if x_in_hbm else pl.BlockSpec((TM, d), lambda i, j: (i, 0))
    in_specs = [x_spec,
                pl.BlockSpec((None, 1, d), lambda i, j: (layer, 0, 0)),
                pl.BlockSpec((None, d, FFN_TF), lambda i, j: (layer, 0, j)),
                pl.BlockSpec((None, d, FFN_TF), lambda i, j: (layer, 0, j)),
                pl.BlockSpec((None, FFN_TF, d), lambda i, j: (layer, j, 0)),
                pl.BlockSpec((1, d), lambda i, j: (0, 0))]
    out_spec = pl.BlockSpec((TM, d), lambda i, j: (out_tile0 + i, 0))
    out_shape = jax.ShapeDtypeStruct((out_rows, d), F32)
    h_scratch = pltpu.VMEM((TM, d), BF16)
    operands = (x, g, wg, wu, wd, fg)
    buf_specs, bufs = ([], ()) if out_buf is None else ([pl.BlockSpec(memory_space=pl.ANY)], (out_buf,))
    if mix is None:
        out_specs, out_shapes = [out_spec], [out_shape]
        if zero_rows is not None:
            per_tile = ZERO_SLABS_PER_TILE
            slab = zero_rows // (grid[0] * per_tile)
            out_specs.append(pl.BlockSpec(
                (slab, d), lambda i, j: (i * per_tile + jnp.minimum(j, per_tile - 1), 0)))
            out_shapes.append(jax.ShapeDtypeStruct((zero_rows, d), F32))
        outs = pl.pallas_call(
            functools.partial(_ffn_kernel, final_norm=final_norm, n_buf=len(bufs), x_in_hbm=x_in_hbm),
            grid=grid, in_specs=in_specs + buf_specs, out_specs=out_specs, out_shape=out_shapes,
            scratch_shapes=[h_scratch],
            input_output_aliases={len(operands): 0} if bufs else {},
            compiler_params=_params("parallel", "arbitrary", vmem=VMEM_LIMIT_FFN),
            name="ffn_final" if final_norm else "ffn",
        )(*operands, *bufs)
        return outs[0] if zero_rows is None else outs
    p_other, mix_params = mix
    m_in, m_out, m_scratch = _mixer_specs(
        lambda i, j: i * MIX_STEPS + jnp.minimum(j, MIX_STEPS - 1), layer)
    operands = operands + _mixer_operands(p_other, mix_params)
    return pl.pallas_call(
        functools.partial(_ffn_mix_kernel, final_norm=final_norm, n_buf=len(bufs)),
        grid=grid, in_specs=in_specs + m_in + buf_specs, out_specs=[out_spec] + m_out,
        out_shape=[out_shape] + _mixer_out_shapes(p_other.shape[0]),
        scratch_shapes=[h_scratch] + m_scratch,
        input_output_aliases={len(operands): 0} if bufs else {},
        compiler_params=_params("arbitrary", "arbitrary"),
        name="ffn_mix_final" if final_norm else "ffn_mix",
    )(*operands, *bufs)


def _retention_tables():
    cs = RET_BLOCK
    pos = np.arange(SEQ, dtype=np.float32)
    inv = (1.0 / (ROPE_BASE ** (np.arange(0, HEAD_DIM, 2, dtype=np.float32) / HEAD_DIM))).astype(np.float32)
    ang = jnp.asarray(pos[:, None] * inv[None, :])
    cos, sin = jnp.cos(ang), jnp.sin(ang)
    cfull = jnp.concatenate([cos, cos], axis=-1)
    ssign = jnp.concatenate([-sin, sin], axis=-1)
    kscale = HEAD_DIM ** -0.5
    gamma_log = np.log1p(-np.exp2(-5.0 - np.arange(RET_HEADS, dtype=np.float64)))
    idx = np.arange(cs)
    dist = np.abs(idx[:, None] - idx[None, :])
    visible = (idx[None, :] // CHUNK) <= (idx[:, None] // CHUNK)
    dmask = kscale * np.where(visible[None], np.exp(gamma_log[:, None, None] * dist[None]), 0.0)
    qdec = np.exp(gamma_log[:, None] * (idx + 1.0)[None, :])
    kdec = kscale * np.exp(gamma_log[:, None] * (cs - 1.0 - idx)[None, :])
    bcast = lambda a: np.broadcast_to(a[:, :, None], (RET_HEADS, cs, HEAD_DIM))
    decay = np.concatenate([dmask, bcast(qdec), bcast(kdec)], axis=-1)
    return (jnp.concatenate([cfull, ssign], axis=-1), jnp.asarray(decay, F32),
            jnp.asarray(np.exp(gamma_log * cs), F32))


def _block_diag(w):
    per = GATE_BLOCK // LRU_GROUP_DIM
    nb = LRU_GROUPS // per
    w4 = w.reshape(w.shape[0], nb, per, LRU_GROUP_DIM, LRU_GROUP_DIM)
    bd = jnp.einsum('lcipq,ij->lcipjq', w4, jnp.eye(per, dtype=w.dtype))
    return bd.reshape(w.shape[0], nb, GATE_BLOCK, GATE_BLOCK).astype(BF16)


def kernel(x, norm1_g, w_in, ret_gn_g, lru_conv_w, lru_conv_b, lru_wa, lru_ba, lru_wx, lru_bx,
           lru_lambda, lru_norm_g, w_out, norm2_g, ffn_w_gate, ffn_w_up, ffn_w_down, final_g):
    batch, seq, d = x.shape
    depth = w_in.shape[0]
    assert (seq, d) == (SEQ, D_MODEL) and batch % 2 == 0
    rows = batch * seq
    half = rows // 2
    half_tiles = half // TM
    vecs = lambda a: a.reshape(depth, 1, -1).astype(F32)
    rot, decay, sdec = _retention_tables()
    head_rows = jnp.stack([ret_gn_g.reshape(depth, RET_HEADS, HEAD_DIM).astype(F32),
                           jnp.broadcast_to(sdec[None, :, None], (depth, RET_HEADS, HEAD_DIM))], axis=2)
    head = jnp.pad(head_rows, ((0, 0), (0, 0), (0, SUBLANES - 2), (0, 0)))
    vec_rows = jnp.concatenate([lru_conv_w.astype(F32), vecs(lru_conv_b), vecs(lru_ba), vecs(lru_bx),
                                vecs(lru_lambda), vecs(lru_norm_g)], axis=1)
    vec = jnp.pad(vec_rows, ((0, 0), (0, VEC_ROWS - vec_rows.shape[1]), (0, 0)))
    gates = jnp.concatenate([_block_diag(lru_wa), _block_diag(lru_wx)], axis=1)
    mix_params = (rot, decay, head, vec, gates)
    g1, g2, fg = vecs(norm1_g), vecs(norm2_g), final_g.reshape(1, d).astype(F32)

    assert depth >= 2
    xf = x.reshape(rows, d)
    xa, xb = (xf, 0), (xf, half_tiles)
    p_a = _in_proj(*xa, half, g1, w_in, 0)
    for l in range(depth):
        last = l == depth - 1
        ffn_w = (ffn_w_gate, ffn_w_up, ffn_w_down, fg, l, last)
        p_b, ret_a, lru_a = _in_proj(*xb, half, g1, w_in, l, mix=(p_a, mix_params, l))
        x1_a = _out_proj(*xa, ret_a, lru_a, w_out, l)
        if last:
            x2_a, ret_b, lru_b = _ffn(x1_a, g2, *ffn_w, mix=(p_b, mix_params), out_rows=rows,
                                      out_buf=out_buf)
            x1_b = _out_proj(*xb, ret_b, lru_b, w_out, l)
            return _ffn(x1_b, g2, *ffn_w, out_rows=rows, out_tile0=half_tiles,
                        out_buf=x2_a).reshape(batch, seq, d)
        x2_a = _ffn(x1_a, g2, *ffn_w)
        xa = (x2_a, 0)
        p_a, ret_b, lru_b = _in_proj(*xa, half, g1, w_in, l + 1, mix=(p_b, mix_params, l))
        x1_b = _out_proj(*xb, ret_b, lru_b, w_out, l)
        zero = dict(zero_rows=rows) if l == depth - 2 else {}
        x2_b = _ffn(x1_b, g2, *ffn_w, **zero)
        if zero:
            x2_b, out_buf = x2_b
        xb = (x2_b, 0)
```

```python
import functools

import numpy as np
import jax
import jax.numpy as jnp
from jax import lax
from jax.experimental import pallas as pl
from jax.experimental.pallas import tpu as pltpu

D_MODEL = 2048
SEQ = 2048
RET_HEADS = 8
HEAD_DIM = 128
RET_WIDTH = RET_HEADS * HEAD_DIM
LRU_WIDTH = 1024
LRU_GROUPS = 16
LRU_GROUP_DIM = 64
CONV_W = 4
LRU_C = 8.0
IN_WIDTH = 4 * RET_WIDTH + 2 * LRU_WIDTH
CHUNK = 64
ROPE_BASE = 10000.0
EPS = 1e-6

F32 = jnp.float32
BF16 = jnp.bfloat16

SUBLANES = 8
BF16_ROWS = 16
TM = 1024
IN_TN = 768
IN_TN_PLAIN = 1024
FFN_TF = 256
OUT_TN = 1024
MIX_STEPS = IN_WIDTH // IN_TN
LRU_BLOCK = TM // MIX_STEPS
RET_GROUPS = 2
RET_GROUP_HEADS = RET_HEADS // RET_GROUPS
RET_GROUP_WIDTH = RET_GROUP_HEADS * HEAD_DIM
RET_BLOCK = LRU_BLOCK * RET_GROUPS
GATE_BLOCK = 256
VMEM_LIMIT = 52 * 1024 * 1024
VMEM_LIMIT_FFN = 57 * 1024 * 1024
ZERO_SLABS_PER_TILE = 16


def _params(*sem, vmem=VMEM_LIMIT):
    return pltpu.CompilerParams(dimension_semantics=sem, vmem_limit_bytes=vmem)


def _sigmoid(z):
    return 0.5 * jnp.tanh(0.5 * z) + 0.5


def _rms_scale(x, g):
    ms = jnp.mean(x * x, axis=-1, keepdims=True)
    return x * lax.rsqrt(ms + EPS) * g


def _retention_head(h, hg, q_ref, k_ref, v_ref, g_ref, rot_ref, decay_ref, head_ref, o_ref, s_ref):
    cos, sin = rot_ref[:, :HEAD_DIM], rot_ref[:, HEAD_DIM:]
    half = HEAD_DIM // 2
    sl = slice(h * HEAD_DIM, (h + 1) * HEAD_DIM)
    q = q_ref[:, sl].astype(F32)
    k = k_ref[:, sl].astype(F32)
    v = v_ref[:, sl]
    qr = q * cos + pltpu.roll(q, half, 1) * sin
    kr = k * cos + pltpu.roll(k, half, 1) * sin
    qb = qr.astype(BF16)
    kb = kr.astype(BF16)
    sidx = hg * RET_GROUP_HEADS + h
    dmask = decay_ref[sidx, :, 0:RET_BLOCK]
    qdec = decay_ref[sidx, :, RET_BLOCK:RET_BLOCK + HEAD_DIM]
    kdec = decay_ref[sidx, :, RET_BLOCK + HEAD_DIM:RET_BLOCK + 2 * HEAD_DIM]
    gn, sdec = head_ref[sidx, 0:1, :], head_ref[sidx, 1:2, :]
    scores = lax.dot_general(qb, kb, (((1,), (1,)), ((), ())),
                             preferred_element_type=F32) * dmask
    o = jnp.dot(scores.astype(BF16), v, preferred_element_type=F32)
    state = s_ref[sidx]
    o = o + qdec * jnp.dot(qb, state.astype(BF16), preferred_element_type=F32)
    kd = (kr * kdec).astype(BF16)
    s_ref[sidx] = sdec * state + lax.dot_general(
        kd, v, (((0,), (0,)), ((), ())), preferred_element_type=F32)
    mu = jnp.mean(o, axis=-1, keepdims=True)
    oc = o - mu
    var = jnp.mean(oc * oc, axis=-1, keepdims=True)
    on = oc * lax.rsqrt(var + EPS) * gn
    gate = g_ref[:, sl].astype(F32)
    o_ref[:, sl] = (on * (gate * _sigmoid(gate))).astype(o_ref.dtype)


def _lru_cols(c, xy_ref, vec_ref, gates_ref, xpad_ref, a_ref, b_ref, y_ref, carry_ref):
    tt = xy_ref.shape[0]
    pad = SUBLANES
    sl = slice(c * GATE_BLOCK, (c + 1) * GATE_BLOCK)
    ysl = slice(LRU_WIDTH + c * GATE_BLOCK, LRU_WIDTH + (c + 1) * GATE_BLOCK)
    vrow = lambda r: vec_ref[r:r + 1, sl]
    x = xy_ref[:, sl].astype(F32)
    xpad_ref[pad:pad + tt, sl] = x
    xc = vrow(VEC_CONV_B) + vrow(VEC_CONV_W + CONV_W - 1) * x
    for j in range(CONV_W - 1):
        off = pad - (CONV_W - 1) + j
        xc = xc + vrow(VEC_CONV_W + j) * xpad_ref[off:off + tt, sl]
    xpad_ref[0:pad, sl] = x[tt - pad:tt, :]

    lam = vrow(VEC_LAMBDA)
    softplus_neg_lam = jnp.maximum(-lam, 0.0) + jnp.log1p(jnp.exp(-jnp.abs(lam)))
    xc16 = xc.astype(BF16)
    n_blocks = LRU_WIDTH // GATE_BLOCK
    r = _sigmoid(jnp.dot(xc16, gates_ref[c], preferred_element_type=F32) + vrow(VEC_BA))
    i = _sigmoid(jnp.dot(xc16, gates_ref[n_blocks + c], preferred_element_type=F32) + vrow(VEC_BX))
    neg_log_a = LRU_C * r * softplus_neg_lam
    a = jnp.exp(-neg_log_a)
    a_ref[:, sl] = a
    one_minus_a2 = jnp.tanh(neg_log_a) * (a * a + 1.0)
    root = jnp.where(one_minus_a2 > 0.0, one_minus_a2 * lax.rsqrt(one_minus_a2), 0.0)
    b_ref[:, sl] = root * (i * xc)

    row = lax.broadcasted_iota(jnp.int32, (SUBLANES, GATE_BLOCK), 0)
    carry = carry_ref[:, sl]
    for gidx in range(tt // SUBLANES):
        rows = slice(gidx * SUBLANES, (gidx + 1) * SUBLANES)
        a = a_ref[rows, sl]
        b = b_ref[rows, sl]
        for s in (1, 2, 4):
            keep = row >= s
            b = jnp.where(keep, a * pltpu.roll(b, s, 0) + b, b)
            a = jnp.where(keep, a * pltpu.roll(a, s, 0), a)
        h = a * carry + b
        b_ref[rows, sl] = h
        carry = h[SUBLANES - 1:SUBLANES, :]
    carry_ref[:, sl] = carry

    y = b_ref[:, sl] * jax.nn.gelu(xy_ref[:, ysl].astype(F32))
    y_ref[:, sl] = y
    return y * y


def _lru_finish(ysq, vec_ref, o_ref, y_ref):
    ms = jnp.sum(ysq, axis=-1, keepdims=True) * (1.0 / LRU_WIDTH)
    ng = vec_ref[VEC_NORM_G:VEC_NORM_G + 1, :]
    o_ref[...] = (y_ref[...] * lax.rsqrt(ms + EPS) * ng).astype(o_ref.dtype)


VEC_CONV_W, VEC_CONV_B, VEC_BA, VEC_BX, VEC_LAMBDA, VEC_NORM_G = 0, CONV_W, CONV_W + 1, CONV_W + 2, CONV_W + 3, CONV_W + 4
VEC_ROWS = 16
N_MIX_IN = 10
N_MIX_OUT = 2
N_MIX_SCRATCH = 6
MIX_PIECES = LRU_WIDTH // GATE_BLOCK
assert MIX_PIECES == RET_GROUP_HEADS


def _row_chunks(n):
    step = TM // n // (2 * BF16_ROWS) * (2 * BF16_ROWS)
    return [slice(c * step, (c + 1) * step if c < n - 1 else TM) for c in range(n)]


MIX_CHUNKS = _row_chunks(MIX_PIECES + 1)
PIECES_AFTER = [[c] for c in range(MIX_PIECES)] + [[]]
MIX_CHUNKS_B = _row_chunks(3)
PIECES_AFTER_B = [[0, 1], [2, 3], []]


def _mixer_resets(s, active, mix_scratch):
    s_ref, xpad_ref, _, _, _, carry_ref = mix_scratch
    lru_blocks = SEQ // LRU_BLOCK
    hg = s % RET_GROUPS

    @pl.when(jnp.logical_and(active, s % lru_blocks == 0))
    def _():
        xpad_ref[0:SUBLANES, :] = jnp.zeros((SUBLANES, LRU_WIDTH), F32)
        carry_ref[...] = jnp.zeros_like(carry_ref)

    @pl.when(jnp.logical_and(active, (s // RET_GROUPS) % (SEQ // RET_BLOCK) == 0))
    def _():
        s_ref[pl.ds(hg * RET_GROUP_HEADS, RET_GROUP_HEADS)] = jnp.zeros(
            (RET_GROUP_HEADS, HEAD_DIM, HEAD_DIM), F32)


def _mixer_piece(c, s, ysq, mix_in, mix_out, mix_scratch):
    q, k, v, g, xy, rot, decay, head, vec, gates = mix_in
    o_ret, o_lru = mix_out
    s_ref, xpad_ref, a_ref, b_ref, y_ref, carry_ref = mix_scratch
    _retention_head(c, s % RET_GROUPS, q, k, v, g, rot, decay, head, o_ret, s_ref)
    ysq_c = _lru_cols(c, xy, vec, gates, xpad_ref, a_ref, b_ref, y_ref, carry_ref)
    ysq = ysq_c if ysq is None else ysq + ysq_c
    if c == MIX_PIECES - 1:
        _lru_finish(ysq, vec, o_lru, y_ref)
    return ysq


def _mixer_specs(smap, layer):
    hg = lambda i, j: smap(i, j) % RET_GROUPS
    rblk = lambda i, j: smap(i, j) // RET_GROUPS
    tblk = lambda i, j: rblk(i, j) % (SEQ // RET_BLOCK)
    pcol = lambda c: pl.BlockSpec((RET_BLOCK, RET_GROUP_WIDTH),
                                  lambda i, j, c=c: (rblk(i, j), c * RET_GROUPS + hg(i, j)))
    resident = dict(pipeline_mode=pl.Buffered(1))
    in_specs = [pcol(0), pcol(1), pcol(2), pcol(3),
                pl.BlockSpec((LRU_BLOCK, 2 * LRU_WIDTH),
                             lambda i, j: (smap(i, j), 4 * RET_WIDTH // (2 * LRU_WIDTH))),
                pl.BlockSpec((RET_BLOCK, 2 * HEAD_DIM), lambda i, j: (tblk(i, j), 0)),
                pl.BlockSpec((RET_HEADS, RET_BLOCK, RET_BLOCK + 2 * HEAD_DIM),
                             lambda i, j: (0, 0, 0), **resident),
                pl.BlockSpec((None, RET_HEADS, SUBLANES, HEAD_DIM),
                             lambda i, j: (layer, 0, 0, 0), **resident),
                pl.BlockSpec((None, VEC_ROWS, LRU_WIDTH), lambda i, j: (layer, 0, 0), **resident),
                pl.BlockSpec((None, 2 * LRU_WIDTH // GATE_BLOCK, GATE_BLOCK, GATE_BLOCK),
                             lambda i, j: (layer, 0, 0, 0), **resident)]
    out_specs = [pl.BlockSpec((RET_BLOCK, RET_GROUP_WIDTH), lambda i, j: (rblk(i, j), hg(i, j))),
                 pl.BlockSpec((LRU_BLOCK, LRU_WIDTH), lambda i, j: (smap(i, j), 0))]
    scratch = [pltpu.VMEM((RET_HEADS, HEAD_DIM, HEAD_DIM), F32),
               pltpu.VMEM((LRU_BLOCK + SUBLANES, LRU_WIDTH), F32),
               pltpu.VMEM((LRU_BLOCK, LRU_WIDTH), F32),
               pltpu.VMEM((LRU_BLOCK, LRU_WIDTH), F32),
               pltpu.VMEM((LRU_BLOCK, LRU_WIDTH), F32),
               pltpu.VMEM((1, LRU_WIDTH), F32)]
    return in_specs, out_specs, scratch


def _mixer_out_shapes(rows):
    return [jax.ShapeDtypeStruct((rows, RET_WIDTH), BF16),
            jax.ShapeDtypeStruct((rows, LRU_WIDTH), BF16)]


def _mixer_operands(p, mix):
    return (p,) * 5 + tuple(mix)


def _in_proj_chunk(rows, wb, o_ref, h_ref):
    o_ref[rows, :] = jnp.dot(h_ref[rows, :], wb, preferred_element_type=F32).astype(o_ref.dtype)


def _in_proj_prologue(x_ref, g_ref, h_ref):
    @pl.when(pl.program_id(1) == 0)
    def _():
        h_ref[...] = _rms_scale(x_ref[...], g_ref[...]).astype(BF16)


def _in_proj_kernel(x_ref, g_ref, w_ref, o_ref, h_ref):
    _in_proj_prologue(x_ref, g_ref, h_ref)
    _in_proj_chunk(slice(0, TM), w_ref[...].astype(BF16), o_ref, h_ref)


def _in_proj_mix_kernel(*refs, variant):
    x_ref, g_ref, w_ref = refs[:3]
    mix_in = refs[3:3 + N_MIX_IN]
    o_ref = refs[3 + N_MIX_IN]
    mix_out = refs[4 + N_MIX_IN:4 + N_MIX_IN + N_MIX_OUT]
    h_ref = refs[-N_MIX_SCRATCH - 1]
    mix_scratch = refs[-N_MIX_SCRATCH:]
    s = pl.program_id(0) * MIX_STEPS + pl.program_id(1)
    _in_proj_prologue(x_ref, g_ref, h_ref)
    _mixer_resets(s, True, mix_scratch)
    wb = w_ref[...].astype(BF16)
    ysq = None
    chunks, after = (MIX_CHUNKS_B, PIECES_AFTER_B) if variant else (MIX_CHUNKS, PIECES_AFTER)
    for rows, pieces in zip(chunks, after):
        _in_proj_chunk(rows, wb, o_ref, h_ref)
        for c in pieces:
            ysq = _mixer_piece(c, s, ysq, mix_in, mix_out, mix_scratch)


def _in_proj_specs(x_tile0, layer, tn):
    d = D_MODEL
    in_specs = [pl.BlockSpec((TM, d), lambda i, j: (x_tile0 + i, 0)),
                pl.BlockSpec((None, 1, d), lambda i, j: (layer, 0, 0)),
                pl.BlockSpec((None, d, tn), lambda i, j: (layer, 0, j))]
    out_spec = pl.BlockSpec((TM, tn), lambda i, j: (i, j))
    return in_specs, out_spec


def _in_proj(x, x_tile0, rows, g, w, layer, mix=None):
    tn = IN_TN if mix is not None else IN_TN_PLAIN
    in_specs, out_spec = _in_proj_specs(x_tile0, layer, tn)
    grid = (rows // TM, IN_WIDTH // tn)
    p_shape = jax.ShapeDtypeStruct((rows, IN_WIDTH), BF16)
    h_scratch = pltpu.VMEM((TM, D_MODEL), BF16)
    if mix is None:
        return pl.pallas_call(
            _in_proj_kernel, grid=grid, in_specs=in_specs, out_specs=out_spec,
            out_shape=p_shape, scratch_shapes=[h_scratch],
            compiler_params=_params("parallel", "arbitrary"), name="in_proj",
        )(x, g, w)
    p_other, mix_params, mix_layer = mix
    m_in, m_out, m_scratch = _mixer_specs(lambda i, j: i * MIX_STEPS + j, mix_layer)
    return pl.pallas_call(
        functools.partial(_in_proj_mix_kernel, variant=layer > 0), grid=grid,
        in_specs=in_specs + m_in, out_specs=[out_spec] + m_out,
        out_shape=[p_shape] + _mixer_out_shapes(p_other.shape[0]),
        scratch_shapes=[h_scratch] + m_scratch,
        compiler_params=_params("arbitrary", "arbitrary"), name="in_proj_mix",
    )(x, g, w, *_mixer_operands(p_other, mix_params))


def _out_proj_kernel(x_ref, r_ref, l_ref, w_ref, o_ref):
    acc = jnp.dot(r_ref[...], w_ref[0:RET_WIDTH, :].astype(BF16), preferred_element_type=F32)
    acc = acc + jnp.dot(l_ref[...], w_ref[RET_WIDTH:, :].astype(BF16), preferred_element_type=F32)
    o_ref[...] = x_ref[...] + acc


def _out_proj(x, x_tile0, o_ret, o_lru, w, layer):
    rows = o_ret.shape[0]
    d = D_MODEL
    return pl.pallas_call(
        _out_proj_kernel,
        grid=(d // OUT_TN, rows // TM),
        in_specs=[pl.BlockSpec((TM, OUT_TN), lambda j, i: (x_tile0 + i, j)),
                  pl.BlockSpec((TM, RET_WIDTH), lambda j, i: (i, 0)),
                  pl.BlockSpec((TM, LRU_WIDTH), lambda j, i: (i, 0)),
                  pl.BlockSpec((None, d, OUT_TN), lambda j, i: (layer, 0, j))],
        out_specs=pl.BlockSpec((TM, OUT_TN), lambda j, i: (i, j)),
        out_shape=jax.ShapeDtypeStruct((rows, d), F32),
        compiler_params=_params("parallel", "arbitrary"),
        name="out_proj",
    )(x, o_ret, o_lru, w)


def _ffn_prologue(x_ref, g_ref, o_ref, h_ref, x_in_hbm):
    @pl.when(pl.program_id(1) == 0)
    def _():
        if x_in_hbm:
            row0 = pl.multiple_of(pl.program_id(0) * TM, TM)
            pltpu.sync_copy(x_ref.at[pl.ds(row0, TM), :], o_ref)
        else:
            o_ref[...] = x_ref[...]
        h_ref[...] = _rms_scale(o_ref[...], g_ref[...]).astype(BF16)


def _ffn_gate_up(rows, wg, wu, h_ref):
    h = h_ref[rows, :]
    return (jnp.dot(h, wg, preferred_element_type=F32), jnp.dot(h, wu, preferred_element_type=F32))


def _ffn_down(rows, gate_up, wd, o_ref):
    gate, up = gate_up
    mid = (gate * _sigmoid(gate) * up).astype(BF16)
    o_ref[rows, :] += jnp.dot(mid, wd, preferred_element_type=F32)


def _ffn_rows(chunks, weights, o_ref, h_ref, between=None, after=None):
    wg, wu, wd = weights
    pending = None
    for c, rows in enumerate(chunks):
        gate_up = _ffn_gate_up(rows, wg, wu, h_ref)
        if pending is not None:
            _ffn_down(*pending, wd, o_ref)
        pending = (rows, gate_up)
        if between is not None:
            for piece in after[c]:
                between(piece)
    _ffn_down(*pending, wd, o_ref)


def _ffn_weights(wg_ref, wu_ref, wd_ref):
    return wg_ref[...].astype(BF16), wu_ref[...].astype(BF16), wd_ref[...].astype(BF16)


def _ffn_epilogue(fg_ref, o_ref, final_norm):
    if final_norm:
        @pl.when(pl.program_id(1) == pl.num_programs(1) - 1)
        def _():
            o_ref[...] = _rms_scale(o_ref[...], fg_ref[...])


def _ffn_kernel(x_ref, g_ref, wg_ref, wu_ref, wd_ref, fg_ref, *rest, final_norm, n_buf, x_in_hbm):
    o_ref, h_ref = rest[n_buf], rest[-1]
    _ffn_prologue(x_ref, g_ref, o_ref, h_ref, x_in_hbm=x_in_hbm)
    _ffn_rows([slice(0, TM)], _ffn_weights(wg_ref, wu_ref, wd_ref), o_ref, h_ref)
    _ffn_epilogue(fg_ref, o_ref, final_norm)
    for z_ref in rest[n_buf + 1:-1]:
        z_ref[...] = jnp.zeros_like(z_ref)


def _ffn_mix_kernel(*refs, final_norm, n_buf):
    x_hbm, g_ref, wg_ref, wu_ref, wd_ref, fg_ref = refs[:6]
    mix_in = refs[6:6 + N_MIX_IN]
    n_in = 6 + N_MIX_IN + n_buf
    o_ref = refs[n_in]
    mix_out = refs[n_in + 1:n_in + 1 + N_MIX_OUT]
    h_ref = refs[-N_MIX_SCRATCH - 1]
    mix_scratch = refs[-N_MIX_SCRATCH:]
    j = pl.program_id(1)
    active = j < MIX_STEPS
    s = pl.program_id(0) * MIX_STEPS + jnp.minimum(j, MIX_STEPS - 1)
    _ffn_prologue(x_hbm, g_ref, o_ref, h_ref, x_in_hbm=True)
    _mixer_resets(s, active, mix_scratch)

    @pl.when(active)
    def _():
        ysq = [None]

        def piece(c):
            ysq[0] = _mixer_piece(c, s, ysq[0], mix_in, mix_out, mix_scratch)

        _ffn_rows(MIX_CHUNKS_B, _ffn_weights(wg_ref, wu_ref, wd_ref), o_ref, h_ref, between=piece,
                  after=PIECES_AFTER_B)

    @pl.when(jnp.logical_not(active))
    def _():
        _ffn_rows([slice(0, TM)], _ffn_weights(wg_ref, wu_ref, wd_ref), o_ref, h_ref)

    _ffn_epilogue(fg_ref, o_ref, final_norm)


def _ffn(x, g, wg, wu, wd, fg, layer, final_norm, mix=None, out_rows=None, out_tile0=0,
         out_buf=None, zero_rows=None):
    rows, d = x.shape
    out_rows = rows if out_rows is None else out_rows
    grid = (rows // TM, wg.shape[2] // FFN_TF)
    x_in_hbm = mix is not None or zero_rows is not None
    x_spec = pl.BlockSpec(memory_space=pl.ANY) if x_in_hbm else pl.BlockSpec((TM, d), lambda i, j: (i, 0))
    in_specs = [x_spec,
                pl.BlockSpec((None, 1, d), lambda i, j: (layer, 0, 0)),
                pl.BlockSpec((None, d, FFN_TF), lambda i, j: (layer, 0, j)),
                pl.BlockSpec((None, d, FFN_TF), lambda i, j: (layer, 0, j)),
                pl.BlockSpec((None, FFN_TF, d), lambda i, j: (layer, j, 0)),
                pl.BlockSpec((1, d), lambda i, j: (0, 0))]
    out_spec = pl.BlockSpec((TM, d), lambda i, j: (out_tile0 + i, 0))
    out_shape = jax.ShapeDtypeStruct((out_rows, d), F32)
    h_scratch = pltpu.VMEM((TM, d), BF16)
    operands = (x, g, wg, wu, wd, fg)
    buf_specs, bufs = ([], ()) if out_buf is None else ([pl.BlockSpec(memory_space=pl.ANY)], (out_buf,))
    if mix is None:
        out_specs, out_shapes = [out_spec], [out_shape]
        if zero_rows is not None:
            per_tile = ZERO_SLABS_PER_TILE
            slab = zero_rows // (grid[0] * per_tile)
            out_specs.append(pl.BlockSpec(
                (slab, d), lambda i, j: (i * per_tile + jnp.minimum(j, per_tile - 1), 0)))
            out_shapes.append(jax.ShapeDtypeStruct((zero_rows, d), F32))
        outs = pl.pallas_call(
            functools.partial(_ffn_kernel, final_norm=final_norm, n_buf=len(bufs), x_in_hbm=x_in_hbm),
            grid=grid, in_specs=in_specs + buf_specs, out_specs=out_specs, out_shape=out_shapes,
            scratch_shapes=[h_scratch],
            input_output_aliases={len(operands): 0} if bufs else {},
            compiler_params=_params("parallel", "arbitrary", vmem=VMEM_LIMIT_FFN),
            name="ffn_final" if final_norm else "ffn",
        )(*operands, *bufs)
        return outs[0] if zero_rows is None else outs
    p_other, mix_params = mix
    m_in, m_out, m_scratch = _mixer_specs(
        lambda i, j: i * MIX_STEPS + jnp.minimum(j, MIX_STEPS - 1), layer)
    operands = operands + _mixer_operands(p_other, mix_params)
    return pl.pallas_call(
        functools.partial(_ffn_mix_kernel, final_norm=final_norm, n_buf=len(bufs)),
        grid=grid, in_specs=in_specs + m_in + buf_specs, out_specs=[out_spec] + m_out,
        out_shape=[out_shape] + _mixer_out_shapes(p_other.shape[0]),
        scratch_shapes=[h_scratch] + m_scratch,
        input_output_aliases={len(operands): 0} if bufs else {},
        compiler_params=_params("arbitrary", "arbitrary"),
        name="ffn_mix_final" if final_norm else "ffn_mix",
    )(*operands, *bufs)


def _retention_tables():
    cs = RET_BLOCK
    pos = np.arange(SEQ, dtype=np.float32)
    inv = (1.0 / (ROPE_BASE ** (np.arange(0, HEAD_DIM, 2, dtype=np.float32) / HEAD_DIM))).astype(np.float32)
    ang = jnp.asarray(pos[:, None] * inv[None, :])
    cos, sin = jnp.cos(ang), jnp.sin(ang)
    cfull = jnp.concatenate([cos, cos], axis=-1)
    ssign = jnp.concatenate([-sin, sin], axis=-1)
    kscale = HEAD_DIM ** -0.5
    gamma_log = np.log1p(-np.exp2(-5.0 - np.arange(RET_HEADS, dtype=np.float64)))
    idx = np.arange(cs)
    dist = np.abs(idx[:, None] - idx[None, :])
    visible = (idx[None, :] // CHUNK) <= (idx[:, None] // CHUNK)
    dmask = kscale * np.where(visible[None], np.exp(gamma_log[:, None, None] * dist[None]), 0.0)
    qdec = np.exp(gamma_log[:, None] * (idx + 1.0)[None, :])
    kdec = kscale * np.exp(gamma_log[:, None] * (cs - 1.0 - idx)[None, :])
    bcast = lambda a: np.broadcast_to(a[:, :, None], (RET_HEADS, cs, HEAD_DIM))
    decay = np.concatenate([dmask, bcast(qdec), bcast(kdec)], axis=-1)
    return (jnp.concatenate([cfull, ssign], axis=-1), jnp.asarray(decay, F32),
            jnp.asarray(np.exp(gamma_log * cs), F32))


def _block_diag(w):
    per = GATE_BLOCK // LRU_GROUP_DIM
    nb = LRU_GROUPS // per
    w4 = w.reshape(w.shape[0], nb, per, LRU_GROUP_DIM, LRU_GROUP_DIM)
    bd = jnp.einsum('lcipq,ij->lcipjq', w4, jnp.eye(per, dtype=w.dtype))
    return bd.reshape(w.shape[0], nb, GATE_BLOCK, GATE_BLOCK).astype(BF16)


def kernel(x, norm1_g, w_in, ret_gn_g, lru_conv_w, lru_conv_b, lru_wa, lru_ba, lru_wx, lru_bx,
           lru_lambda, lru_norm_g, w_out, norm2_g, ffn_w_gate, ffn_w_up, ffn_w_down, final_g):
    batch, seq, d = x.shape
    depth = w_in.shape[0]
    assert (seq, d) == (SEQ, D_MODEL) and batch % 2 == 0
    rows = batch * seq
    half = rows // 2
    half_tiles = half // TM
    vecs = lambda a: a.reshape(depth, 1, -1).astype(F32)
    rot, decay, sdec = _retention_tables()
    head_rows = jnp.stack([ret_gn_g.reshape(depth, RET_HEADS, HEAD_DIM).astype(F32),
                           jnp.broadcast_to(sdec[None, :, None], (depth, RET_HEADS, HEAD_DIM))], axis=2)
    head = jnp.pad(head_rows, ((0, 0), (0, 0), (0, SUBLANES - 2), (0, 0)))
    vec_rows = jnp.concatenate([lru_conv_w.astype(F32), vecs(lru_conv_b), vecs(lru_ba), vecs(lru_bx),
                                vecs(lru_lambda), vecs(lru_norm_g)], axis=1)
    vec = jnp.pad(vec_rows, ((0, 0), (0, VEC_ROWS - vec_rows.shape[1]), (0, 0)))
    gates = jnp.concatenate([_block_diag(lru_wa), _block_diag(lru_wx)], axis=1)
    mix_params = (rot, decay, head, vec, gates)
    g1, g2, fg = vecs(norm1_g), vecs(norm2_g), final_g.reshape(1, d).astype(F32)

    assert depth >= 2
    xf = x.reshape(rows, d)
    xa, xb = (xf, 0), (xf, half_tiles)
    p_a = _in_proj(*xa, half, g1, w_in, 0)
    for l in range(depth):
        last = l == depth - 1
        ffn_w = (ffn_w_gate, ffn_w_up, ffn_w_down, fg, l, last)
        p_b, ret_a, lru_a = _in_proj(*xb, half, g1, w_in, l, mix=(p_a, mix_params, l))
        x1_a = _out_proj(*xa, ret_a, lru_a, w_out, l)
        if last:
            x2_a, ret_b, lru_b = _ffn(x1_a, g2, *ffn_w, mix=(p_b, mix_params), out_rows=rows,
                                      out_buf=out_buf)
            x1_b = _out_proj(*xb, ret_b, lru_b, w_out, l)
            return _ffn(x1_b, g2, *ffn_w, out_rows=rows, out_tile0=half_tiles,
                        out_buf=x2_a).reshape(batch, seq, d)
        x2_a = _ffn(x1_a, g2, *ffn_w)
        xa = (x2_a, 0)
        p_a, ret_b, lru_b = _in_proj(*xa, half, g1, w_in, l + 1, mix=(p_b, mix_params, l))
        x1_b = _out_proj(*xb, ret_b, lru_b, w_out, l)
        zero = dict(zero_rows=rows) if l == depth - 2 else {}
        x2_b = _ffn(x1_b, g2, *ffn_w, **zero)
        if zero:
            x2_b, out_buf = x2_b
        xb = (x2_b, 0)
```

```python
import functools

import numpy as np
import jax
import jax.numpy as jnp
from jax import lax
from jax.experimental import pallas as pl
from jax.experimental.pallas import tpu as pltpu

D_MODEL = 2048
SEQ = 2048
RET_HEADS = 8
HEAD_DIM = 128
RET_WIDTH = RET_HEADS * HEAD_DIM
LRU_WIDTH = 1024
LRU_GROUPS = 16
LRU_GROUP_DIM = 64
CONV_W = 4
LRU_C = 8.0
IN_WIDTH = 4 * RET_WIDTH + 2 * LRU_WIDTH
CHUNK = 64
ROPE_BASE = 10000.0
EPS = 1e-6

F32 = jnp.float32
BF16 = jnp.bfloat16

SUBLANES = 8
BF16_ROWS = 16
TM = 1024
IN_TN = 768
IN_TN_PLAIN = 1024
FFN_TF = 256
OUT_TN = 1024
MIX_STEPS = IN_WIDTH // IN_TN
LRU_BLOCK = TM // MIX_STEPS
RET_GROUPS = 2
RET_GROUP_HEADS = RET_HEADS // RET_GROUPS
RET_GROUP_WIDTH = RET_GROUP_HEADS * HEAD_DIM
RET_BLOCK = LRU_BLOCK * RET_GROUPS
GATE_BLOCK = 256
VMEM_LIMIT = 55 * 1024 * 1024
VMEM_LIMIT_FFN = 57 * 1024 * 1024
ZERO_SLABS_PER_TILE = 16


def _params(*sem, vmem=VMEM_LIMIT):
    return pltpu.CompilerParams(dimension_semantics=sem, vmem_limit_bytes=vmem)


def _sigmoid(z):
    return 0.5 * jnp.tanh(0.5 * z) + 0.5


def _rms_scale(x, g):
    ms = jnp.mean(x * x, axis=-1, keepdims=True)
    return x * lax.rsqrt(ms + EPS) * g


def _retention_head(h, hg, q_ref, k_ref, v_ref, g_ref, rot_ref, decay_ref, head_ref, o_ref, s_ref):
    cos, sin = rot_ref[:, :HEAD_DIM], rot_ref[:, HEAD_DIM:]
    half = HEAD_DIM // 2
    sl = slice(h * HEAD_DIM, (h + 1) * HEAD_DIM)
    q = q_ref[:, sl].astype(F32)
    k = k_ref[:, sl].astype(F32)
    v = v_ref[:, sl]
    qr = q * cos + pltpu.roll(q, half, 1) * sin
    kr = k * cos + pltpu.roll(k, half, 1) * sin
    qb = qr.astype(BF16)
    kb = kr.astype(BF16)
    sidx = hg * RET_GROUP_HEADS + h
    dmask = decay_ref[sidx, :, 0:RET_BLOCK]
    qdec = decay_ref[sidx, :, RET_BLOCK:RET_BLOCK + HEAD_DIM]
    kdec = decay_ref[sidx, :, RET_BLOCK + HEAD_DIM:RET_BLOCK + 2 * HEAD_DIM]
    gn, sdec = head_ref[sidx, 0:1, :], head_ref[sidx, 1:2, :]
    scores = lax.dot_general(qb, kb, (((1,), (1,)), ((), ())),
                             preferred_element_type=F32) * dmask
    o = jnp.dot(scores.astype(BF16), v, preferred_element_type=F32)
    state = s_ref[sidx]
    o = o + qdec * jnp.dot(qb, state.astype(BF16), preferred_element_type=F32)
    kd = (kr * kdec).astype(BF16)
    s_ref[sidx] = sdec * state + lax.dot_general(
        kd, v, (((0,), (0,)), ((), ())), preferred_element_type=F32)
    mu = jnp.mean(o, axis=-1, keepdims=True)
    oc = o - mu
    var = jnp.mean(oc * oc, axis=-1, keepdims=True)
    on = oc * lax.rsqrt(var + EPS) * gn
    gate = g_ref[:, sl].astype(F32)
    o_ref[:, sl] = (on * (gate * _sigmoid(gate))).astype(o_ref.dtype)


def _lru_cols(c, xy_ref, vec_ref, gates_ref, xpad_ref, a_ref, b_ref, y_ref, carry_ref):
    tt = xy_ref.shape[0]
    pad = SUBLANES
    sl = slice(c * GATE_BLOCK, (c + 1) * GATE_BLOCK)
    ysl = slice(LRU_WIDTH + c * GATE_BLOCK, LRU_WIDTH + (c + 1) * GATE_BLOCK)
    vrow = lambda r: vec_ref[r:r + 1, sl]
    x = xy_ref[:, sl].astype(F32)
    xpad_ref[pad:pad + tt, sl] = x
    xc = vrow(VEC_CONV_B) + vrow(VEC_CONV_W + CONV_W - 1) * x
    for j in range(CONV_W - 1):
        off = pad - (CONV_W - 1) + j
        xc = xc + vrow(VEC_CONV_W + j) * xpad_ref[off:off + tt, sl]
    xpad_ref[0:pad, sl] = x[tt - pad:tt, :]

    lam = vrow(VEC_LAMBDA)
    softplus_neg_lam = jnp.maximum(-lam, 0.0) + jnp.log1p(jnp.exp(-jnp.abs(lam)))
    xc16 = xc.astype(BF16)
    n_blocks = LRU_WIDTH // GATE_BLOCK
    r = _sigmoid(jnp.dot(xc16, gates_ref[c], preferred_element_type=F32) + vrow(VEC_BA))
    i = _sigmoid(jnp.dot(xc16, gates_ref[n_blocks + c], preferred_element_type=F32) + vrow(VEC_BX))
    neg_log_a = LRU_C * r * softplus_neg_lam
    a = jnp.exp(-neg_log_a)
    a_ref[:, sl] = a
    one_minus_a2 = jnp.tanh(neg_log_a) * (a * a + 1.0)
    root = jnp.where(one_minus_a2 > 0.0, one_minus_a2 * lax.rsqrt(one_minus_a2), 0.0)
    b_ref[:, sl] = root * (i * xc)

    row = lax.broadcasted_iota(jnp.int32, (SUBLANES, GATE_BLOCK), 0)
    carry = carry_ref[:, sl]
    for gidx in range(tt // SUBLANES):
        rows = slice(gidx * SUBLANES, (gidx + 1) * SUBLANES)
        a = a_ref[rows, sl]
        b = b_ref[rows, sl]
        for s in (1, 2, 4):
            keep = row >= s
            b = jnp.where(keep, a * pltpu.roll(b, s, 0) + b, b)
            a = jnp.where(keep, a * pltpu.roll(a, s, 0), a)
        h = a * carry + b
        b_ref[rows, sl] = h
        carry = h[SUBLANES - 1:SUBLANES, :]
    carry_ref[:, sl] = carry

    y = b_ref[:, sl] * jax.nn.gelu(xy_ref[:, ysl].astype(F32))
    y_ref[:, sl] = y
    return y * y


def _lru_finish(ysq, vec_ref, o_ref, y_ref):
    ms = jnp.sum(ysq, axis=-1, keepdims=True) * (1.0 / LRU_WIDTH)
    ng = vec_ref[VEC_NORM_G:VEC_NORM_G + 1, :]
    o_ref[...] = (y_ref[...] * lax.rsqrt(ms + EPS) * ng).astype(o_ref.dtype)


VEC_CONV_W, VEC_CONV_B, VEC_BA, VEC_BX, VEC_LAMBDA, VEC_NORM_G = 0, CONV_W, CONV_W + 1, CONV_W + 2, CONV_W + 3, CONV_W + 4
VEC_ROWS = 16
N_MIX_IN = 10
N_MIX_OUT = 2
N_MIX_SCRATCH = 6
MIX_PIECES = LRU_WIDTH // GATE_BLOCK
assert MIX_PIECES == RET_GROUP_HEADS


def _row_chunks(n):
    step = TM // n // (2 * BF16_ROWS) * (2 * BF16_ROWS)
    return [slice(c * step, (c + 1) * step if c < n - 1 else TM) for c in range(n)]


MIX_CHUNKS = _row_chunks(MIX_PIECES + 1)
PIECES_AFTER = [[c] for c in range(MIX_PIECES)] + [[]]
NORM_CHUNKS = _row_chunks(4)
NO_PIECES = [[]] * len(NORM_CHUNKS)


def _mixer_resets(s, active, mix_scratch):
    s_ref, xpad_ref, _, _, _, carry_ref = mix_scratch
    lru_blocks = SEQ // LRU_BLOCK
    hg = s % RET_GROUPS

    @pl.when(jnp.logical_and(active, s % lru_blocks == 0))
    def _():
        xpad_ref[0:SUBLANES, :] = jnp.zeros((SUBLANES, LRU_WIDTH), F32)
        carry_ref[...] = jnp.zeros_like(carry_ref)

    @pl.when(jnp.logical_and(active, (s // RET_GROUPS) % (SEQ // RET_BLOCK) == 0))
    def _():
        s_ref[pl.ds(hg * RET_GROUP_HEADS, RET_GROUP_HEADS)] = jnp.zeros(
            (RET_GROUP_HEADS, HEAD_DIM, HEAD_DIM), F32)


def _mixer_piece(c, s, ysq, mix_in, mix_out, mix_scratch):
    q, k, v, g, xy, rot, decay, head, vec, gates = mix_in
    o_ret, o_lru = mix_out
    s_ref, xpad_ref, a_ref, b_ref, y_ref, carry_ref = mix_scratch
    _retention_head(c, s % RET_GROUPS, q, k, v, g, rot, decay, head, o_ret, s_ref)
    ysq_c = _lru_cols(c, xy, vec, gates, xpad_ref, a_ref, b_ref, y_ref, carry_ref)
    ysq = ysq_c if ysq is None else ysq + ysq_c
    if c == MIX_PIECES - 1:
        _lru_finish(ysq, vec, o_lru, y_ref)
    return ysq


def _mixer_specs(smap, layer):
    hg = lambda i, j: smap(i, j) % RET_GROUPS
    rblk = lambda i, j: smap(i, j) // RET_GROUPS
    tblk = lambda i, j: rblk(i, j) % (SEQ // RET_BLOCK)
    pcol = lambda c: pl.BlockSpec((RET_BLOCK, RET_GROUP_WIDTH),
                                  lambda i, j, c=c: (rblk(i, j), c * RET_GROUPS + hg(i, j)))
    resident = dict(pipeline_mode=pl.Buffered(1))
    in_specs = [pcol(0), pcol(1), pcol(2), pcol(3),
                pl.BlockSpec((LRU_BLOCK, 2 * LRU_WIDTH),
                             lambda i, j: (smap(i, j), 4 * RET_WIDTH // (2 * LRU_WIDTH))),
                pl.BlockSpec((RET_BLOCK, 2 * HEAD_DIM), lambda i, j: (tblk(i, j), 0)),
                pl.BlockSpec((RET_HEADS, RET_BLOCK, RET_BLOCK + 2 * HEAD_DIM),
                             lambda i, j: (0, 0, 0), **resident),
                pl.BlockSpec((None, RET_HEADS, SUBLANES, HEAD_DIM),
                             lambda i, j: (layer, 0, 0, 0), **resident),
                pl.BlockSpec((None, VEC_ROWS, LRU_WIDTH), lambda i, j: (layer, 0, 0), **resident),
                pl.BlockSpec((None, 2 * LRU_WIDTH // GATE_BLOCK, GATE_BLOCK, GATE_BLOCK),
                             lambda i, j: (layer, 0, 0, 0), **resident)]
    out_specs = [pl.BlockSpec((RET_BLOCK, RET_GROUP_WIDTH), lambda i, j: (rblk(i, j), hg(i, j))),
                 pl.BlockSpec((LRU_BLOCK, LRU_WIDTH), lambda i, j: (smap(i, j), 0))]
    scratch = [pltpu.VMEM((RET_HEADS, HEAD_DIM, HEAD_DIM), F32),
               pltpu.VMEM((LRU_BLOCK + SUBLANES, LRU_WIDTH), F32),
               pltpu.VMEM((LRU_BLOCK, LRU_WIDTH), F32),
               pltpu.VMEM((LRU_BLOCK, LRU_WIDTH), F32),
               pltpu.VMEM((LRU_BLOCK, LRU_WIDTH), F32),
               pltpu.VMEM((1, LRU_WIDTH), F32)]
    return in_specs, out_specs, scratch


def _mixer_out_shapes(rows):
    return [jax.ShapeDtypeStruct((rows, RET_WIDTH), BF16),
            jax.ShapeDtypeStruct((rows, LRU_WIDTH), BF16)]


def _mixer_operands(p, mix):
    return (p,) * 5 + tuple(mix)


def _in_proj_chunk(rows, wb, o_ref, h_ref):
    o_ref[rows, :] = jnp.dot(h_ref[rows, :], wb, preferred_element_type=F32).astype(o_ref.dtype)


def _in_proj_rows(chunks, after, wb, o_ref, h_ref, norm=None, piece=None):
    for rows, pieces in zip(chunks, after):
        if norm is not None:
            x_ref, g_ref = norm
            h_ref[rows, :] = _rms_scale(x_ref[rows, :], g_ref[...]).astype(BF16)
        _in_proj_chunk(rows, wb, o_ref, h_ref)
        for c in pieces:
            piece(c)


def _in_proj_kernel(x_ref, g_ref, w_ref, o_ref, h_ref):
    wb = w_ref[...].astype(BF16)
    first = pl.program_id(1) == 0

    @pl.when(first)
    def _():
        _in_proj_rows(NORM_CHUNKS, NO_PIECES, wb, o_ref, h_ref, norm=(x_ref, g_ref))

    @pl.when(jnp.logical_not(first))
    def _():
        _in_proj_chunk(slice(0, TM), wb, o_ref, h_ref)


def _in_proj_mix_kernel(*refs):
    x_ref, g_ref, w_ref = refs[:3]
    mix_in = refs[3:3 + N_MIX_IN]
    o_ref = refs[3 + N_MIX_IN]
    mix_out = refs[4 + N_MIX_IN:4 + N_MIX_IN + N_MIX_OUT]
    h_ref = refs[-N_MIX_SCRATCH - 1]
    mix_scratch = refs[-N_MIX_SCRATCH:]
    s = pl.program_id(0) * MIX_STEPS + pl.program_id(1)
    _mixer_resets(s, True, mix_scratch)
    wb = w_ref[...].astype(BF16)
    first = pl.program_id(1) == 0

    def step(norm):
        ysq = [None]

        def piece(c):
            ysq[0] = _mixer_piece(c, s, ysq[0], mix_in, mix_out, mix_scratch)

        _in_proj_rows(MIX_CHUNKS, PIECES_AFTER, wb, o_ref, h_ref, norm=norm, piece=piece)

    pl.when(first)(lambda: step((x_ref, g_ref)))
    pl.when(jnp.logical_not(first))(lambda: step(None))


def _in_proj_specs(x_tile0, layer, tn):
    d = D_MODEL
    in_specs = [pl.BlockSpec((TM, d), lambda i, j: (x_tile0 + i, 0)),
                pl.BlockSpec((None, 1, d), lambda i, j: (layer, 0, 0)),
                pl.BlockSpec((None, d, tn), lambda i, j: (layer, 0, j))]
    out_spec = pl.BlockSpec((TM, tn), lambda i, j: (i, j))
    return in_specs, out_spec


def _in_proj(x, x_tile0, rows, g, w, layer, mix=None):
    tn = IN_TN if mix is not None else IN_TN_PLAIN
    in_specs, out_spec = _in_proj_specs(x_tile0, layer, tn)
    grid = (rows // TM, IN_WIDTH // tn)
    p_shape = jax.ShapeDtypeStruct((rows, IN_WIDTH), BF16)
    h_scratch = pltpu.VMEM((TM, D_MODEL), BF16)
    if mix is None:
        return pl.pallas_call(
            _in_proj_kernel, grid=grid, in_specs=in_specs, out_specs=out_spec,
            out_shape=p_shape, scratch_shapes=[h_scratch],
            compiler_params=_params("parallel", "arbitrary"), name="in_proj",
        )(x, g, w)
    p_other, mix_params, mix_layer = mix
    m_in, m_out, m_scratch = _mixer_specs(lambda i, j: i * MIX_STEPS + j, mix_layer)
    return pl.pallas_call(
        _in_proj_mix_kernel, grid=grid,
        in_specs=in_specs + m_in, out_specs=[out_spec] + m_out,
        out_shape=[p_shape] + _mixer_out_shapes(p_other.shape[0]),
        scratch_shapes=[h_scratch] + m_scratch,
        compiler_params=_params("arbitrary", "arbitrary"), name="in_proj_mix",
    )(x, g, w, *_mixer_operands(p_other, mix_params))


def _out_proj_kernel(x_ref, r_ref, l_ref, w_ref, o_ref):
    acc = jnp.dot(r_ref[...], w_ref[0:RET_WIDTH, :].astype(BF16), preferred_element_type=F32)
    acc = acc + jnp.dot(l_ref[...], w_ref[RET_WIDTH:, :].astype(BF16), preferred_element_type=F32)
    o_ref[...] = x_ref[...] + acc


def _out_proj(x, x_tile0, o_ret, o_lru, w, layer):
    rows = o_ret.shape[0]
    d = D_MODEL
    return pl.pallas_call(
        _out_proj_kernel,
        grid=(d // OUT_TN, rows // TM),
        in_specs=[pl.BlockSpec((TM, OUT_TN), lambda j, i: (x_tile0 + i, j)),
                  pl.BlockSpec((TM, RET_WIDTH), lambda j, i: (i, 0)),
                  pl.BlockSpec((TM, LRU_WIDTH), lambda j, i: (i, 0)),
                  pl.BlockSpec((None, d, OUT_TN), lambda j, i: (layer, 0, j))],
        out_specs=pl.BlockSpec((TM, OUT_TN), lambda j, i: (i, j)),
        out_shape=jax.ShapeDtypeStruct((rows, d), F32),
        compiler_params=_params("parallel", "arbitrary"),
        name="out_proj",
    )(x, o_ret, o_lru, w)


def _ffn_load_x(x_hbm, o_ref):
    @pl.when(pl.program_id(1) == 0)
    def _():
        row0 = pl.multiple_of(pl.program_id(0) * TM, TM)
        pltpu.sync_copy(x_hbm.at[pl.ds(row0, TM), :], o_ref)


def _ffn_gate_up(rows, wg, wu, h_ref):
    h = h_ref[rows, :]
    return (jnp.dot(h, wg, preferred_element_type=F32), jnp.dot(h, wu, preferred_element_type=F32))


def _ffn_down(rows, gate_up, wd, o_ref):
    gate, up = gate_up
    mid = (gate * _sigmoid(gate) * up).astype(BF16)
    o_ref[rows, :] += jnp.dot(mid, wd, preferred_element_type=F32)


def _ffn_rows(chunks, after, weights, o_ref, h_ref, norm=None, piece=None):
    wg, wu, wd = weights
    pending = None
    for rows, pieces in zip(chunks, after):
        if norm is not None:
            x_ref, g_ref = norm
            if x_ref is not None:
                o_ref[rows, :] = x_ref[rows, :]
            h_ref[rows, :] = _rms_scale(o_ref[rows, :], g_ref[...]).astype(BF16)
        gate_up = _ffn_gate_up(rows, wg, wu, h_ref)
        if pending is not None:
            _ffn_down(*pending, wd, o_ref)
        pending = (rows, gate_up)
        for c in pieces:
            piece(c)
    _ffn_down(*pending, wd, o_ref)


def _ffn_weights(wg_ref, wu_ref, wd_ref):
    return wg_ref[...].astype(BF16), wu_ref[...].astype(BF16), wd_ref[...].astype(BF16)


def _ffn_epilogue(fg_ref, o_ref, final_norm):
    if final_norm:
        @pl.when(pl.program_id(1) == pl.num_programs(1) - 1)
        def _():
            o_ref[...] = _rms_scale(o_ref[...], fg_ref[...])


def _ffn_kernel(x_ref, g_ref, wg_ref, wu_ref, wd_ref, fg_ref, *rest, final_norm, n_buf, x_in_hbm):
    o_ref, h_ref = rest[n_buf], rest[-1]
    if x_in_hbm:
        _ffn_load_x(x_ref, o_ref)
    weights = _ffn_weights(wg_ref, wu_ref, wd_ref)
    first = pl.program_id(1) == 0

    @pl.when(first)
    def _():
        _ffn_rows(NORM_CHUNKS, NO_PIECES, weights, o_ref, h_ref,
                  norm=(None if x_in_hbm else x_ref, g_ref))

    @pl.when(jnp.logical_not(first))
    def _():
        _ffn_rows([slice(0, TM)], [[]], weights, o_ref, h_ref)

    _ffn_epilogue(fg_ref, o_ref, final_norm)
    for z_ref in rest[n_buf + 1:-1]:
        z_ref[...] = jnp.zeros_like(z_ref)


def _ffn_mix_kernel(*refs, final_norm, n_buf):
    x_hbm, g_ref, wg_ref, wu_ref, wd_ref, fg_ref = refs[:6]
    mix_in = refs[6:6 + N_MIX_IN]
    n_in = 6 + N_MIX_IN + n_buf
    o_ref = refs[n_in]
    mix_out = refs[n_in + 1:n_in + 1 + N_MIX_OUT]
    h_ref = refs[-N_MIX_SCRATCH - 1]
    mix_scratch = refs[-N_MIX_SCRATCH:]
    j = pl.program_id(1)
    active = j < MIX_STEPS
    s = pl.program_id(0) * MIX_STEPS + jnp.minimum(j, MIX_STEPS - 1)
    _ffn_load_x(x_hbm, o_ref)
    _mixer_resets(s, active, mix_scratch)
    weights = _ffn_weights(wg_ref, wu_ref, wd_ref)

    def mixer_step(norm):
        ysq = [None]

        def piece(c):
            ysq[0] = _mixer_piece(c, s, ysq[0], mix_in, mix_out, mix_scratch)

        _ffn_rows(MIX_CHUNKS, PIECES_AFTER, weights, o_ref, h_ref, norm=norm, piece=piece)

    pl.when(j == 0)(lambda: mixer_step((None, g_ref)))
    pl.when(jnp.logical_and(j > 0, active))(lambda: mixer_step(None))

    @pl.when(jnp.logical_not(active))
    def _():
        _ffn_rows([slice(0, TM)], [[]], weights, o_ref, h_ref)

    _ffn_epilogue(fg_ref, o_ref, final_norm)


def _ffn(x, g, wg, wu, wd, fg, layer, final_norm, mix=None, out_rows=None, out_tile0=0,
         out_buf=None, zero_rows=None):
    rows, d = x.shape
    out_rows = rows if out_rows is None else out_rows
    grid = (rows // TM, wg.shape[2] // FFN_TF)
    x_in_hbm = mix is not None or zero_rows is not None
    x_spec = pl.BlockSpec(memory_space=pl.ANY) if x_in_hbm else pl.BlockSpec((TM, d), lambda i, j: (i, 0))
    in_specs = [x_spec,
                pl.BlockSpec((None, 1, d), lambda i, j: (layer, 0, 0)),
                pl.BlockSpec((None, d, FFN_TF), lambda i, j: (layer, 0, j)),
                pl.BlockSpec((None, d, FFN_TF), lambda i, j: (layer, 0, j)),
                pl.BlockSpec((None, FFN_TF, d), lambda i, j: (layer, j, 0)),
                pl.BlockSpec((1, d), lambda i, j: (0, 0))]
    out_spec = pl.BlockSpec((TM, d), lambda i, j: (out_tile0 + i, 0))
    out_shape = jax.ShapeDtypeStruct((out_rows, d), F32)
    h_scratch = pltpu.VMEM((TM, d), BF16)
    operands = (x, g, wg, wu, wd, fg)
    buf_specs, bufs = ([], ()) if out_buf is None else ([pl.BlockSpec(memory_space=pl.ANY)], (out_buf,))
    if mix is None:
        out_specs, out_shapes = [out_spec], [out_shape]
        if zero_rows is not None:
            per_tile = ZERO_SLABS_PER_TILE
            slab = zero_rows // (grid[0] * per_tile)
            out_specs.append(pl.BlockSpec(
                (slab, d), lambda i, j: (i * per_tile + jnp.minimum(j, per_tile - 1), 0)))
            out_shapes.append(jax.ShapeDtypeStruct((zero_rows, d), F32))
        outs = pl.pallas_call(
            functools.partial(_ffn_kernel, final_norm=final_norm, n_buf=len(bufs), x_in_hbm=x_in_hbm),
            grid=grid, in_specs=in_specs + buf_specs, out_specs=out_specs, out_shape=out_shapes,
            scratch_shapes=[h_scratch],
            input_output_aliases={len(operands): 0} if bufs else {},
            compiler_params=_params("parallel", "arbitrary", vmem=VMEM_LIMIT_FFN),
            name="ffn_final" if final_norm else "ffn",
        )(*operands, *bufs)
        return outs[0] if zero_rows is None else outs
    p_other, mix_params = mix
    m_in, m_out, m_scratch = _mixer_specs(
        lambda i, j: i * MIX_STEPS + jnp.minimum(j, MIX_STEPS - 1), layer)
    operands = operands + _mixer_operands(p_other, mix_params)
    return pl.pallas_call(
        functools.partial(_ffn_mix_kernel, final_norm=final_norm, n_buf=len(bufs)),
        grid=grid, in_specs=in_specs + m_in + buf_specs, out_specs=[out_spec] + m_out,
        out_shape=[out_shape] + _mixer_out_shapes(p_other.shape[0]),
        scratch_shapes=[h_scratch] + m_scratch,
        input_output_aliases={len(operands): 0} if bufs else {},
        compiler_params=_params("arbitrary", "arbitrary"),
        name="ffn_mix_final" if final_norm else "ffn_mix",
    )(*operands, *bufs)


def _retention_tables():
    cs = RET_BLOCK
    pos = np.arange(SEQ, dtype=np.float32)
    inv = (1.0 / (ROPE_BASE ** (np.arange(0, HEAD_DIM, 2, dtype=np.float32) / HEAD_DIM))).astype(np.float32)
    ang = jnp.asarray(pos[:, None] * inv[None, :])
    cos, sin = jnp.cos(ang), jnp.sin(ang)
    cfull = jnp.concatenate([cos, cos], axis=-1)
    ssign = jnp.concatenate([-sin, sin], axis=-1)
    kscale = HEAD_DIM ** -0.5
    gamma_log = np.log1p(-np.exp2(-5.0 - np.arange(RET_HEADS, dtype=np.float64)))
    idx = np.arange(cs)
    dist = np.abs(idx[:, None] - idx[None, :])
    visible = (idx[None, :] // CHUNK) <= (idx[:, None] // CHUNK)
    dmask = kscale * np.where(visible[None], np.exp(gamma_log[:, None, None] * dist[None]), 0.0)
    qdec = np.exp(gamma_log[:, None] * (idx + 1.0)[None, :])
    kdec = kscale * np.exp(gamma_log[:, None] * (cs - 1.0 - idx)[None, :])
    bcast = lambda a: np.broadcast_to(a[:, :, None], (RET_HEADS, cs, HEAD_DIM))
    decay = np.concatenate([dmask, bcast(qdec), bcast(kdec)], axis=-1)
    return (jnp.concatenate([cfull, ssign], axis=-1), jnp.asarray(decay, F32),
            jnp.asarray(np.exp(gamma_log * cs), F32))


def _block_diag(w):
    per = GATE_BLOCK // LRU_GROUP_DIM
    nb = LRU_GROUPS // per
    w4 = w.reshape(w.shape[0], nb, per, LRU_GROUP_DIM, LRU_GROUP_DIM)
    bd = jnp.einsum('lcipq,ij->lcipjq', w4, jnp.eye(per, dtype=w.dtype))
    return bd.reshape(w.shape[0], nb, GATE_BLOCK, GATE_BLOCK).astype(BF16)


def kernel(x, norm1_g, w_in, ret_gn_g, lru_conv_w, lru_conv_b, lru_wa, lru_ba, lru_wx, lru_bx,
           lru_lambda, lru_norm_g, w_out, norm2_g, ffn_w_gate, ffn_w_up, ffn_w_down, final_g):
    batch, seq, d = x.shape
    depth = w_in.shape[0]
    assert (seq, d) == (SEQ, D_MODEL) and batch % 2 == 0
    rows = batch * seq
    half = rows // 2
    half_tiles = half // TM
    vecs = lambda a: a.reshape(depth, 1, -1).astype(F32)
    rot, decay, sdec = _retention_tables()
    head_rows = jnp.stack([ret_gn_g.reshape(depth, RET_HEADS, HEAD_DIM).astype(F32),
                           jnp.broadcast_to(sdec[None, :, None], (depth, RET_HEADS, HEAD_DIM))], axis=2)
    head = jnp.pad(head_rows, ((0, 0), (0, 0), (0, SUBLANES - 2), (0, 0)))
    vec_rows = jnp.concatenate([lru_conv_w.astype(F32), vecs(lru_conv_b), vecs(lru_ba), vecs(lru_bx),
                                vecs(lru_lambda), vecs(lru_norm_g)], axis=1)
    vec = jnp.pad(vec_rows, ((0, 0), (0, VEC_ROWS - vec_rows.shape[1]), (0, 0)))
    gates = jnp.concatenate([_block_diag(lru_wa), _block_diag(lru_wx)], axis=1)
    mix_params = (rot, decay, head, vec, gates)
    g1, g2, fg = vecs(norm1_g), vecs(norm2_g), final_g.reshape(1, d).astype(F32)

    assert depth >= 2
    xf = x.reshape(rows, d)
    xa, xb = (xf, 0), (xf, half_tiles)
    p_a = _in_proj(*xa, half, g1, w_in, 0)
    for l in range(depth):
        last = l == depth - 1
        ffn_w = (ffn_w_gate, ffn_w_up, ffn_w_down, fg, l, last)
        p_b, ret_a, lru_a = _in_proj(*xb, half, g1, w_in, l, mix=(p_a, mix_params, l))
        x1_a = _out_proj(*xa, ret_a, lru_a, w_out, l)
        if last:
            x2_a, ret_b, lru_b = _ffn(x1_a, g2, *ffn_w, mix=(p_b, mix_params), out_rows=rows,
                                      out_buf=out_buf)
            x1_b = _out_proj(*xb, ret_b, lru_b, w_out, l)
            return _ffn(x1_b, g2, *ffn_w, out_rows=rows, out_tile0=half_tiles,
                        out_buf=x2_a).reshape(batch, seq, d)
        x2_a = _ffn(x1_a, g2, *ffn_w)
        xa = (x2_a, 0)
        p_a, ret_b, lru_b = _in_proj(*xa, half, g1, w_in, l + 1, mix=(p_b, mix_params, l))
        x1_b = _out_proj(*xb, ret_b, lru_b, w_out, l)
        zero = dict(zero_rows=rows) if l == depth - 2 else {}
        x2_b = _ffn(x1_b, g2, *ffn_w, **zero)
        if zero:
            x2_b, out_buf = x2_b
        xb = (x2_b, 0)
```

```python
import functools

import numpy as np
import jax
import jax.numpy as jnp
from jax import lax
from jax.experimental import pallas as pl
from jax.experimental.pallas import tpu as pltpu

D_MODEL = 2048
SEQ = 2048
RET_HEADS = 8
HEAD_DIM = 128
RET_WIDTH = RET_HEADS * HEAD_DIM
LRU_WIDTH = 1024
LRU_GROUPS = 16
LRU_GROUP_DIM = 64
CONV_W = 4
LRU_C = 8.0
IN_WIDTH = 4 * RET_WIDTH + 2 * LRU_WIDTH
CHUNK = 64
ROPE_BASE = 10000.0
EPS = 1e-6

F32 = jnp.float32
BF16 = jnp.bfloat16

SUBLANES = 8
BF16_ROWS = 16
TM = 1024
IN_TN = 768
IN_TN_PLAIN = 1024
FFN_TF = 256
OUT_TN = 1024
MIX_STEPS = IN_WIDTH // IN_TN
LRU_BLOCK = TM // MIX_STEPS
RET_GROUPS = 2
RET_GROUP_HEADS = RET_HEADS // RET_GROUPS
RET_GROUP_WIDTH = RET_GROUP_HEADS * HEAD_DIM
RET_BLOCK = LRU_BLOCK * RET_GROUPS
GATE_BLOCK = 256
VMEM_LIMIT = 55 * 1024 * 1024
VMEM_LIMIT_FFN = 57 * 1024 * 1024
ZERO_SLABS_PER_TILE = 16


def _params(*sem, vmem=VMEM_LIMIT):
    return pltpu.CompilerParams(dimension_semantics=sem, vmem_limit_bytes=vmem)


def _sigmoid(z):
    return 0.5 * jnp.tanh(0.5 * z) + 0.5


def _rms_scale(x, g):
    ms = jnp.mean(x * x, axis=-1, keepdims=True)
    return x * lax.rsqrt(ms + EPS) * g


def _retention_head(h, hg, q_ref, k_ref, v_ref, g_ref, rot_ref, decay_ref, head_ref, o_ref, s_ref):
    cos, sin = rot_ref[:, :HEAD_DIM], rot_ref[:, HEAD_DIM:]
    half = HEAD_DIM // 2
    sl = slice(h * HEAD_DIM, (h + 1) * HEAD_DIM)
    q = q_ref[:, sl].astype(F32)
    k = k_ref[:, sl].astype(F32)
    v = v_ref[:, sl]
    qr = q * cos + pltpu.roll(q, half, 1) * sin
    kr = k * cos + pltpu.roll(k, half, 1) * sin
    qb = qr.astype(BF16)
    kb = kr.astype(BF16)
    sidx = hg * RET_GROUP_HEADS + h
    dmask = decay_ref[sidx, :, 0:RET_BLOCK]
    qdec = decay_ref[sidx, :, RET_BLOCK:RET_BLOCK + HEAD_DIM]
    kdec = decay_ref[sidx, :, RET_BLOCK + HEAD_DIM:RET_BLOCK + 2 * HEAD_DIM]
    gn, sdec = head_ref[sidx, 0:1, :], head_ref[sidx, 1:2, :]
    scores = lax.dot_general(qb, kb, (((1,), (1,)), ((), ())),
                             preferred_element_type=F32) * dmask
    o = jnp.dot(scores.astype(BF16), v, preferred_element_type=F32)
    state = s_ref[sidx]
    o = o + qdec * jnp.dot(qb, state.astype(BF16), preferred_element_type=F32)
    kd = (kr * kdec).astype(BF16)
    s_ref[sidx] = sdec * state + lax.dot_general(
        kd, v, (((0,), (0,)), ((), ())), preferred_element_type=F32)
    mu = jnp.mean(o, axis=-1, keepdims=True)
    oc = o - mu
    var = jnp.mean(oc * oc, axis=-1, keepdims=True)
    on = oc * lax.rsqrt(var + EPS) * gn
    gate = g_ref[:, sl].astype(F32)
    o_ref[:, sl] = (on * (gate * _sigmoid(gate))).astype(o_ref.dtype)


def _lru_cols(c, xy_ref, vec_ref, gates_ref, xpad_ref, a_ref, b_ref, y_ref, carry_ref):
    tt = xy_ref.shape[0]
    pad = SUBLANES
    sl = slice(c * GATE_BLOCK, (c + 1) * GATE_BLOCK)
    ysl = slice(LRU_WIDTH + c * GATE_BLOCK, LRU_WIDTH + (c + 1) * GATE_BLOCK)
    vrow = lambda r: vec_ref[r:r + 1, sl]
    x = xy_ref[:, sl].astype(F32)
    xpad_ref[pad:pad + tt, sl] = x
    xc = vrow(VEC_CONV_B) + vrow(VEC_CONV_W + CONV_W - 1) * x
    for j in range(CONV_W - 1):
        off = pad - (CONV_W - 1) + j
        xc = xc + vrow(VEC_CONV_W + j) * xpad_ref[off:off + tt, sl]
    xpad_ref[0:pad, sl] = x[tt - pad:tt, :]

    lam = vrow(VEC_LAMBDA)
    softplus_neg_lam = jnp.maximum(-lam, 0.0) + jnp.log1p(jnp.exp(-jnp.abs(lam)))
    xc16 = xc.astype(BF16)
    n_blocks = LRU_WIDTH // GATE_BLOCK
    r = _sigmoid(jnp.dot(xc16, gates_ref[c], preferred_element_type=F32) + vrow(VEC_BA))
    i = _sigmoid(jnp.dot(xc16, gates_ref[n_blocks + c], preferred_element_type=F32) + vrow(VEC_BX))
    neg_log_a = LRU_C * r * softplus_neg_lam
    a = jnp.exp(-neg_log_a)
    a_ref[:, sl] = a
    one_minus_a2 = jnp.tanh(neg_log_a) * (a * a + 1.0)
    root = jnp.where(one_minus_a2 > 0.0, one_minus_a2 * lax.rsqrt(one_minus_a2), 0.0)
    b_ref[:, sl] = root * (i * xc)

    row = lax.broadcasted_iota(jnp.int32, (SUBLANES, GATE_BLOCK), 0)
    carry = carry_ref[:, sl]
    for gidx in range(tt // SUBLANES):
        rows = slice(gidx * SUBLANES, (gidx + 1) * SUBLANES)
        a = a_ref[rows, sl]
        b = b_ref[rows, sl]
        for s in (1, 2, 4):
            keep = row >= s
            b = jnp.where(keep, a * pltpu.roll(b, s, 0) + b, b)
            a = jnp.where(keep, a * pltpu.roll(a, s, 0), a)
        h = a * carry + b
        b_ref[rows, sl] = h
        carry = h[SUBLANES - 1:SUBLANES, :]
    carry_ref[:, sl] = carry

    y = b_ref[:, sl] * jax.nn.gelu(xy_ref[:, ysl].astype(F32))
    y_ref[:, sl] = y
    return y * y


def _lru_finish(ysq, vec_ref, o_ref, y_ref):
    ms = jnp.sum(ysq, axis=-1, keepdims=True) * (1.0 / LRU_WIDTH)
    ng = vec_ref[VEC_NORM_G:VEC_NORM_G + 1, :]
    o_ref[...] = (y_ref[...] * lax.rsqrt(ms + EPS) * ng).astype(o_ref.dtype)


VEC_CONV_W, VEC_CONV_B, VEC_BA, VEC_BX, VEC_LAMBDA, VEC_NORM_G = 0, CONV_W, CONV_W + 1, CONV_W + 2, CONV_W + 3, CONV_W + 4
VEC_ROWS = 16
N_MIX_IN = 10
N_MIX_OUT = 2
N_MIX_SCRATCH = 6
MIX_PIECES = LRU_WIDTH // GATE_BLOCK
assert MIX_PIECES == RET_GROUP_HEADS


def _row_chunks(n):
    step = TM // n // (2 * BF16_ROWS) * (2 * BF16_ROWS)
    return [slice(c * step, (c + 1) * step if c < n - 1 else TM) for c in range(n)]


MIX_CHUNKS = _row_chunks(MIX_PIECES + 1)
PIECES_AFTER = [[c] for c in range(MIX_PIECES)] + [[]]
NORM_CHUNKS = _row_chunks(4)
NO_PIECES = [[]] * len(NORM_CHUNKS)


def _mixer_resets(s, active, mix_scratch):
    s_ref, xpad_ref, _, _, _, carry_ref = mix_scratch
    lru_blocks = SEQ // LRU_BLOCK
    hg = s % RET_GROUPS

    @pl.when(jnp.logical_and(active, s % lru_blocks == 0))
    def _():
        xpad_ref[0:SUBLANES, :] = jnp.zeros((SUBLANES, LRU_WIDTH), F32)
        carry_ref[...] = jnp.zeros_like(carry_ref)

    @pl.when(jnp.logical_and(active, (s // RET_GROUPS) % (SEQ // RET_BLOCK) == 0))
    def _():
        s_ref[pl.ds(hg * RET_GROUP_HEADS, RET_GROUP_HEADS)] = jnp.zeros(
            (RET_GROUP_HEADS, HEAD_DIM, HEAD_DIM), F32)


def _mixer_piece(c, s, ysq, mix_in, mix_out, mix_scratch):
    q, k, v, g, xy, rot, decay, head, vec, gates = mix_in
    o_ret, o_lru = mix_out
    s_ref, xpad_ref, a_ref, b_ref, y_ref, carry_ref = mix_scratch
    _retention_head(c, s % RET_GROUPS, q, k, v, g, rot, decay, head, o_ret, s_ref)
    ysq_c = _lru_cols(c, xy, vec, gates, xpad_ref, a_ref, b_ref, y_ref, carry_ref)
    ysq = ysq_c if ysq is None else ysq + ysq_c
    if c == MIX_PIECES - 1:
        _lru_finish(ysq, vec, o_lru, y_ref)
    return ysq


def _mixer_specs(smap, layer):
    hg = lambda i, j: smap(i, j) % RET_GROUPS
    rblk = lambda i, j: smap(i, j) // RET_GROUPS
    tblk = lambda i, j: rblk(i, j) % (SEQ // RET_BLOCK)
    pcol = lambda c: pl.BlockSpec((RET_BLOCK, RET_GROUP_WIDTH),
                                  lambda i, j, c=c: (rblk(i, j), c * RET_GROUPS + hg(i, j)))
    resident = dict(pipeline_mode=pl.Buffered(1))
    in_specs = [pcol(0), pcol(1), pcol(2), pcol(3),
                pl.BlockSpec((LRU_BLOCK, 2 * LRU_WIDTH),
                             lambda i, j: (smap(i, j), 4 * RET_WIDTH // (2 * LRU_WIDTH))),
                pl.BlockSpec((RET_BLOCK, 2 * HEAD_DIM), lambda i, j: (tblk(i, j), 0)),
                pl.BlockSpec((RET_HEADS, RET_BLOCK, RET_BLOCK + 2 * HEAD_DIM),
                             lambda i, j: (0, 0, 0), **resident),
                pl.BlockSpec((None, RET_HEADS, SUBLANES, HEAD_DIM),
                             lambda i, j: (layer, 0, 0, 0), **resident),
                pl.BlockSpec((None, VEC_ROWS, LRU_WIDTH), lambda i, j: (layer, 0, 0), **resident),
                pl.BlockSpec((None, 2 * LRU_WIDTH // GATE_BLOCK, GATE_BLOCK, GATE_BLOCK),
                             lambda i, j: (layer, 0, 0, 0), **resident)]
    out_specs = [pl.BlockSpec((RET_BLOCK, RET_GROUP_WIDTH), lambda i, j: (rblk(i, j), hg(i, j))),
                 pl.BlockSpec((LRU_BLOCK, LRU_WIDTH), lambda i, j: (smap(i, j), 0))]
    scratch = [pltpu.VMEM((RET_HEADS, HEAD_DIM, HEAD_DIM), F32),
               pltpu.VMEM((LRU_BLOCK + SUBLANES, LRU_WIDTH), F32),
               pltpu.VMEM((LRU_BLOCK, LRU_WIDTH), F32),
               pltpu.VMEM((LRU_BLOCK, LRU_WIDTH), F32),
               pltpu.VMEM((LRU_BLOCK, LRU_WIDTH), F32),
               pltpu.VMEM((1, LRU_WIDTH), F32)]
    return in_specs, out_specs, scratch


def _mixer_out_shapes(rows):
    return [jax.ShapeDtypeStruct((rows, RET_WIDTH), BF16),
            jax.ShapeDtypeStruct((rows, LRU_WIDTH), BF16)]


def _mixer_operands(p, mix):
    return (p,) * 5 + tuple(mix)


def _in_proj_chunk(rows, wb, o_ref, h_ref):
    o_ref[rows, :] = jnp.dot(h_ref[rows, :], wb, preferred_element_type=F32).astype(o_ref.dtype)


def _in_proj_rows(chunks, after, w_ref, o_ref, h_ref, norm=None, piece=None):
    wb = w_ref[...].astype(BF16)
    for rows, pieces in zip(chunks, after):
        if norm is not None:
            x_ref, g_ref = norm
            h_ref[rows, :] = _rms_scale(x_ref[rows, :], g_ref[...]).astype(BF16)
        _in_proj_chunk(rows, wb, o_ref, h_ref)
        for c in pieces:
            piece(c)


def _in_proj_kernel(x_ref, g_ref, w_ref, o_ref, h_ref):
    first = pl.program_id(1) == 0

    @pl.when(first)
    def _():
        _in_proj_rows(NORM_CHUNKS, NO_PIECES, w_ref, o_ref, h_ref, norm=(x_ref, g_ref))

    @pl.when(jnp.logical_not(first))
    def _():
        _in_proj_rows([slice(0, TM)], [[]], w_ref, o_ref, h_ref)


def _in_proj_mix_kernel(*refs):
    x_ref, g_ref, w_ref = refs[:3]
    mix_in = refs[3:3 + N_MIX_IN]
    o_ref = refs[3 + N_MIX_IN]
    mix_out = refs[4 + N_MIX_IN:4 + N_MIX_IN + N_MIX_OUT]
    h_ref = refs[-N_MIX_SCRATCH - 1]
    mix_scratch = refs[-N_MIX_SCRATCH:]
    s = pl.program_id(0) * MIX_STEPS + pl.program_id(1)
    _mixer_resets(s, True, mix_scratch)
    first = pl.program_id(1) == 0

    def step(norm):
        ysq = [None]

        def piece(c):
            ysq[0] = _mixer_piece(c, s, ysq[0], mix_in, mix_out, mix_scratch)

        _in_proj_rows(MIX_CHUNKS, PIECES_AFTER, w_ref, o_ref, h_ref, norm=norm, piece=piece)

    pl.when(first)(lambda: step((x_ref, g_ref)))
    pl.when(jnp.logical_not(first))(lambda: step(None))


def _in_proj_specs(x_tile0, layer, tn):
    d = D_MODEL
    in_specs = [pl.BlockSpec((TM, d), lambda i, j: (x_tile0 + i, 0)),
                pl.BlockSpec((None, 1, d), lambda i, j: (layer, 0, 0)),
                pl.BlockSpec((None, d, tn), lambda i, j: (layer, 0, j))]
    out_spec = pl.BlockSpec((TM, tn), lambda i, j: (i, j))
    return in_specs, out_spec


def _in_proj(x, x_tile0, rows, g, w, layer, mix=None):
    tn = IN_TN if mix is not None else IN_TN_PLAIN
    in_specs, out_spec = _in_proj_specs(x_tile0, layer, tn)
    grid = (rows // TM, IN_WIDTH // tn)
    p_shape = jax.ShapeDtypeStruct((rows, IN_WIDTH), BF16)
    h_scratch = pltpu.VMEM((TM, D_MODEL), BF16)
    if mix is None:
        return pl.pallas_call(
            _in_proj_kernel, grid=grid, in_specs=in_specs, out_specs=out_spec,
            out_shape=p_shape, scratch_shapes=[h_scratch],
            compiler_params=_params("parallel", "arbitrary"), name="in_proj",
        )(x, g, w)
    p_other, mix_params, mix_layer = mix
    m_in, m_out, m_scratch = _mixer_specs(lambda i, j: i * MIX_STEPS + j, mix_layer)
    return pl.pallas_call(
        _in_proj_mix_kernel, grid=grid,
        in_specs=in_specs + m_in, out_specs=[out_spec] + m_out,
        out_shape=[p_shape] + _mixer_out_shapes(p_other.shape[0]),
        scratch_shapes=[h_scratch] + m_scratch,
        compiler_params=_params("arbitrary", "arbitrary"), name="in_proj_mix",
    )(x, g, w, *_mixer_operands(p_other, mix_params))


def _out_proj_kernel(x_ref, r_ref, l_ref, w_ref, o_ref):
    acc = jnp.dot(r_ref[...], w_ref[0:RET_WIDTH, :].astype(BF16), preferred_element_type=F32)
    acc = acc + jnp.dot(l_ref[...], w_ref[RET_WIDTH:, :].astype(BF16), preferred_element_type=F32)
    o_ref[...] = x_ref[...] + acc


def _out_proj(x, x_tile0, o_ret, o_lru, w, layer):
    rows = o_ret.shape[0]
    d = D_MODEL
    return pl.pallas_call(
        _out_proj_kernel,
        grid=(d // OUT_TN, rows // TM),
        in_specs=[pl.BlockSpec((TM, OUT_TN), lambda j, i: (x_tile0 + i, j)),
                  pl.BlockSpec((TM, RET_WIDTH), lambda j, i: (i, 0)),
                  pl.BlockSpec((TM, LRU_WIDTH), lambda j, i: (i, 0)),
                  pl.BlockSpec((None, d, OUT_TN), lambda j, i: (layer, 0, j))],
        out_specs=pl.BlockSpec((TM, OUT_TN), lambda j, i: (i, j)),
        out_shape=jax.ShapeDtypeStruct((rows, d), F32),
        compiler_params=_params("parallel", "arbitrary"),
        name="out_proj",
    )(x, o_ret, o_lru, w)


def _ffn_load_x(x_hbm, o_ref):
    @pl.when(pl.program_id(1) == 0)
    def _():
        row0 = pl.multiple_of(pl.program_id(0) * TM, TM)
        pltpu.sync_copy(x_hbm.at[pl.ds(row0, TM), :], o_ref)


def _ffn_gate_up(rows, wg, wu, h_ref):
    h = h_ref[rows, :]
    return (jnp.dot(h, wg, preferred_element_type=F32), jnp.dot(h, wu, preferred_element_type=F32))


def _ffn_down(rows, gate_up, wd, o_ref):
    gate, up = gate_up
    mid = (gate * _sigmoid(gate) * up).astype(BF16)
    o_ref[rows, :] += jnp.dot(mid, wd, preferred_element_type=F32)


def _ffn_rows(chunks, after, w_refs, o_ref, h_ref, norm=None, piece=None):
    wg, wu, wd = _ffn_weights(*w_refs)
    pending = None
    for rows, pieces in zip(chunks, after):
        if norm is not None:
            x_ref, g_ref = norm
            if x_ref is not None:
                o_ref[rows, :] = x_ref[rows, :]
            h_ref[rows, :] = _rms_scale(o_ref[rows, :], g_ref[...]).astype(BF16)
        gate_up = _ffn_gate_up(rows, wg, wu, h_ref)
        if pending is not None:
            _ffn_down(*pending, wd, o_ref)
        pending = (rows, gate_up)
        for c in pieces:
            piece(c)
    _ffn_down(*pending, wd, o_ref)


def _ffn_weights(wg_ref, wu_ref, wd_ref):
    return wg_ref[...].astype(BF16), wu_ref[...].astype(BF16), wd_ref[...].astype(BF16)


def _ffn_epilogue(fg_ref, o_ref, final_norm):
    if final_norm:
        @pl.when(pl.program_id(1) == pl.num_programs(1) - 1)
        def _():
            o_ref[...] = _rms_scale(o_ref[...], fg_ref[...])


def _ffn_kernel(x_ref, g_ref, wg_ref, wu_ref, wd_ref, fg_ref, *rest, final_norm, n_buf, x_in_hbm):
    o_ref, h_ref = rest[n_buf], rest[-1]
    if x_in_hbm:
        _ffn_load_x(x_ref, o_ref)
    weights = (wg_ref, wu_ref, wd_ref)
    first = pl.program_id(1) == 0

    @pl.when(first)
    def _():
        _ffn_rows(NORM_CHUNKS, NO_PIECES, weights, o_ref, h_ref,
                  norm=(None if x_in_hbm else x_ref, g_ref))

    @pl.when(jnp.logical_not(first))
    def _():
        _ffn_rows([slice(0, TM)], [[]], weights, o_ref, h_ref)

    _ffn_epilogue(fg_ref, o_ref, final_norm)
    for z_ref in rest[n_buf + 1:-1]:
        z_ref[...] = jnp.zeros_like(z_ref)


def _ffn_mix_kernel(*refs, final_norm, n_buf):
    x_hbm, g_ref, wg_ref, wu_ref, wd_ref, fg_ref = refs[:6]
    mix_in = refs[6:6 + N_MIX_IN]
    n_in = 6 + N_MIX_IN + n_buf
    o_ref = refs[n_in]
    mix_out = refs[n_in + 1:n_in + 1 + N_MIX_OUT]
    h_ref = refs[-N_MIX_SCRATCH - 1]
    mix_scratch = refs[-N_MIX_SCRATCH:]
    j = pl.program_id(1)
    active = j < MIX_STEPS
    s = pl.program_id(0) * MIX_STEPS + jnp.minimum(j, MIX_STEPS - 1)
    _ffn_load_x(x_hbm, o_ref)
    _mixer_resets(s, active, mix_scratch)
    weights = (wg_ref, wu_ref, wd_ref)

    def mixer_step(norm):
        ysq = [None]

        def piece(c):
            ysq[0] = _mixer_piece(c, s, ysq[0], mix_in, mix_out, mix_scratch)

        _ffn_rows(MIX_CHUNKS, PIECES_AFTER, weights, o_ref, h_ref, norm=norm, piece=piece)

    pl.when(j == 0)(lambda: mixer_step((None, g_ref)))
    pl.when(jnp.logical_and(j > 0, active))(lambda: mixer_step(None))

    @pl.when(jnp.logical_not(active))
    def _():
        _ffn_rows([slice(0, TM)], [[]], weights, o_ref, h_ref)

    _ffn_epilogue(fg_ref, o_ref, final_norm)


def _ffn(x, g, wg, wu, wd, fg, layer, final_norm, mix=None, out_rows=None, out_tile0=0,
         out_buf=None, zero_rows=None):
    rows, d = x.shape
    out_rows = rows if out_rows is None else out_rows
    grid = (rows // TM, wg.shape[2] // FFN_TF)
    x_in_hbm = mix is not None or zero_rows is not None
    x_spec = pl.BlockSpec(memory_space=pl.ANY) if x_in_hbm else pl.BlockSpec((TM, d), lambda i, j: (i, 0))
    in_specs = [x_spec,
                pl.BlockSpec((None, 1, d), lambda i, j: (layer, 0, 0)),
                pl.BlockSpec((None, d, FFN_TF), lambda i, j: (layer, 0, j)),
                pl.BlockSpec((None, d, FFN_TF), lambda i, j: (layer, 0, j)),
                pl.BlockSpec((None, FFN_TF, d), lambda i, j: (layer, j, 0)),
                pl.BlockSpec((1, d), lambda i, j: (0, 0))]
    out_spec = pl.BlockSpec((TM, d), lambda i, j: (out_tile0 + i, 0))
    out_shape = jax.ShapeDtypeStruct((out_rows, d), F32)
    h_scratch = pltpu.VMEM((TM, d), BF16)
    operands = (x, g, wg, wu, wd, fg)
    buf_specs, bufs = ([], ()) if out_buf is None else ([pl.BlockSpec(memory_space=pl.ANY)], (out_buf,))
    if mix is None:
        out_specs, out_shapes = [out_spec], [out_shape]
        if zero_rows is not None:
            per_tile = ZERO_SLABS_PER_TILE
            slab = zero_rows // (grid[0] * per_tile)
            out_specs.append(pl.BlockSpec(
                (slab, d), lambda i, j: (i * per_tile + jnp.minimum(j, per_tile - 1), 0)))
            out_shapes.append(jax.ShapeDtypeStruct((zero_rows, d), F32))
        outs = pl.pallas_call(
            functools.partial(_ffn_kernel, final_norm=final_norm, n_buf=len(bufs), x_in_hbm=x_in_hbm),
            grid=grid, in_specs=in_specs + buf_specs, out_specs=out_specs, out_shape=out_shapes,
            scratch_shapes=[h_scratch],
            input_output_aliases={len(operands): 0} if bufs else {},
            compiler_params=_params("parallel", "arbitrary", vmem=VMEM_LIMIT_FFN),
            name="ffn_final" if final_norm else "ffn",
        )(*operands, *bufs)
        return outs[0] if zero_rows is None else outs
    p_other, mix_params = mix
    m_in, m_out, m_scratch = _mixer_specs(
        lambda i, j: i * MIX_STEPS + jnp.minimum(j, MIX_STEPS - 1), layer)
    operands = operands + _mixer_operands(p_other, mix_params)
    return pl.pallas_call(
        functools.partial(_ffn_mix_kernel, final_norm=final_norm, n_buf=len(bufs)),
        grid=grid, in_specs=in_specs + m_in + buf_specs, out_specs=[out_spec] + m_out,
        out_shape=[out_shape] + _mixer_out_shapes(p_other.shape[0]),
        scratch_shapes=[h_scratch] + m_scratch,
        input_output_aliases={len(operands): 0} if bufs else {},
        compiler_params=_params("arbitrary", "arbitrary"),
        name="ffn_mix_final" if final_norm else "ffn_mix",
    )(*operands, *bufs)


def _retention_tables():
    cs = RET_BLOCK
    pos = np.arange(SEQ, dtype=np.float32)
    inv = (1.0 / (ROPE_BASE ** (np.arange(0, HEAD_DIM, 2, dtype=np.float32) / HEAD_DIM))).astype(np.float32)
    ang = jnp.asarray(pos[:, None] * inv[None, :])
    cos, sin = jnp.cos(ang), jnp.sin(ang)
    cfull = jnp.concatenate([cos, cos], axis=-1)
    ssign = jnp.concatenate([-sin, sin], axis=-1)
    kscale = HEAD_DIM ** -0.5
    gamma_log = np.log1p(-np.exp2(-5.0 - np.arange(RET_HEADS, dtype=np.float64)))
    idx = np.arange(cs)
    dist = np.abs(idx[:, None] - idx[None, :])
    visible = (idx[None, :] // CHUNK) <= (idx[:, None] // CHUNK)
    dmask = kscale * np.where(visible[None], np.exp(gamma_log[:, None, None] * dist[None]), 0.0)
    qdec = np.exp(gamma_log[:, None] * (idx + 1.0)[None, :])
    kdec = kscale * np.exp(gamma_log[:, None] * (cs - 1.0 - idx)[None, :])
    bcast = lambda a: np.broadcast_to(a[:, :, None], (RET_HEADS, cs, HEAD_DIM))
    decay = np.concatenate([dmask, bcast(qdec), bcast(kdec)], axis=-1)
    return (jnp.concatenate([cfull, ssign], axis=-1), jnp.asarray(decay, F32),
            jnp.asarray(np.exp(gamma_log * cs), F32))


def _block_diag(w):
    per = GATE_BLOCK // LRU_GROUP_DIM
    nb = LRU_GROUPS // per
    w4 = w.reshape(w.shape[0], nb, per, LRU_GROUP_DIM, LRU_GROUP_DIM)
    bd = jnp.einsum('lcipq,ij->lcipjq', w4, jnp.eye(per, dtype=w.dtype))
    return bd.reshape(w.shape[0], nb, GATE_BLOCK, GATE_BLOCK).astype(BF16)


def kernel(x, norm1_g, w_in, ret_gn_g, lru_conv_w, lru_conv_b, lru_wa, lru_ba, lru_wx, lru_bx,
           lru_lambda, lru_norm_g, w_out, norm2_g, ffn_w_gate, ffn_w_up, ffn_w_down, final_g):
    batch, seq, d = x.shape
    depth = w_in.shape[0]
    assert (seq, d) == (SEQ, D_MODEL) and batch % 2 == 0
    rows = batch * seq
    half = rows // 2
    half_tiles = half // TM
    vecs = lambda a: a.reshape(depth, 1, -1).astype(F32)
    rot, decay, sdec = _retention_tables()
    head_rows = jnp.stack([ret_gn_g.reshape(depth, RET_HEADS, HEAD_DIM).astype(F32),
                           jnp.broadcast_to(sdec[None, :, None], (depth, RET_HEADS, HEAD_DIM))], axis=2)
    head = jnp.pad(head_rows, ((0, 0), (0, 0), (0, SUBLANES - 2), (0, 0)))
    vec_rows = jnp.concatenate([lru_conv_w.astype(F32), vecs(lru_conv_b), vecs(lru_ba), vecs(lru_bx),
                                vecs(lru_lambda), vecs(lru_norm_g)], axis=1)
    vec = jnp.pad(vec_rows, ((0, 0), (0, VEC_ROWS - vec_rows.shape[1]), (0, 0)))
    gates = jnp.concatenate([_block_diag(lru_wa), _block_diag(lru_wx)], axis=1)
    mix_params = (rot, decay, head, vec, gates)
    g1, g2, fg = vecs(norm1_g), vecs(norm2_g), final_g.reshape(1, d).astype(F32)

    assert depth >= 2
    xf = x.reshape(rows, d)
    xa, xb = (xf, 0), (xf, half_tiles)
    p_a = _in_proj(*xa, half, g1, w_in, 0)
    for l in range(depth):
        last = l == depth - 1
        ffn_w = (ffn_w_gate, ffn_w_up, ffn_w_down, fg, l, last)
        p_b, ret_a, lru_a = _in_proj(*xb, half, g1, w_in, l, mix=(p_a, mix_params, l))
        x1_a = _out_proj(*xa, ret_a, lru_a, w_out, l)
        if last:
            x2_a, ret_b, lru_b = _ffn(x1_a, g2, *ffn_w, mix=(p_b, mix_params), out_rows=rows,
                                      out_buf=out_buf)
            x1_b = _out_proj(*xb, ret_b, lru_b, w_out, l)
            return _ffn(x1_b, g2, *ffn_w, out_rows=rows, out_tile0=half_tiles,
                        out_buf=x2_a).reshape(batch, seq, d)
        x2_a = _ffn(x1_a, g2, *ffn_w)
        xa = (x2_a, 0)
        p_a, ret_b, lru_b = _in_proj(*xa, half, g1, w_in, l + 1, mix=(p_b, mix_params, l))
        x1_b = _out_proj(*xb, ret_b, lru_b, w_out, l)
        zero = dict(zero_rows=rows) if l == depth - 2 else {}
        x2_b = _ffn(x1_b, g2, *ffn_w, **zero)
        if zero:
            x2_b, out_buf = x2_b
        xb = (x2_b, 0)
```

```python
import functools

import numpy as np
import jax
import jax.numpy as jnp
from jax import lax
from jax.experimental import pallas as pl
from jax.experimental.pallas import tpu as pltpu

D_MODEL = 2048
SEQ = 2048
RET_HEADS = 8
HEAD_DIM = 128
RET_WIDTH = RET_HEADS * HEAD_DIM
LRU_WIDTH = 1024
LRU_GROUPS = 16
LRU_GROUP_DIM = 64
CONV_W = 4
LRU_C = 8.0
IN_WIDTH = 4 * RET_WIDTH + 2 * LRU_WIDTH
CHUNK = 64
ROPE_BASE = 10000.0
EPS = 1e-6

F32 = jnp.float32
BF16 = jnp.bfloat16

SUBLANES = 8
BF16_ROWS = 16
TM = 1024
IN_TN = 768
IN_TN_PLAIN = 1024
FFN_TF = 256
OUT_TN = 1024
MIX_STEPS = IN_WIDTH // IN_TN
LRU_BLOCK = TM // MIX_STEPS
RET_GROUPS = 2
RET_GROUP_HEADS = RET_HEADS // RET_GROUPS
RET_GROUP_WIDTH = RET_GROUP_HEADS * HEAD_DIM
RET_BLOCK = LRU_BLOCK * RET_GROUPS
GATE_BLOCK = 256
VMEM_LIMIT = 55 * 1024 * 1024
VMEM_LIMIT_FFN = 57 * 1024 * 1024
ZERO_SLABS_PER_TILE = 4


def _params(*sem, vmem=VMEM_LIMIT):
    return pltpu.CompilerParams(dimension_semantics=sem, vmem_limit_bytes=vmem)


def _sigmoid(z):
    return 0.5 * jnp.tanh(0.5 * z) + 0.5


def _rms_scale(x, g):
    ms = jnp.mean(x * x, axis=-1, keepdims=True)
    return x * lax.rsqrt(ms + EPS) * g


def _retention_head(h, hg, q_ref, k_ref, v_ref, g_ref, rot_ref, decay_ref, head_ref, o_ref, s_ref):
    cos, sin = rot_ref[:, :HEAD_DIM], rot_ref[:, HEAD_DIM:]
    half = HEAD_DIM // 2
    sl = slice(h * HEAD_DIM, (h + 1) * HEAD_DIM)
    q = q_ref[:, sl].astype(F32)
    k = k_ref[:, sl].astype(F32)
    v = v_ref[:, sl]
    qr = q * cos + pltpu.roll(q, half, 1) * sin
    kr = k * cos + pltpu.roll(k, half, 1) * sin
    qb = qr.astype(BF16)
    kb = kr.astype(BF16)
    sidx = hg * RET_GROUP_HEADS + h
    dmask = decay_ref[sidx, :, 0:RET_BLOCK]
    qdec = decay_ref[sidx, :, RET_BLOCK:RET_BLOCK + HEAD_DIM]
    kdec = decay_ref[sidx, :, RET_BLOCK + HEAD_DIM:RET_BLOCK + 2 * HEAD_DIM]
    gn, sdec = head_ref[sidx, 0:1, :], head_ref[sidx, 1:2, :]
    scores = lax.dot_general(qb, kb, (((1,), (1,)), ((), ())),
                             preferred_element_type=F32) * dmask
    o = jnp.dot(scores.astype(BF16), v, preferred_element_type=F32)
    state = s_ref[sidx]
    o = o + qdec * jnp.dot(qb, state.astype(BF16), preferred_element_type=F32)
    kd = (kr * kdec).astype(BF16)
    s_ref[sidx] = sdec * state + lax.dot_general(
        kd, v, (((0,), (0,)), ((), ())), preferred_element_type=F32)
    mu = jnp.mean(o, axis=-1, keepdims=True)
    oc = o - mu
    var = jnp.mean(oc * oc, axis=-1, keepdims=True)
    on = oc * lax.rsqrt(var + EPS) * gn
    gate = g_ref[:, sl].astype(F32)
    o_ref[:, sl] = (on * (gate * _sigmoid(gate))).astype(o_ref.dtype)


def _lru_cols(c, xy_ref, vec_ref, gates_ref, xpad_ref, a_ref, b_ref, y_ref, carry_ref):
    tt = xy_ref.shape[0]
    pad = SUBLANES
    sl = slice(c * GATE_BLOCK, (c + 1) * GATE_BLOCK)
    ysl = slice(LRU_WIDTH + c * GATE_BLOCK, LRU_WIDTH + (c + 1) * GATE_BLOCK)
    vrow = lambda r: vec_ref[r:r + 1, sl]
    x = xy_ref[:, sl].astype(F32)
    xpad_ref[pad:pad + tt, sl] = x
    xc = vrow(VEC_CONV_B) + vrow(VEC_CONV_W + CONV_W - 1) * x
    for j in range(CONV_W - 1):
        off = pad - (CONV_W - 1) + j
        xc = xc + vrow(VEC_CONV_W + j) * xpad_ref[off:off + tt, sl]
    xpad_ref[0:pad, sl] = x[tt - pad:tt, :]

    lam = vrow(VEC_LAMBDA)
    softplus_neg_lam = jnp.maximum(-lam, 0.0) + jnp.log1p(jnp.exp(-jnp.abs(lam)))
    xc16 = xc.astype(BF16)
    n_blocks = LRU_WIDTH // GATE_BLOCK
    r = _sigmoid(jnp.dot(xc16, gates_ref[c], preferred_element_type=F32) + vrow(VEC_BA))
    i = _sigmoid(jnp.dot(xc16, gates_ref[n_blocks + c], preferred_element_type=F32) + vrow(VEC_BX))
    neg_log_a = LRU_C * r * softplus_neg_lam
    a = jnp.exp(-neg_log_a)
    a_ref[:, sl] = a
    one_minus_a2 = jnp.tanh(neg_log_a) * (a * a + 1.0)
    root = jnp.where(one_minus_a2 > 0.0, one_minus_a2 * lax.rsqrt(one_minus_a2), 0.0)
    b_ref[:, sl] = root * (i * xc)

    row = lax.broadcasted_iota(jnp.int32, (SUBLANES, GATE_BLOCK), 0)
    carry = carry_ref[:, sl]
    for gidx in range(tt // SUBLANES):
        rows = slice(gidx * SUBLANES, (gidx + 1) * SUBLANES)
        a = a_ref[rows, sl]
        b = b_ref[rows, sl]
        for s in (1, 2, 4):
            keep = row >= s
            b = jnp.where(keep, a * pltpu.roll(b, s, 0) + b, b)
            a = jnp.where(keep, a * pltpu.roll(a, s, 0), a)
        h = a * carry + b
        b_ref[rows, sl] = h
        carry = h[SUBLANES - 1:SUBLANES, :]
    carry_ref[:, sl] = carry

    y = b_ref[:, sl] * jax.nn.gelu(xy_ref[:, ysl].astype(F32))
    y_ref[:, sl] = y
    return y * y


def _lru_finish(ysq, vec_ref, o_ref, y_ref):
    ms = jnp.sum(ysq, axis=-1, keepdims=True) * (1.0 / LRU_WIDTH)
    ng = vec_ref[VEC_NORM_G:VEC_NORM_G + 1, :]
    o_ref[...] = (y_ref[...] * lax.rsqrt(ms + EPS) * ng).astype(o_ref.dtype)


VEC_CONV_W, VEC_CONV_B, VEC_BA, VEC_BX, VEC_LAMBDA, VEC_NORM_G = 0, CONV_W, CONV_W + 1, CONV_W + 2, CONV_W + 3, CONV_W + 4
VEC_ROWS = 16
N_MIX_IN = 10
N_MIX_OUT = 2
N_MIX_SCRATCH = 6
MIX_PIECES = LRU_WIDTH // GATE_BLOCK
assert MIX_PIECES == RET_GROUP_HEADS


def _row_chunks(n):
    step = TM // n // (2 * BF16_ROWS) * (2 * BF16_ROWS)
    return [slice(c * step, (c + 1) * step if c < n - 1 else TM) for c in range(n)]


MIX_CHUNKS = _row_chunks(MIX_PIECES + 1)
PIECES_AFTER = [[c] for c in range(MIX_PIECES)] + [[]]
NORM_CHUNKS = _row_chunks(4)
NO_PIECES = [[]] * len(NORM_CHUNKS)


def _mixer_resets(s, active, mix_scratch):
    s_ref, xpad_ref, _, _, _, carry_ref = mix_scratch
    lru_blocks = SEQ // LRU_BLOCK
    hg = s % RET_GROUPS

    @pl.when(jnp.logical_and(active, s % lru_blocks == 0))
    def _():
        xpad_ref[0:SUBLANES, :] = jnp.zeros((SUBLANES, LRU_WIDTH), F32)
        carry_ref[...] = jnp.zeros_like(carry_ref)

    @pl.when(jnp.logical_and(active, (s // RET_GROUPS) % (SEQ // RET_BLOCK) == 0))
    def _():
        s_ref[pl.ds(hg * RET_GROUP_HEADS, RET_GROUP_HEADS)] = jnp.zeros(
            (RET_GROUP_HEADS, HEAD_DIM, HEAD_DIM), F32)


def _mixer_piece(c, s, ysq, mix_in, mix_out, mix_scratch):
    q, k, v, g, xy, rot, decay, head, vec, gates = mix_in
    o_ret, o_lru = mix_out
    s_ref, xpad_ref, a_ref, b_ref, y_ref, carry_ref = mix_scratch
    _retention_head(c, s % RET_GROUPS, q, k, v, g, rot, decay, head, o_ret, s_ref)
    ysq_c = _lru_cols(c, xy, vec, gates, xpad_ref, a_ref, b_ref, y_ref, carry_ref)
    ysq = ysq_c if ysq is None else ysq + ysq_c
    if c == MIX_PIECES - 1:
        _lru_finish(ysq, vec, o_lru, y_ref)
    return ysq


def _mixer_specs(smap, layer):
    hg = lambda i, j: smap(i, j) % RET_GROUPS
    rblk = lambda i, j: smap(i, j) // RET_GROUPS
    tblk = lambda i, j: rblk(i, j) % (SEQ // RET_BLOCK)
    pcol = lambda c: pl.BlockSpec((RET_BLOCK, RET_GROUP_WIDTH),
                                  lambda i, j, c=c: (rblk(i, j), c * RET_GROUPS + hg(i, j)))
    resident = dict(pipeline_mode=pl.Buffered(1))
    in_specs = [pcol(0), pcol(1), pcol(2), pcol(3),
                pl.BlockSpec((LRU_BLOCK, 2 * LRU_WIDTH),
                             lambda i, j: (smap(i, j), 4 * RET_WIDTH // (2 * LRU_WIDTH))),
                pl.BlockSpec((RET_BLOCK, 2 * HEAD_DIM), lambda i, j: (tblk(i, j), 0)),
                pl.BlockSpec((RET_HEADS, RET_BLOCK, RET_BLOCK + 2 * HEAD_DIM),
                             lambda i, j: (0, 0, 0), **resident),
                pl.BlockSpec((None, RET_HEADS, SUBLANES, HEAD_DIM),
                             lambda i, j: (layer, 0, 0, 0), **resident),
                pl.BlockSpec((None, VEC_ROWS, LRU_WIDTH), lambda i, j: (layer, 0, 0), **resident),
                pl.BlockSpec((None, 2 * LRU_WIDTH // GATE_BLOCK, GATE_BLOCK, GATE_BLOCK),
                             lambda i, j: (layer, 0, 0, 0), **resident)]
    out_specs = [pl.BlockSpec((RET_BLOCK, RET_GROUP_WIDTH), lambda i, j: (rblk(i, j), hg(i, j))),
                 pl.BlockSpec((LRU_BLOCK, LRU_WIDTH), lambda i, j: (smap(i, j), 0))]
    scratch = [pltpu.VMEM((RET_HEADS, HEAD_DIM, HEAD_DIM), F32),
               pltpu.VMEM((LRU_BLOCK + SUBLANES, LRU_WIDTH), F32),
               pltpu.VMEM((LRU_BLOCK, LRU_WIDTH), F32),
               pltpu.VMEM((LRU_BLOCK, LRU_WIDTH), F32),
               pltpu.VMEM((LRU_BLOCK, LRU_WIDTH), F32),
               pltpu.VMEM((1, LRU_WIDTH), F32)]
    return in_specs, out_specs, scratch


def _mixer_out_shapes(rows):
    return [jax.ShapeDtypeStruct((rows, RET_WIDTH), BF16),
            jax.ShapeDtypeStruct((rows, LRU_WIDTH), BF16)]


def _mixer_operands(p, mix):
    return (p,) * 5 + tuple(mix)


def _in_proj_chunk(rows, wb, o_ref, h_ref):
    o_ref[rows, :] = jnp.dot(h_ref[rows, :], wb, preferred_element_type=F32).astype(o_ref.dtype)


def _in_proj_rows(chunks, after, w_ref, o_ref, h_ref, norm=None, piece=None):
    wb = w_ref[...].astype(BF16)
    for rows, pieces in zip(chunks, after):
        if norm is not None:
            x_ref, g_ref = norm
            h_ref[rows, :] = _rms_scale(x_ref[rows, :], g_ref[...]).astype(BF16)
        _in_proj_chunk(rows, wb, o_ref, h_ref)
        for c in pieces:
            piece(c)


def _in_proj_kernel(x_ref, g_ref, w_ref, o_ref, *rest):
    h_ref = rest[-1]
    first = pl.program_id(1) == 0

    def zero_fill():
        for z_ref in rest[:-1]:
            z_ref[...] = jnp.zeros_like(z_ref)

    @pl.when(first)
    def _():
        _in_proj_rows(NORM_CHUNKS, NO_PIECES, w_ref, o_ref, h_ref, norm=(x_ref, g_ref))
        zero_fill()

    @pl.when(jnp.logical_not(first))
    def _():
        _in_proj_rows([slice(0, TM)], [[]], w_ref, o_ref, h_ref)
        zero_fill()


def _in_proj_mix_kernel(*refs):
    x_ref, g_ref, w_ref = refs[:3]
    mix_in = refs[3:3 + N_MIX_IN]
    o_ref = refs[3 + N_MIX_IN]
    mix_out = refs[4 + N_MIX_IN:4 + N_MIX_IN + N_MIX_OUT]
    h_ref = refs[-N_MIX_SCRATCH - 1]
    mix_scratch = refs[-N_MIX_SCRATCH:]
    s = pl.program_id(0) * MIX_STEPS + pl.program_id(1)
    _mixer_resets(s, True, mix_scratch)

    @pl.when(pl.program_id(1) == 0)
    def _():
        h_ref[...] = _rms_scale(x_ref[...], g_ref[...]).astype(BF16)

    ysq = [None]

    def piece(c):
        ysq[0] = _mixer_piece(c, s, ysq[0], mix_in, mix_out, mix_scratch)

    _in_proj_rows(MIX_CHUNKS, PIECES_AFTER, w_ref, o_ref, h_ref, piece=piece)


def _in_proj_specs(x_tile0, layer, tn):
    d = D_MODEL
    in_specs = [pl.BlockSpec((TM, d), lambda i, j: (x_tile0 + i, 0)),
                pl.BlockSpec((None, 1, d), lambda i, j: (layer, 0, 0)),
                pl.BlockSpec((None, d, tn), lambda i, j: (layer, 0, j))]
    out_spec = pl.BlockSpec((TM, tn), lambda i, j: (i, j))
    return in_specs, out_spec


def _in_proj(x, x_tile0, rows, g, w, layer, mix=None, zero_rows=None):
    tn = IN_TN if mix is not None else IN_TN_PLAIN
    in_specs, out_spec = _in_proj_specs(x_tile0, layer, tn)
    grid = (rows // TM, IN_WIDTH // tn)
    p_shape = jax.ShapeDtypeStruct((rows, IN_WIDTH), BF16)
    h_scratch = pltpu.VMEM((TM, D_MODEL), BF16)
    if mix is None:
        out_specs, out_shapes = [out_spec], [p_shape]
        if zero_rows is not None:
            per_tile = ZERO_SLABS_PER_TILE
            slab = zero_rows // (grid[0] * per_tile)
            out_specs.append(pl.BlockSpec(
                (slab, D_MODEL), lambda i, j: (i * per_tile + jnp.minimum(j, per_tile - 1), 0)))
            out_shapes.append(jax.ShapeDtypeStruct((zero_rows, D_MODEL), F32))
        outs = pl.pallas_call(
            _in_proj_kernel, grid=grid, in_specs=in_specs, out_specs=out_specs,
            out_shape=out_shapes, scratch_shapes=[h_scratch],
            compiler_params=_params("parallel", "arbitrary"), name="in_proj",
        )(x, g, w)
        return outs[0] if zero_rows is None else outs
    p_other, mix_params, mix_layer = mix
    m_in, m_out, m_scratch = _mixer_specs(lambda i, j: i * MIX_STEPS + j, mix_layer)
    return pl.pallas_call(
        _in_proj_mix_kernel, grid=grid,
        in_specs=in_specs + m_in, out_specs=[out_spec] + m_out,
        out_shape=[p_shape] + _mixer_out_shapes(p_other.shape[0]),
        scratch_shapes=[h_scratch] + m_scratch,
        compiler_params=_params("arbitrary", "arbitrary"), name="in_proj_mix",
    )(x, g, w, *_mixer_operands(p_other, mix_params))


def _out_proj_kernel(x_ref, r_ref, l_ref, w_ref, o_ref):
    acc = jnp.dot(r_ref[...], w_ref[0:RET_WIDTH, :].astype(BF16), preferred_element_type=F32)
    acc = acc + jnp.dot(l_ref[...], w_ref[RET_WIDTH:, :].astype(BF16), preferred_element_type=F32)
    o_ref[...] = x_ref[...] + acc


def _out_proj(x, x_tile0, o_ret, o_lru, w, layer):
    rows = o_ret.shape[0]
    d = D_MODEL
    return pl.pallas_call(
        _out_proj_kernel,
        grid=(d // OUT_TN, rows // TM),
        in_specs=[pl.BlockSpec((TM, OUT_TN), lambda j, i: (x_tile0 + i, j)),
                  pl.BlockSpec((TM, RET_WIDTH), lambda j, i: (i, 0)),
                  pl.BlockSpec((TM, LRU_WIDTH), lambda j, i: (i, 0)),
                  pl.BlockSpec((None, d, OUT_TN), lambda j, i: (layer, 0, j))],
        out_specs=pl.BlockSpec((TM, OUT_TN), lambda j, i: (i, j)),
        out_shape=jax.ShapeDtypeStruct((rows, d), F32),
        compiler_params=_params("parallel", "arbitrary"),
        name="out_proj",
    )(x, o_ret, o_lru, w)


def _ffn_load_x(x_hbm, o_ref):
    @pl.when(pl.program_id(1) == 0)
    def _():
        row0 = pl.multiple_of(pl.program_id(0) * TM, TM)
        pltpu.sync_copy(x_hbm.at[pl.ds(row0, TM), :], o_ref)


def _ffn_gate_up(rows, wg, wu, h_ref):
    h = h_ref[rows, :]
    return (jnp.dot(h, wg, preferred_element_type=F32), jnp.dot(h, wu, preferred_element_type=F32))


def _ffn_down(rows, gate_up, wd, o_ref):
    gate, up = gate_up
    mid = (gate * _sigmoid(gate) * up).astype(BF16)
    o_ref[rows, :] += jnp.dot(mid, wd, preferred_element_type=F32)


def _ffn_rows(chunks, after, w_refs, o_ref, h_ref, norm=None, piece=None):
    wg, wu, wd = _ffn_weights(*w_refs)
    pending = None
    for rows, pieces in zip(chunks, after):
        if norm is not None:
            x_ref, g_ref = norm
            if x_ref is not None:
                o_ref[rows, :] = x_ref[rows, :]
            h_ref[rows, :] = _rms_scale(o_ref[rows, :], g_ref[...]).astype(BF16)
        gate_up = _ffn_gate_up(rows, wg, wu, h_ref)
        if pending is not None:
            _ffn_down(*pending, wd, o_ref)
        pending = (rows, gate_up)
        for c in pieces:
            piece(c)
    _ffn_down(*pending, wd, o_ref)


def _ffn_weights(wg_ref, wu_ref, wd_ref):
    return wg_ref[...].astype(BF16), wu_ref[...].astype(BF16), wd_ref[...].astype(BF16)


def _ffn_epilogue(fg_ref, o_ref, final_norm):
    if final_norm:
        @pl.when(pl.program_id(1) == pl.num_programs(1) - 1)
        def _():
            o_ref[...] = _rms_scale(o_ref[...], fg_ref[...])


def _ffn_kernel(x_ref, g_ref, wg_ref, wu_ref, wd_ref, fg_ref, *rest, final_norm, n_buf, x_in_hbm):
    o_ref, h_ref = rest[n_buf], rest[-1]
    if x_in_hbm:
        _ffn_load_x(x_ref, o_ref)
    weights = (wg_ref, wu_ref, wd_ref)
    first = pl.program_id(1) == 0

    @pl.when(first)
    def _():
        _ffn_rows(NORM_CHUNKS, NO_PIECES, weights, o_ref, h_ref,
                  norm=(None if x_in_hbm else x_ref, g_ref))

    @pl.when(jnp.logical_not(first))
    def _():
        _ffn_rows([slice(0, TM)], [[]], weights, o_ref, h_ref)

    _ffn_epilogue(fg_ref, o_ref, final_norm)


def _ffn_mix_kernel(*refs, final_norm, n_buf):
    x_hbm, g_ref, wg_ref, wu_ref, wd_ref, fg_ref = refs[:6]
    mix_in = refs[6:6 + N_MIX_IN]
    n_in = 6 + N_MIX_IN + n_buf
    o_ref = refs[n_in]
    mix_out = refs[n_in + 1:n_in + 1 + N_MIX_OUT]
    h_ref = refs[-N_MIX_SCRATCH - 1]
    mix_scratch = refs[-N_MIX_SCRATCH:]
    j = pl.program_id(1)
    active = j < MIX_STEPS
    s = pl.program_id(0) * MIX_STEPS + jnp.minimum(j, MIX_STEPS - 1)
    _ffn_load_x(x_hbm, o_ref)
    _mixer_resets(s, active, mix_scratch)
    weights = (wg_ref, wu_ref, wd_ref)

    @pl.when(j == 0)
    def _():
        h_ref[...] = _rms_scale(o_ref[...], g_ref[...]).astype(BF16)

    @pl.when(active)
    def _():
        ysq = [None]

        def piece(c):
            ysq[0] = _mixer_piece(c, s, ysq[0], mix_in, mix_out, mix_scratch)

        _ffn_rows(MIX_CHUNKS, PIECES_AFTER, weights, o_ref, h_ref, piece=piece)

    @pl.when(jnp.logical_not(active))
    def _():
        _ffn_rows([slice(0, TM)], [[]], weights, o_ref, h_ref)

    _ffn_epilogue(fg_ref, o_ref, final_norm)


def _ffn(x, g, wg, wu, wd, fg, layer, final_norm, mix=None, out_rows=None, out_tile0=0,
         out_buf=None):
    rows, d = x.shape
    out_rows = rows if out_rows is None else out_rows
    grid = (rows // TM, wg.shape[2] // FFN_TF)
    x_in_hbm = mix is not None
    x_spec = pl.BlockSpec(memory_space=pl.ANY) if x_in_hbm else pl.BlockSpec((TM, d), lambda i, j: (i, 0))
    in_specs = [x_spec,
                pl.BlockSpec((None, 1, d), lambda i, j: (layer, 0, 0)),
                pl.BlockSpec((None, d, FFN_TF), lambda i, j: (layer, 0, j)),
                pl.BlockSpec((None, d, FFN_TF), lambda i, j: (layer, 0, j)),
                pl.BlockSpec((None, FFN_TF, d), lambda i, j: (layer, j, 0)),
                pl.BlockSpec((1, d), lambda i, j: (0, 0))]
    out_spec = pl.BlockSpec((TM, d), lambda i, j: (out_tile0 + i, 0))
    out_shape = jax.ShapeDtypeStruct((out_rows, d), F32)
    h_scratch = pltpu.VMEM((TM, d), BF16)
    operands = (x, g, wg, wu, wd, fg)
    buf_specs, bufs = ([], ()) if out_buf is None else ([pl.BlockSpec(memory_space=pl.ANY)], (out_buf,))
    if mix is None:
        return pl.pallas_call(
            functools.partial(_ffn_kernel, final_norm=final_norm, n_buf=len(bufs), x_in_hbm=x_in_hbm),
            grid=grid, in_specs=in_specs + buf_specs, out_specs=out_spec, out_shape=out_shape,
            scratch_shapes=[h_scratch],
            input_output_aliases={len(operands): 0} if bufs else {},
            compiler_params=_params("parallel", "arbitrary", vmem=VMEM_LIMIT_FFN),
            name="ffn_final" if final_norm else "ffn",
        )(*operands, *bufs)
    p_other, mix_params = mix
    m_in, m_out, m_scratch = _mixer_specs(
        lambda i, j: i * MIX_STEPS + jnp.minimum(j, MIX_STEPS - 1), layer)
    operands = operands + _mixer_operands(p_other, mix_params)
    return pl.pallas_call(
        functools.partial(_ffn_mix_kernel, final_norm=final_norm, n_buf=len(bufs)),
        grid=grid, in_specs=in_specs + m_in + buf_specs, out_specs=[out_spec] + m_out,
        out_shape=[out_shape] + _mixer_out_shapes(p_other.shape[0]),
        scratch_shapes=[h_scratch] + m_scratch,
        input_output_aliases={len(operands): 0} if bufs else {},
        compiler_params=_params("arbitrary", "arbitrary"),
        name="ffn_mix_final" if final_norm else "ffn_mix",
    )(*operands, *bufs)


def _retention_tables():
    cs = RET_BLOCK
    pos = np.arange(SEQ, dtype=np.float32)
    inv = (1.0 / (ROPE_BASE ** (np.arange(0, HEAD_DIM, 2, dtype=np.float32) / HEAD_DIM))).astype(np.float32)
    ang = jnp.asarray(pos[:, None] * inv[None, :])
    cos, sin = jnp.cos(ang), jnp.sin(ang)
    cfull = jnp.concatenate([cos, cos], axis=-1)
    ssign = jnp.concatenate([-sin, sin], axis=-1)
    kscale = HEAD_DIM ** -0.5
    gamma_log = np.log1p(-np.exp2(-5.0 - np.arange(RET_HEADS, dtype=np.float64)))
    idx = np.arange(cs)
    dist = np.abs(idx[:, None] - idx[None, :])
    visible = (idx[None, :] // CHUNK) <= (idx[:, None] // CHUNK)
    dmask = kscale * np.where(visible[None], np.exp(gamma_log[:, None, None] * dist[None]), 0.0)
    qdec = np.exp(gamma_log[:, None] * (idx + 1.0)[None, :])
    kdec = kscale * np.exp(gamma_log[:, None] * (cs - 1.0 - idx)[None, :])
    bcast = lambda a: np.broadcast_to(a[:, :, None], (RET_HEADS, cs, HEAD_DIM))
    decay = np.concatenate([dmask, bcast(qdec), bcast(kdec)], axis=-1)
    return (jnp.concatenate([cfull, ssign], axis=-1), jnp.asarray(decay, F32),
            jnp.asarray(np.exp(gamma_log * cs), F32))


def _block_diag(w):
    per = GATE_BLOCK // LRU_GROUP_DIM
    nb = LRU_GROUPS // per
    w4 = w.reshape(w.shape[0], nb, per, LRU_GROUP_DIM, LRU_GROUP_DIM)
    bd = jnp.einsum('lcipq,ij->lcipjq', w4, jnp.eye(per, dtype=w.dtype))
    return bd.reshape(w.shape[0], nb, GATE_BLOCK, GATE_BLOCK).astype(BF16)


def kernel(x, norm1_g, w_in, ret_gn_g, lru_conv_w, lru_conv_b, lru_wa, lru_ba, lru_wx, lru_bx,
           lru_lambda, lru_norm_g, w_out, norm2_g, ffn_w_gate, ffn_w_up, ffn_w_down, final_g):
    batch, seq, d = x.shape
    depth = w_in.shape[0]
    assert (seq, d) == (SEQ, D_MODEL) and batch % 2 == 0
    rows = batch * seq
    half = rows // 2
    half_tiles = half // TM
    vecs = lambda a: a.reshape(depth, 1, -1).astype(F32)
    rot, decay, sdec = _retention_tables()
    head_rows = jnp.stack([ret_gn_g.reshape(depth, RET_HEADS, HEAD_DIM).astype(F32),
                           jnp.broadcast_to(sdec[None, :, None], (depth, RET_HEADS, HEAD_DIM))], axis=2)
    head = jnp.pad(head_rows, ((0, 0), (0, 0), (0, SUBLANES - 2), (0, 0)))
    vec_rows = jnp.concatenate([lru_conv_w.astype(F32), vecs(lru_conv_b), vecs(lru_ba), vecs(lru_bx),
                                vecs(lru_lambda), vecs(lru_norm_g)], axis=1)
    vec = jnp.pad(vec_rows, ((0, 0), (0, VEC_ROWS - vec_rows.shape[1]), (0, 0)))
    gates = jnp.concatenate([_block_diag(lru_wa), _block_diag(lru_wx)], axis=1)
    mix_params = (rot, decay, head, vec, gates)
    g1, g2, fg = vecs(norm1_g), vecs(norm2_g), final_g.reshape(1, d).astype(F32)

    assert depth >= 2
    xf = x.reshape(rows, d)
    xa, xb = (xf, 0), (xf, half_tiles)
    p_a, out_buf = _in_proj(*xa, half, g1, w_in, 0, zero_rows=rows)
    for l in range(depth):
        last = l == depth - 1
        ffn_w = (ffn_w_gate, ffn_w_up, ffn_w_down, fg, l, last)
        p_b, ret_a, lru_a = _in_proj(*xb, half, g1, w_in, l, mix=(p_a, mix_params, l))
        x1_a = _out_proj(*xa, ret_a, lru_a, w_out, l)
        if last:
            x2_a, ret_b, lru_b = _ffn(x1_a, g2, *ffn_w, mix=(p_b, mix_params), out_rows=rows,
                                      out_buf=out_buf)
            x1_b = _out_proj(*xb, ret_b, lru_b, w_out, l)
            return _ffn(x1_b, g2, *ffn_w, out_rows=rows, out_tile0=half_tiles,
                        out_buf=x2_a).reshape(batch, seq, d)
        x2_a = _ffn(x1_a, g2, *ffn_w)
        xa = (x2_a, 0)
        p_a, ret_b, lru_b = _in_proj(*xa, half, g1, w_in, l + 1, mix=(p_b, mix_params, l))
        x1_b = _out_proj(*xb, ret_b, lru_b, w_out, l)
        x2_b = _ffn(x1_b, g2, *ffn_w)
        xb = (x2_b, 0)
```

```python
import functools

import numpy as np
import jax
import jax.numpy as jnp
from jax import lax
from jax.experimental import pallas as pl
from jax.experimental.pallas import tpu as pltpu

D_MODEL = 2048
SEQ = 2048
RET_HEADS = 8
HEAD_DIM = 128
RET_WIDTH = RET_HEADS * HEAD_DIM
LRU_WIDTH = 1024
LRU_GROUPS = 16
LRU_GROUP_DIM = 64
CONV_W = 4
LRU_C = 8.0
IN_WIDTH = 4 * RET_WIDTH + 2 * LRU_WIDTH
CHUNK = 64
ROPE_BASE = 10000.0
EPS = 1e-6

F32 = jnp.float32
BF16 = jnp.bfloat16

SUBLANES = 8
BF16_ROWS = 16
TM = 1024
IN_TN = 768
IN_TN_PLAIN = 1024
FFN_TF = 256
FFN_TF_BF16 = 512
CAST_SLABS = 64
OUT_TN = 1024
MIX_STEPS = IN_WIDTH // IN_TN
LRU_BLOCK = TM // MIX_STEPS
RET_GROUPS = 2
RET_GROUP_HEADS = RET_HEADS // RET_GROUPS
RET_GROUP_WIDTH = RET_GROUP_HEADS * HEAD_DIM
RET_BLOCK = LRU_BLOCK * RET_GROUPS
GATE_BLOCK = 256
VMEM_LIMIT = 55 * 1024 * 1024
VMEM_LIMIT_FFN = 57 * 1024 * 1024
ZERO_SLABS_PER_TILE = 4


def _params(*sem, vmem=VMEM_LIMIT):
    return pltpu.CompilerParams(dimension_semantics=sem, vmem_limit_bytes=vmem)


def _sigmoid(z):
    return 0.5 * jnp.tanh(0.5 * z) + 0.5


def _rms_scale(x, g):
    ms = jnp.mean(x * x, axis=-1, keepdims=True)
    return x * lax.rsqrt(ms + EPS) * g


def _retention_head(h, hg, q_ref, k_ref, v_ref, g_ref, rot_ref, decay_ref, head_ref, o_ref, s_ref):
    cos, sin = rot_ref[:, :HEAD_DIM], rot_ref[:, HEAD_DIM:]
    half = HEAD_DIM // 2
    sl = slice(h * HEAD_DIM, (h + 1) * HEAD_DIM)
    q = q_ref[:, sl].astype(F32)
    k = k_ref[:, sl].astype(F32)
    v = v_ref[:, sl]
    qr = q * cos + pltpu.roll(q, half, 1) * sin
    kr = k * cos + pltpu.roll(k, half, 1) * sin
    qb = qr.astype(BF16)
    kb = kr.astype(BF16)
    sidx = hg * RET_GROUP_HEADS + h
    dmask = decay_ref[sidx, :, 0:RET_BLOCK]
    qdec = decay_ref[sidx, :, RET_BLOCK:RET_BLOCK + HEAD_DIM]
    kdec = decay_ref[sidx, :, RET_BLOCK + HEAD_DIM:RET_BLOCK + 2 * HEAD_DIM]
    gn, sdec = head_ref[sidx, 0:1, :], head_ref[sidx, 1:2, :]
    scores = lax.dot_general(qb, kb, (((1,), (1,)), ((), ())),
                             preferred_element_type=F32) * dmask
    o = jnp.dot(scores.astype(BF16), v, preferred_element_type=F32)
    state = s_ref[sidx]
    o = o + qdec * jnp.dot(qb, state.astype(BF16), preferred_element_type=F32)
    kd = (kr * kdec).astype(BF16)
    s_ref[sidx] = sdec * state + lax.dot_general(
        kd, v, (((0,), (0,)), ((), ())), preferred_element_type=F32)
    mu = jnp.mean(o, axis=-1, keepdims=True)
    oc = o - mu
    var = jnp.mean(oc * oc, axis=-1, keepdims=True)
    on = oc * lax.rsqrt(var + EPS) * gn
    gate = g_ref[:, sl].astype(F32)
    o_ref[:, sl] = (on * (gate * _sigmoid(gate))).astype(o_ref.dtype)


def _lru_cols(c, xy_ref, vec_ref, gates_ref, xpad_ref, a_ref, b_ref, y_ref, carry_ref):
    tt = xy_ref.shape[0]
    pad = SUBLANES
    sl = slice(c * GATE_BLOCK, (c + 1) * GATE_BLOCK)
    ysl = slice(LRU_WIDTH + c * GATE_BLOCK, LRU_WIDTH + (c + 1) * GATE_BLOCK)
    vrow = lambda r: vec_ref[r:r + 1, sl]
    x = xy_ref[:, sl].astype(F32)
    xpad_ref[pad:pad + tt, sl] = x
    xc = vrow(VEC_CONV_B) + vrow(VEC_CONV_W + CONV_W - 1) * x
    for j in range(CONV_W - 1):
        off = pad - (CONV_W - 1) + j
        xc = xc + vrow(VEC_CONV_W + j) * xpad_ref[off:off + tt, sl]
    xpad_ref[0:pad, sl] = x[tt - pad:tt, :]

    lam = vrow(VEC_LAMBDA)
    softplus_neg_lam = jnp.maximum(-lam, 0.0) + jnp.log1p(jnp.exp(-jnp.abs(lam)))
    xc16 = xc.astype(BF16)
    n_blocks = LRU_WIDTH // GATE_BLOCK
    r = _sigmoid(jnp.dot(xc16, gates_ref[c], preferred_element_type=F32) + vrow(VEC_BA))
    i = _sigmoid(jnp.dot(xc16, gates_ref[n_blocks + c], preferred_element_type=F32) + vrow(VEC_BX))
    neg_log_a = LRU_C * r * softplus_neg_lam
    a = jnp.exp(-neg_log_a)
    a_ref[:, sl] = a
    one_minus_a2 = jnp.tanh(neg_log_a) * (a * a + 1.0)
    root = jnp.where(one_minus_a2 > 0.0, one_minus_a2 * lax.rsqrt(one_minus_a2), 0.0)
    b_ref[:, sl] = root * (i * xc)

    row = lax.broadcasted_iota(jnp.int32, (SUBLANES, GATE_BLOCK), 0)
    carry = carry_ref[:, sl]
    for gidx in range(tt // SUBLANES):
        rows = slice(gidx * SUBLANES, (gidx + 1) * SUBLANES)
        a = a_ref[rows, sl]
        b = b_ref[rows, sl]
        for s in (1, 2, 4):
            keep = row >= s
            b = jnp.where(keep, a * pltpu.roll(b, s, 0) + b, b)
            a = jnp.where(keep, a * pltpu.roll(a, s, 0), a)
        h = a * carry + b
        b_ref[rows, sl] = h
        carry = h[SUBLANES - 1:SUBLANES, :]
    carry_ref[:, sl] = carry

    y = b_ref[:, sl] * jax.nn.gelu(xy_ref[:, ysl].astype(F32))
    y_ref[:, sl] = y
    return y * y


def _lru_finish(ysq, vec_ref, o_ref, y_ref):
    ms = jnp.sum(ysq, axis=-1, keepdims=True) * (1.0 / LRU_WIDTH)
    ng = vec_ref[VEC_NORM_G:VEC_NORM_G + 1, :]
    o_ref[...] = (y_ref[...] * lax.rsqrt(ms + EPS) * ng).astype(o_ref.dtype)


VEC_CONV_W, VEC_CONV_B, VEC_BA, VEC_BX, VEC_LAMBDA, VEC_NORM_G = 0, CONV_W, CONV_W + 1, CONV_W + 2, CONV_W + 3, CONV_W + 4
VEC_ROWS = 16
N_MIX_IN = 10
N_MIX_OUT = 2
N_MIX_SCRATCH = 6
MIX_PIECES = LRU_WIDTH // GATE_BLOCK
assert MIX_PIECES == RET_GROUP_HEADS


def _row_chunks(n):
    step = TM // n // (2 * BF16_ROWS) * (2 * BF16_ROWS)
    return [slice(c * step, (c + 1) * step if c < n - 1 else TM) for c in range(n)]


MIX_CHUNKS = _row_chunks(MIX_PIECES + 1)
PIECES_AFTER = [[c] for c in range(MIX_PIECES)] + [[]]
NORM_CHUNKS = _row_chunks(4)
NO_PIECES = [[]] * len(NORM_CHUNKS)


def _mixer_resets(s, active, mix_scratch):
    s_ref, xpad_ref, _, _, _, carry_ref = mix_scratch
    lru_blocks = SEQ // LRU_BLOCK
    hg = s % RET_GROUPS

    @pl.when(jnp.logical_and(active, s % lru_blocks == 0))
    def _():
        xpad_ref[0:SUBLANES, :] = jnp.zeros((SUBLANES, LRU_WIDTH), F32)
        carry_ref[...] = jnp.zeros_like(carry_ref)

    @pl.when(jnp.logical_and(active, (s // RET_GROUPS) % (SEQ // RET_BLOCK) == 0))
    def _():
        s_ref[pl.ds(hg * RET_GROUP_HEADS, RET_GROUP_HEADS)] = jnp.zeros(
            (RET_GROUP_HEADS, HEAD_DIM, HEAD_DIM), F32)


def _mixer_piece(c, s, ysq, mix_in, mix_out, mix_scratch):
    q, k, v, g, xy, rot, decay, head, vec, gates = mix_in
    o_ret, o_lru = mix_out
    s_ref, xpad_ref, a_ref, b_ref, y_ref, carry_ref = mix_scratch
    _retention_head(c, s % RET_GROUPS, q, k, v, g, rot, decay, head, o_ret, s_ref)
    ysq_c = _lru_cols(c, xy, vec, gates, xpad_ref, a_ref, b_ref, y_ref, carry_ref)
    ysq = ysq_c if ysq is None else ysq + ysq_c
    if c == MIX_PIECES - 1:
        _lru_finish(ysq, vec, o_lru, y_ref)
    return ysq


def _mixer_specs(smap, layer):
    hg = lambda i, j: smap(i, j) % RET_GROUPS
    rblk = lambda i, j: smap(i, j) // RET_GROUPS
    tblk = lambda i, j: rblk(i, j) % (SEQ // RET_BLOCK)
    pcol = lambda c: pl.BlockSpec((RET_BLOCK, RET_GROUP_WIDTH),
                                  lambda i, j, c=c: (rblk(i, j), c * RET_GROUPS + hg(i, j)))
    resident = dict(pipeline_mode=pl.Buffered(1))
    in_specs = [pcol(0), pcol(1), pcol(2), pcol(3),
                pl.BlockSpec((LRU_BLOCK, 2 * LRU_WIDTH),
                             lambda i, j: (smap(i, j), 4 * RET_WIDTH // (2 * LRU_WIDTH))),
                pl.BlockSpec((RET_BLOCK, 2 * HEAD_DIM), lambda i, j: (tblk(i, j), 0)),
                pl.BlockSpec((RET_HEADS, RET_BLOCK, RET_BLOCK + 2 * HEAD_DIM),
                             lambda i, j: (0, 0, 0), **resident),
                pl.BlockSpec((None, RET_HEADS, SUBLANES, HEAD_DIM),
                             lambda i, j: (layer, 0, 0, 0), **resident),
                pl.BlockSpec((None, VEC_ROWS, LRU_WIDTH), lambda i, j: (layer, 0, 0), **resident),
                pl.BlockSpec((None, 2 * LRU_WIDTH // GATE_BLOCK, GATE_BLOCK, GATE_BLOCK),
                             lambda i, j: (layer, 0, 0, 0), **resident)]
    out_specs = [pl.BlockSpec((RET_BLOCK, RET_GROUP_WIDTH), lambda i, j: (rblk(i, j), hg(i, j))),
                 pl.BlockSpec((LRU_BLOCK, LRU_WIDTH), lambda i, j: (smap(i, j), 0))]
    scratch = [pltpu.VMEM((RET_HEADS, HEAD_DIM, HEAD_DIM), F32),
               pltpu.VMEM((LRU_BLOCK + SUBLANES, LRU_WIDTH), F32),
               pltpu.VMEM((LRU_BLOCK, LRU_WIDTH), F32),
               pltpu.VMEM((LRU_BLOCK, LRU_WIDTH), F32),
               pltpu.VMEM((LRU_BLOCK, LRU_WIDTH), F32),
               pltpu.VMEM((1, LRU_WIDTH), F32)]
    return in_specs, out_specs, scratch


def _mixer_out_shapes(rows):
    return [jax.ShapeDtypeStruct((rows, RET_WIDTH), BF16),
            jax.ShapeDtypeStruct((rows, LRU_WIDTH), BF16)]


def _mixer_operands(p, mix):
    return (p,) * 5 + tuple(mix)


def _in_proj_chunk(rows, wb, o_ref, h_ref):
    o_ref[rows, :] = jnp.dot(h_ref[rows, :], wb, preferred_element_type=F32).astype(o_ref.dtype)


def _in_proj_rows(chunks, after, w_ref, o_ref, h_ref, norm=None, piece=None):
    wb = w_ref[...].astype(BF16)
    for rows, pieces in zip(chunks, after):
        if norm is not None:
            x_ref, g_ref = norm
            h_ref[rows, :] = _rms_scale(x_ref[rows, :], g_ref[...]).astype(BF16)
        _in_proj_chunk(rows, wb, o_ref, h_ref)
        for c in pieces:
            piece(c)


def _in_proj_kernel(x_ref, g_ref, w_ref, o_ref, *rest):
    h_ref = rest[-1]
    first = pl.program_id(1) == 0

    def zero_fill():
        for z_ref in rest[:-1]:
            z_ref[...] = jnp.zeros_like(z_ref)

    @pl.when(first)
    def _():
        _in_proj_rows(NORM_CHUNKS, NO_PIECES, w_ref, o_ref, h_ref, norm=(x_ref, g_ref))
        zero_fill()

    @pl.when(jnp.logical_not(first))
    def _():
        _in_proj_rows([slice(0, TM)], [[]], w_ref, o_ref, h_ref)
        zero_fill()


def _in_proj_mix_kernel(*refs):
    x_ref, g_ref, w_ref = refs[:3]
    mix_in = refs[3:3 + N_MIX_IN]
    o_ref = refs[3 + N_MIX_IN]
    mix_out = refs[4 + N_MIX_IN:4 + N_MIX_IN + N_MIX_OUT]
    h_ref = refs[-N_MIX_SCRATCH - 1]
    mix_scratch = refs[-N_MIX_SCRATCH:]
    s = pl.program_id(0) * MIX_STEPS + pl.program_id(1)
    _mixer_resets(s, True, mix_scratch)

    @pl.when(pl.program_id(1) == 0)
    def _():
        h_ref[...] = _rms_scale(x_ref[...], g_ref[...]).astype(BF16)

    ysq = [None]

    def piece(c):
        ysq[0] = _mixer_piece(c, s, ysq[0], mix_in, mix_out, mix_scratch)

    _in_proj_rows(MIX_CHUNKS, PIECES_AFTER, w_ref, o_ref, h_ref, piece=piece)


def _in_proj_specs(x_tile0, layer, tn):
    d = D_MODEL
    in_specs = [pl.BlockSpec((TM, d), lambda i, j: (x_tile0 + i, 0)),
                pl.BlockSpec((None, 1, d), lambda i, j: (layer, 0, 0)),
                pl.BlockSpec((None, d, tn), lambda i, j: (layer, 0, j))]
    out_spec = pl.BlockSpec((TM, tn), lambda i, j: (i, j))
    return in_specs, out_spec


def _in_proj(x, x_tile0, rows, g, w, layer, mix=None, zero_rows=None):
    tn = IN_TN if mix is not None else IN_TN_PLAIN
    in_specs, out_spec = _in_proj_specs(x_tile0, layer, tn)
    grid = (rows // TM, IN_WIDTH // tn)
    p_shape = jax.ShapeDtypeStruct((rows, IN_WIDTH), BF16)
    h_scratch = pltpu.VMEM((TM, D_MODEL), BF16)
    if mix is None:
        out_specs, out_shapes = [out_spec], [p_shape]
        if zero_rows is not None:
            per_tile = ZERO_SLABS_PER_TILE
            slab = zero_rows // (grid[0] * per_tile)
            out_specs.append(pl.BlockSpec(
                (slab, D_MODEL), lambda i, j: (i * per_tile + jnp.minimum(j, per_tile - 1), 0)))
            out_shapes.append(jax.ShapeDtypeStruct((zero_rows, D_MODEL), F32))
        outs = pl.pallas_call(
            _in_proj_kernel, grid=grid, in_specs=in_specs, out_specs=out_specs,
            out_shape=out_shapes, scratch_shapes=[h_scratch],
            compiler_params=_params("parallel", "arbitrary"), name="in_proj",
        )(x, g, w)
        return outs[0] if zero_rows is None else outs
    p_other, mix_params, mix_layer = mix
    m_in, m_out, m_scratch = _mixer_specs(lambda i, j: i * MIX_STEPS + j, mix_layer)
    return pl.pallas_call(
        _in_proj_mix_kernel, grid=grid,
        in_specs=in_specs + m_in, out_specs=[out_spec] + m_out,
        out_shape=[p_shape] + _mixer_out_shapes(p_other.shape[0]),
        scratch_shapes=[h_scratch] + m_scratch,
        compiler_params=_params("arbitrary", "arbitrary"), name="in_proj_mix",
    )(x, g, w, *_mixer_operands(p_other, mix_params))


def _out_proj_kernel(x_ref, r_ref, l_ref, w_ref, o_ref):
    acc = jnp.dot(r_ref[...], w_ref[0:RET_WIDTH, :].astype(BF16), preferred_element_type=F32)
    acc = acc + jnp.dot(l_ref[...], w_ref[RET_WIDTH:, :].astype(BF16), preferred_element_type=F32)
    o_ref[...] = x_ref[...] + acc


def _out_proj(x, x_tile0, o_ret, o_lru, w, layer):
    rows = o_ret.shape[0]
    d = D_MODEL
    if w.ndim == 3:
        w_spec = pl.BlockSpec((None, d, OUT_TN), lambda j, i: (layer, 0, j))
    else:
        w_spec = pl.BlockSpec((d, OUT_TN), lambda j, i: (0, j))
    return pl.pallas_call(
        _out_proj_kernel,
        grid=(d // OUT_TN, rows // TM),
        in_specs=[pl.BlockSpec((TM, OUT_TN), lambda j, i: (x_tile0 + i, j)),
                  pl.BlockSpec((TM, RET_WIDTH), lambda j, i: (i, 0)),
                  pl.BlockSpec((TM, LRU_WIDTH), lambda j, i: (i, 0)),
                  w_spec],
        out_specs=pl.BlockSpec((TM, OUT_TN), lambda j, i: (i, j)),
        out_shape=jax.ShapeDtypeStruct((rows, d), F32),
        compiler_params=_params("parallel", "arbitrary"),
        name="out_proj",
    )(x, o_ret, o_lru, w)


def _ffn_load_x(x_hbm, o_ref):
    @pl.when(pl.program_id(1) == 0)
    def _():
        row0 = pl.multiple_of(pl.program_id(0) * TM, TM)
        pltpu.sync_copy(x_hbm.at[pl.ds(row0, TM), :], o_ref)


def _ffn_gate_up(rows, wg, wu, h_ref):
    h = h_ref[rows, :]
    return (jnp.dot(h, wg, preferred_element_type=F32), jnp.dot(h, wu, preferred_element_type=F32))


def _ffn_down(rows, gate_up, wd, o_ref):
    gate, up = gate_up
    mid = (gate * _sigmoid(gate) * up).astype(BF16)
    o_ref[rows, :] += jnp.dot(mid, wd, preferred_element_type=F32)


def _ffn_rows(chunks, after, w_refs, o_ref, h_ref, norm=None, piece=None):
    wg, wu, wd = _ffn_weights(*w_refs)
    pending = None
    for rows, pieces in zip(chunks, after):
        if norm is not None:
            x_ref, g_ref = norm
            if x_ref is not None:
                o_ref[rows, :] = x_ref[rows, :]
            h_ref[rows, :] = _rms_scale(o_ref[rows, :], g_ref[...]).astype(BF16)
        gate_up = _ffn_gate_up(rows, wg, wu, h_ref)
        if pending is not None:
            _ffn_down(*pending, wd, o_ref)
        pending = (rows, gate_up)
        for c in pieces:
            piece(c)
    _ffn_down(*pending, wd, o_ref)


def _ffn_weights(wg_ref, wu_ref, wd_ref):
    return wg_ref[...].astype(BF16), wu_ref[...].astype(BF16), wd_ref[...].astype(BF16)


def _ffn_epilogue(fg_ref, o_ref, final_norm):
    if final_norm:
        @pl.when(pl.program_id(1) == pl.num_programs(1) - 1)
        def _():
            o_ref[...] = _rms_scale(o_ref[...], fg_ref[...])


def _ffn_kernel(x_ref, g_ref, wg_ref, wu_ref, wd_ref, fg_ref, *rest, final_norm, n_buf, n_cast,
                x_in_hbm):
    cast_in = rest[n_buf:n_buf + n_cast]
    o_ref = rest[n_buf + n_cast]
    cast_out = rest[n_buf + n_cast + 1:n_buf + 2 * n_cast + 1]
    h_ref = rest[-1]
    if x_in_hbm:
        _ffn_load_x(x_ref, o_ref)
    weights = (wg_ref, wu_ref, wd_ref)
    first = pl.program_id(1) == 0

    def cast_slabs():
        for src, dst in zip(cast_in, cast_out):
            dst[...] = src[...].astype(dst.dtype)

    @pl.when(first)
    def _():
        _ffn_rows(NORM_CHUNKS, NO_PIECES, weights, o_ref, h_ref,
                  norm=(None if x_in_hbm else x_ref, g_ref))
        cast_slabs()

    @pl.when(jnp.logical_not(first))
    def _():
        _ffn_rows([slice(0, TM)], [[]], weights, o_ref, h_ref)
        cast_slabs()

    _ffn_epilogue(fg_ref, o_ref, final_norm)


def _ffn_mix_kernel(*refs, final_norm, n_buf):
    x_hbm, g_ref, wg_ref, wu_ref, wd_ref, fg_ref = refs[:6]
    mix_in = refs[6:6 + N_MIX_IN]
    n_in = 6 + N_MIX_IN + n_buf
    o_ref = refs[n_in]
    mix_out = refs[n_in + 1:n_in + 1 + N_MIX_OUT]
    h_ref = refs[-N_MIX_SCRATCH - 1]
    mix_scratch = refs[-N_MIX_SCRATCH:]
    j = pl.program_id(1)
    active = j < MIX_STEPS
    s = pl.program_id(0) * MIX_STEPS + jnp.minimum(j, MIX_STEPS - 1)
    _ffn_load_x(x_hbm, o_ref)
    _mixer_resets(s, active, mix_scratch)
    weights = (wg_ref, wu_ref, wd_ref)

    @pl.when(j == 0)
    def _():
        h_ref[...] = _rms_scale(o_ref[...], g_ref[...]).astype(BF16)

    @pl.when(active)
    def _():
        ysq = [None]

        def piece(c):
            ysq[0] = _mixer_piece(c, s, ysq[0], mix_in, mix_out, mix_scratch)

        _ffn_rows(MIX_CHUNKS, PIECES_AFTER, weights, o_ref, h_ref, piece=piece)

    @pl.when(jnp.logical_not(active))
    def _():
        _ffn_rows([slice(0, TM)], [[]], weights, o_ref, h_ref)

    _ffn_epilogue(fg_ref, o_ref, final_norm)


def _ffn(x, g, wg, wu, wd, fg, layer, final_norm, mix=None, out_rows=None, out_tile0=0,
         out_buf=None, cast=()):
    rows, d = x.shape
    out_rows = rows if out_rows is None else out_rows
    stacked = wg.ndim == 3
    tf = FFN_TF if stacked else FFN_TF_BF16
    grid = (rows // TM, wg.shape[-1] // tf)
    steps = grid[0] * grid[1]

    def wspec(block, idx):
        if stacked:
            return pl.BlockSpec((None,) + block, lambda i, j: (layer,) + idx(i, j))
        return pl.BlockSpec(block, idx)

    x_in_hbm = mix is not None
    x_spec = pl.BlockSpec(memory_space=pl.ANY) if x_in_hbm else pl.BlockSpec((TM, d), lambda i, j: (i, 0))
    in_specs = [x_spec,
                pl.BlockSpec((None, 1, d), lambda i, j: (layer, 0, 0)),
                wspec((d, tf), lambda i, j: (0, j)),
                wspec((d, tf), lambda i, j: (0, j)),
                wspec((tf, d), lambda i, j: (j, 0)),
                pl.BlockSpec((1, d), lambda i, j: (0, 0))]
    cast_specs, cast_out_specs, cast_shapes = [], [], []
    for w, w_layer in cast:
        slab = w.shape[1] // steps if w.shape[1] % steps == 0 else w.shape[1] // CAST_SLABS
        n_slabs = w.shape[1] // slab
        idx = lambda i, j, n=n_slabs: jnp.minimum(i * grid[1] + j, n - 1)
        cast_specs.append(pl.BlockSpec((None, slab, w.shape[2]),
                                       lambda i, j, idx=idx, wl=w_layer: (wl, idx(i, j), 0)))
        cast_out_specs.append(pl.BlockSpec((slab, w.shape[2]), lambda i, j, idx=idx: (idx(i, j), 0)))
        cast_shapes.append(jax.ShapeDtypeStruct(w.shape[1:], BF16))
    out_spec = pl.BlockSpec((TM, d), lambda i, j: (out_tile0 + i, 0))
    out_shape = jax.ShapeDtypeStruct((out_rows, d), F32)
    h_scratch = pltpu.VMEM((TM, d), BF16)
    operands = (x, g, wg, wu, wd, fg)
    buf_specs, bufs = ([], ()) if out_buf is None else ([pl.BlockSpec(memory_space=pl.ANY)], (out_buf,))
    if mix is None:
        outs = pl.pallas_call(
            functools.partial(_ffn_kernel, final_norm=final_norm, n_buf=len(bufs), n_cast=len(cast),
                              x_in_hbm=x_in_hbm),
            grid=grid, in_specs=in_specs + buf_specs + cast_specs,
            out_specs=[out_spec] + cast_out_specs, out_shape=[out_shape] + cast_shapes,
            scratch_shapes=[h_scratch],
            input_output_aliases={len(operands): 0} if bufs else {},
            compiler_params=_params("parallel", "arbitrary", vmem=VMEM_LIMIT_FFN),
            name="ffn_final" if final_norm else "ffn",
        )(*operands, *bufs, *(w for w, _ in cast))
        return outs if cast else outs[0]
    p_other, mix_params = mix
    m_in, m_out, m_scratch = _mixer_specs(
        lambda i, j: i * MIX_STEPS + jnp.minimum(j, MIX_STEPS - 1), layer)
    operands = operands + _mixer_operands(p_other, mix_params)
    return pl.pallas_call(
        functools.partial(_ffn_mix_kernel, final_norm=final_norm, n_buf=len(bufs)),
        grid=grid, in_specs=in_specs + m_in + buf_specs, out_specs=[out_spec] + m_out,
        out_shape=[out_shape] + _mixer_out_shapes(p_other.shape[0]),
        scratch_shapes=[h_scratch] + m_scratch,
        input_output_aliases={len(operands): 0} if bufs else {},
        compiler_params=_params("arbitrary", "arbitrary"),
        name="ffn_mix_final" if final_norm else "ffn_mix",
    )(*operands, *bufs)


def _retention_tables():
    cs = RET_BLOCK
    pos = np.arange(SEQ, dtype=np.float32)
    inv = (1.0 / (ROPE_BASE ** (np.arange(0, HEAD_DIM, 2, dtype=np.float32) / HEAD_DIM))).astype(np.float32)
    ang = jnp.asarray(pos[:, None] * inv[None, :])
    cos, sin = jnp.cos(ang), jnp.sin(ang)
    cfull = jnp.concatenate([cos, cos], axis=-1)
    ssign = jnp.concatenate([-sin, sin], axis=-1)
    kscale = HEAD_DIM ** -0.5
    gamma_log = np.log1p(-np.exp2(-5.0 - np.arange(RET_HEADS, dtype=np.float64)))
    idx = np.arange(cs)
    dist = np.abs(idx[:, None] - idx[None, :])
    visible = (idx[None, :] // CHUNK) <= (idx[:, None] // CHUNK)
    dmask = kscale * np.where(visible[None], np.exp(gamma_log[:, None, None] * dist[None]), 0.0)
    qdec = np.exp(gamma_log[:, None] * (idx + 1.0)[None, :])
    kdec = kscale * np.exp(gamma_log[:, None] * (cs - 1.0 - idx)[None, :])
    bcast = lambda a: np.broadcast_to(a[:, :, None], (RET_HEADS, cs, HEAD_DIM))
    decay = np.concatenate([dmask, bcast(qdec), bcast(kdec)], axis=-1)
    return (jnp.concatenate([cfull, ssign], axis=-1), jnp.asarray(decay, F32),
            jnp.asarray(np.exp(gamma_log * cs), F32))


def _block_diag(w):
    per = GATE_BLOCK // LRU_GROUP_DIM
    nb = LRU_GROUPS // per
    w4 = w.reshape(w.shape[0], nb, per, LRU_GROUP_DIM, LRU_GROUP_DIM)
    bd = jnp.einsum('lcipq,ij->lcipjq', w4, jnp.eye(per, dtype=w.dtype))
    return bd.reshape(w.shape[0], nb, GATE_BLOCK, GATE_BLOCK).astype(BF16)


def kernel(x, norm1_g, w_in, ret_gn_g, lru_conv_w, lru_conv_b, lru_wa, lru_ba, lru_wx, lru_bx,
           lru_lambda, lru_norm_g, w_out, norm2_g, ffn_w_gate, ffn_w_up, ffn_w_down, final_g):
    batch, seq, d = x.shape
    depth = w_in.shape[0]
    assert (seq, d) == (SEQ, D_MODEL) and batch % 2 == 0
    rows = batch * seq
    half = rows // 2
    half_tiles = half // TM
    vecs = lambda a: a.reshape(depth, 1, -1).astype(F32)
    rot, decay, sdec = _retention_tables()
    head_rows = jnp.stack([ret_gn_g.reshape(depth, RET_HEADS, HEAD_DIM).astype(F32),
                           jnp.broadcast_to(sdec[None, :, None], (depth, RET_HEADS, HEAD_DIM))], axis=2)
    head = jnp.pad(head_rows, ((0, 0), (0, 0), (0, SUBLANES - 2), (0, 0)))
    vec_rows = jnp.concatenate([lru_conv_w.astype(F32), vecs(lru_conv_b), vecs(lru_ba), vecs(lru_bx),
                                vecs(lru_lambda), vecs(lru_norm_g)], axis=1)
    vec = jnp.pad(vec_rows, ((0, 0), (0, VEC_ROWS - vec_rows.shape[1]), (0, 0)))
    gates = jnp.concatenate([_block_diag(lru_wa), _block_diag(lru_wx)], axis=1)
    mix_params = (rot, decay, head, vec, gates)
    g1, g2, fg = vecs(norm1_g), vecs(norm2_g), final_g.reshape(1, d).astype(F32)

    assert depth >= 2
    xf = x.reshape(rows, d)
    xa, xb = (xf, 0), (xf, half_tiles)
    p_a, out_buf = _in_proj(*xa, half, g1, w_in, 0, zero_rows=rows)
    ffn_weights, wo = (ffn_w_gate, ffn_w_up, ffn_w_down), w_out
    for l in range(depth):
        last = l == depth - 1
        ffn_w = ffn_weights + (fg, l, last)
        p_b, ret_a, lru_a = _in_proj(*xb, half, g1, w_in, l, mix=(p_a, mix_params, l))
        x1_a = _out_proj(*xa, ret_a, lru_a, wo, l)
        if last:
            x2_a, ret_b, lru_b = _ffn(x1_a, g2, *ffn_w, mix=(p_b, mix_params), out_rows=rows,
                                      out_buf=out_buf)
            x1_b = _out_proj(*xb, ret_b, lru_b, wo, l)
            return _ffn(x1_b, g2, *ffn_w, out_rows=rows, out_tile0=half_tiles,
                        out_buf=x2_a).reshape(batch, seq, d)
        x2_a, wg16, wu16 = _ffn(x1_a, g2, *ffn_w, cast=((ffn_w_gate, l + 1), (ffn_w_up, l + 1)))
        xa = (x2_a, 0)
        p_a, ret_b, lru_b = _in_proj(*xa, half, g1, w_in, l + 1, mix=(p_b, mix_params, l))
        x1_b = _out_proj(*xb, ret_b, lru_b, wo, l)
        x2_b, wd16, wo = _ffn(x1_b, g2, *ffn_w, cast=((ffn_w_down, l + 1), (w_out, l + 1)))
        xb = (x2_b, 0)
        ffn_weights = (wg16, wu16, wd16)
```

```python
import functools

import numpy as np
import jax
import jax.numpy as jnp
from jax import lax
from jax.experimental import pallas as pl
from jax.experimental.pallas import tpu as pltpu

D_MODEL = 2048
SEQ = 2048
RET_HEADS = 8
HEAD_DIM = 128
RET_WIDTH = RET_HEADS * HEAD_DIM
LRU_WIDTH = 1024
LRU_GROUPS = 16
LRU_GROUP_DIM = 64
CONV_W = 4
LRU_C = 8.0
IN_WIDTH = 4 * RET_WIDTH + 2 * LRU_WIDTH
CHUNK = 64
ROPE_BASE = 10000.0
EPS = 1e-6

F32 = jnp.float32
BF16 = jnp.bfloat16

SUBLANES = 8
BF16_ROWS = 16
TM = 1024
IN_TN = 768
IN_TN_PLAIN = 1024
FFN_TF = 256
FFN_TF_BF16 = 512
CAST_SLABS = 64
OUT_TN = 1024
MIX_STEPS = IN_WIDTH // IN_TN
LRU_BLOCK = TM // MIX_STEPS
RET_GROUPS = 2
RET_GROUP_HEADS = RET_HEADS // RET_GROUPS
RET_GROUP_WIDTH = RET_GROUP_HEADS * HEAD_DIM
RET_BLOCK = LRU_BLOCK * RET_GROUPS
GATE_BLOCK = 256
VMEM_LIMIT = 55 * 1024 * 1024
VMEM_LIMIT_FFN = 57 * 1024 * 1024
ZERO_SLABS_PER_TILE = 4


def _params(*sem, vmem=VMEM_LIMIT):
    return pltpu.CompilerParams(dimension_semantics=sem, vmem_limit_bytes=vmem)


def _sigmoid(z):
    return 0.5 * jnp.tanh(0.5 * z) + 0.5


def _rms_scale(x, g):
    ms = jnp.mean(x * x, axis=-1, keepdims=True)
    return x * lax.rsqrt(ms + EPS) * g


def _retention_head(h, hg, q_ref, k_ref, v_ref, g_ref, rot_ref, decay_ref, head_ref, o_ref, s_ref):
    cos, sin = rot_ref[:, :HEAD_DIM], rot_ref[:, HEAD_DIM:]
    half = HEAD_DIM // 2
    sl = slice(h * HEAD_DIM, (h + 1) * HEAD_DIM)
    q = q_ref[:, sl].astype(F32)
    k = k_ref[:, sl].astype(F32)
    v = v_ref[:, sl]
    qr = q * cos + pltpu.roll(q, half, 1) * sin
    kr = k * cos + pltpu.roll(k, half, 1) * sin
    qb = qr.astype(BF16)
    kb = kr.astype(BF16)
    sidx = hg * RET_GROUP_HEADS + h
    dmask = decay_ref[sidx, :, 0:RET_BLOCK]
    qdec = decay_ref[sidx, :, RET_BLOCK:RET_BLOCK + HEAD_DIM]
    kdec = decay_ref[sidx, :, RET_BLOCK + HEAD_DIM:RET_BLOCK + 2 * HEAD_DIM]
    gn, sdec = head_ref[sidx, 0:1, :], head_ref[sidx, 1:2, :]
    scores = lax.dot_general(qb, kb, (((1,), (1,)), ((), ())),
                             preferred_element_type=F32) * dmask
    o = jnp.dot(scores.astype(BF16), v, preferred_element_type=F32)
    state = s_ref[sidx]
    o = o + qdec * jnp.dot(qb, state.astype(BF16), preferred_element_type=F32)
    kd = (kr * kdec).astype(BF16)
    s_ref[sidx] = sdec * state + lax.dot_general(
        kd, v, (((0,), (0,)), ((), ())), preferred_element_type=F32)
    mu = jnp.mean(o, axis=-1, keepdims=True)
    oc = o - mu
    var = jnp.mean(oc * oc, axis=-1, keepdims=True)
    on = oc * lax.rsqrt(var + EPS) * gn
    gate = g_ref[:, sl].astype(F32)
    o_ref[:, sl] = (on * (gate * _sigmoid(gate))).astype(o_ref.dtype)


def _lru_cols(c, xy_ref, vec_ref, gates_ref, xpad_ref, a_ref, b_ref, y_ref, carry_ref):
    tt = xy_ref.shape[0]
    pad = SUBLANES
    sl = slice(c * GATE_BLOCK, (c + 1) * GATE_BLOCK)
    ysl = slice(LRU_WIDTH + c * GATE_BLOCK, LRU_WIDTH + (c + 1) * GATE_BLOCK)
    vrow = lambda r: vec_ref[r:r + 1, sl]
    x = xy_ref[:, sl].astype(F32)
    xpad_ref[pad:pad + tt, sl] = x
    xc = vrow(VEC_CONV_B) + vrow(VEC_CONV_W + CONV_W - 1) * x
    for j in range(CONV_W - 1):
        off = pad - (CONV_W - 1) + j
        xc = xc + vrow(VEC_CONV_W + j) * xpad_ref[off:off + tt, sl]
    xpad_ref[0:pad, sl] = x[tt - pad:tt, :]

    lam = vrow(VEC_LAMBDA)
    softplus_neg_lam = jnp.maximum(-lam, 0.0) + jnp.log1p(jnp.exp(-jnp.abs(lam)))
    xc16 = xc.astype(BF16)
    n_blocks = LRU_WIDTH // GATE_BLOCK
    r = _sigmoid(jnp.dot(xc16, gates_ref[c], preferred_element_type=F32) + vrow(VEC_BA))
    i = _sigmoid(jnp.dot(xc16, gates_ref[n_blocks + c], preferred_element_type=F32) + vrow(VEC_BX))
    neg_log_a = LRU_C * r * softplus_neg_lam
    a = jnp.exp(-neg_log_a)
    a_ref[:, sl] = a
    one_minus_a2 = jnp.tanh(neg_log_a) * (a * a + 1.0)
    root = jnp.where(one_minus_a2 > 0.0, one_minus_a2 * lax.rsqrt(one_minus_a2), 0.0)
    b_ref[:, sl] = root * (i * xc)

    row = lax.broadcasted_iota(jnp.int32, (SUBLANES, GATE_BLOCK), 0)
    carry = carry_ref[:, sl]
    for gidx in range(tt // SUBLANES):
        rows = slice(gidx * SUBLANES, (gidx + 1) * SUBLANES)
        a = a_ref[rows, sl]
        b = b_ref[rows, sl]
        for s in (1, 2, 4):
            keep = row >= s
            b = jnp.where(keep, a * pltpu.roll(b, s, 0) + b, b)
            a = jnp.where(keep, a * pltpu.roll(a, s, 0), a)
        h = a * carry + b
        b_ref[rows, sl] = h
        carry = h[SUBLANES - 1:SUBLANES, :]
    carry_ref[:, sl] = carry

    y = b_ref[:, sl] * jax.nn.gelu(xy_ref[:, ysl].astype(F32))
    y_ref[:, sl] = y
    return y * y


def _lru_finish(ysq, vec_ref, o_ref, y_ref):
    ms = jnp.sum(ysq, axis=-1, keepdims=True) * (1.0 / LRU_WIDTH)
    ng = vec_ref[VEC_NORM_G:VEC_NORM_G + 1, :]
    o_ref[...] = (y_ref[...] * lax.rsqrt(ms + EPS) * ng).astype(o_ref.dtype)


VEC_CONV_W, VEC_CONV_B, VEC_BA, VEC_BX, VEC_LAMBDA, VEC_NORM_G = 0, CONV_W, CONV_W + 1, CONV_W + 2, CONV_W + 3, CONV_W + 4
VEC_ROWS = 16
N_MIX_IN = 10
N_MIX_OUT = 2
N_MIX_SCRATCH = 6
MIX_PIECES = LRU_WIDTH // GATE_BLOCK
assert MIX_PIECES == RET_GROUP_HEADS


def _row_chunks(n):
    step = TM // n // (2 * BF16_ROWS) * (2 * BF16_ROWS)
    return [slice(c * step, (c + 1) * step if c < n - 1 else TM) for c in range(n)]


MIX_CHUNKS = _row_chunks(MIX_PIECES + 1)
PIECES_AFTER = [[c] for c in range(MIX_PIECES)] + [[]]
NORM_CHUNKS = _row_chunks(4)
NO_PIECES = [[]] * len(NORM_CHUNKS)


def _mixer_resets(s, active, mix_scratch):
    s_ref, xpad_ref, _, _, _, carry_ref = mix_scratch
    lru_blocks = SEQ // LRU_BLOCK
    hg = s % RET_GROUPS

    @pl.when(jnp.logical_and(active, s % lru_blocks == 0))
    def _():
        xpad_ref[0:SUBLANES, :] = jnp.zeros((SUBLANES, LRU_WIDTH), F32)
        carry_ref[...] = jnp.zeros_like(carry_ref)

    @pl.when(jnp.logical_and(active, (s // RET_GROUPS) % (SEQ // RET_BLOCK) == 0))
    def _():
        s_ref[pl.ds(hg * RET_GROUP_HEADS, RET_GROUP_HEADS)] = jnp.zeros(
            (RET_GROUP_HEADS, HEAD_DIM, HEAD_DIM), F32)


def _mixer_piece(c, s, ysq, mix_in, mix_out, mix_scratch):
    q, k, v, g, xy, rot, decay, head, vec, gates = mix_in
    o_ret, o_lru = mix_out
    s_ref, xpad_ref, a_ref, b_ref, y_ref, carry_ref = mix_scratch
    _retention_head(c, s % RET_GROUPS, q, k, v, g, rot, decay, head, o_ret, s_ref)
    ysq_c = _lru_cols(c, xy, vec, gates, xpad_ref, a_ref, b_ref, y_ref, carry_ref)
    ysq = ysq_c if ysq is None else ysq + ysq_c
    if c == MIX_PIECES - 1:
        _lru_finish(ysq, vec, o_lru, y_ref)
    return ysq


def _mixer_specs(smap, layer):
    hg = lambda i, j: smap(i, j) % RET_GROUPS
    rblk = lambda i, j: smap(i, j) // RET_GROUPS
    tblk = lambda i, j: rblk(i, j) % (SEQ // RET_BLOCK)
    pcol = lambda c: pl.BlockSpec((RET_BLOCK, RET_GROUP_WIDTH),
                                  lambda i, j, c=c: (rblk(i, j), c * RET_GROUPS + hg(i, j)))
    resident = dict(pipeline_mode=pl.Buffered(1))
    in_specs = [pcol(0), pcol(1), pcol(2), pcol(3),
                pl.BlockSpec((LRU_BLOCK, 2 * LRU_WIDTH),
                             lambda i, j: (smap(i, j), 4 * RET_WIDTH // (2 * LRU_WIDTH))),
                pl.BlockSpec((RET_BLOCK, 2 * HEAD_DIM), lambda i, j: (tblk(i, j), 0)),
                pl.BlockSpec((RET_HEADS, RET_BLOCK, RET_BLOCK + 2 * HEAD_DIM),
                             lambda i, j: (0, 0, 0), **resident),
                pl.BlockSpec((None, RET_HEADS, SUBLANES, HEAD_DIM),
                             lambda i, j: (layer, 0, 0, 0), **resident),
                pl.BlockSpec((None, VEC_ROWS, LRU_WIDTH), lambda i, j: (layer, 0, 0), **resident),
                pl.BlockSpec((None, 2 * LRU_WIDTH // GATE_BLOCK, GATE_BLOCK, GATE_BLOCK),
                             lambda i, j: (layer, 0, 0, 0), **resident)]
    out_specs = [pl.BlockSpec((RET_BLOCK, RET_GROUP_WIDTH), lambda i, j: (rblk(i, j), hg(i, j))),
                 pl.BlockSpec((LRU_BLOCK, LRU_WIDTH), lambda i, j: (smap(i, j), 0))]
    scratch = [pltpu.VMEM((RET_HEADS, HEAD_DIM, HEAD_DIM), F32),
               pltpu.VMEM((LRU_BLOCK + SUBLANES, LRU_WIDTH), F32),
               pltpu.VMEM((LRU_BLOCK, LRU_WIDTH), F32),
               pltpu.VMEM((LRU_BLOCK, LRU_WIDTH), F32),
               pltpu.VMEM((LRU_BLOCK, LRU_WIDTH), F32),
               pltpu.VMEM((1, LRU_WIDTH), F32)]
    return in_specs, out_specs, scratch


def _mixer_out_shapes(rows):
    return [jax.ShapeDtypeStruct((rows, RET_WIDTH), BF16),
            jax.ShapeDtypeStruct((rows, LRU_WIDTH), BF16)]


def _mixer_operands(p, mix):
    return (p,) * 5 + tuple(mix)


def _in_proj_chunk(rows, wb, o_ref, h_ref):
    o_ref[rows, :] = jnp.dot(h_ref[rows, :], wb, preferred_element_type=F32).astype(o_ref.dtype)


def _in_proj_rows(chunks, after, w_ref, o_ref, h_ref, norm=None, piece=None):
    wb = w_ref[...].astype(BF16)
    for rows, pieces in zip(chunks, after):
        if norm is not None:
            x_ref, g_ref = norm
            h_ref[rows, :] = _rms_scale(x_ref[rows, :], g_ref[...]).astype(BF16)
        _in_proj_chunk(rows, wb, o_ref, h_ref)
        for c in pieces:
            piece(c)


def _in_proj_kernel(x_ref, g_ref, w_ref, o_ref, *rest):
    h_ref = rest[-1]
    first = pl.program_id(1) == 0

    def zero_fill():
        for z_ref in rest[:-1]:
            z_ref[...] = jnp.zeros_like(z_ref)

    @pl.when(first)
    def _():
        _in_proj_rows(NORM_CHUNKS, NO_PIECES, w_ref, o_ref, h_ref, norm=(x_ref, g_ref))
        zero_fill()

    @pl.when(jnp.logical_not(first))
    def _():
        _in_proj_rows([slice(0, TM)], [[]], w_ref, o_ref, h_ref)
        zero_fill()


def _in_proj_mix_kernel(*refs):
    x_ref, g_ref, w_ref = refs[:3]
    mix_in = refs[3:3 + N_MIX_IN]
    o_ref = refs[3 + N_MIX_IN]
    mix_out = refs[4 + N_MIX_IN:4 + N_MIX_IN + N_MIX_OUT]
    h_ref = refs[-N_MIX_SCRATCH - 1]
    mix_scratch = refs[-N_MIX_SCRATCH:]
    s = pl.program_id(0) * MIX_STEPS + pl.program_id(1)
    _mixer_resets(s, True, mix_scratch)

    @pl.when(pl.program_id(1) == 0)
    def _():
        h_ref[...] = _rms_scale(x_ref[...], g_ref[...]).astype(BF16)

    ysq = [None]

    def piece(c):
        ysq[0] = _mixer_piece(c, s, ysq[0], mix_in, mix_out, mix_scratch)

    _in_proj_rows(MIX_CHUNKS, PIECES_AFTER, w_ref, o_ref, h_ref, piece=piece)


def _in_proj_specs(x_tile0, layer, tn):
    d = D_MODEL
    in_specs = [pl.BlockSpec((TM, d), lambda i, j: (x_tile0 + i, 0)),
                pl.BlockSpec((None, 1, d), lambda i, j: (layer, 0, 0)),
                pl.BlockSpec((None, d, tn), lambda i, j: (layer, 0, j))]
    out_spec = pl.BlockSpec((TM, tn), lambda i, j: (i, j))
    return in_specs, out_spec


def _in_proj(x, x_tile0, rows, g, w, layer, mix=None, zero_rows=None):
    tn = IN_TN if mix is not None else IN_TN_PLAIN
    in_specs, out_spec = _in_proj_specs(x_tile0, layer, tn)
    grid = (rows // TM, IN_WIDTH // tn)
    p_shape = jax.ShapeDtypeStruct((rows, IN_WIDTH), BF16)
    h_scratch = pltpu.VMEM((TM, D_MODEL), BF16)
    if mix is None:
        out_specs, out_shapes = [out_spec], [p_shape]
        if zero_rows is not None:
            per_tile = ZERO_SLABS_PER_TILE
            slab = zero_rows // (grid[0] * per_tile)
            out_specs.append(pl.BlockSpec(
                (slab, D_MODEL), lambda i, j: (i * per_tile + jnp.minimum(j, per_tile - 1), 0)))
            out_shapes.append(jax.ShapeDtypeStruct((zero_rows, D_MODEL), F32))
        outs = pl.pallas_call(
            _in_proj_kernel, grid=grid, in_specs=in_specs, out_specs=out_specs,
            out_shape=out_shapes, scratch_shapes=[h_scratch],
            compiler_params=_params("parallel", "arbitrary"), name="in_proj",
        )(x, g, w)
        return outs[0] if zero_rows is None else outs
    p_other, mix_params, mix_layer = mix
    m_in, m_out, m_scratch = _mixer_specs(lambda i, j: i * MIX_STEPS + j, mix_layer)
    return pl.pallas_call(
        _in_proj_mix_kernel, grid=grid,
        in_specs=in_specs + m_in, out_specs=[out_spec] + m_out,
        out_shape=[p_shape] + _mixer_out_shapes(p_other.shape[0]),
        scratch_shapes=[h_scratch] + m_scratch,
        compiler_params=_params("arbitrary", "arbitrary"), name="in_proj_mix",
    )(x, g, w, *_mixer_operands(p_other, mix_params))


def _out_proj_kernel(x_ref, r_ref, l_ref, w_ref, o_ref):
    acc = jnp.dot(r_ref[...], w_ref[0:RET_WIDTH, :].astype(BF16), preferred_element_type=F32)
    acc = acc + jnp.dot(l_ref[...], w_ref[RET_WIDTH:, :].astype(BF16), preferred_element_type=F32)
    o_ref[...] = x_ref[...] + acc


def _out_proj(x, x_tile0, o_ret, o_lru, w, layer):
    rows = o_ret.shape[0]
    d = D_MODEL
    return pl.pallas_call(
        _out_proj_kernel,
        grid=(d // OUT_TN, rows // TM),
        in_specs=[pl.BlockSpec((TM, OUT_TN), lambda j, i: (x_tile0 + i, j)),
                  pl.BlockSpec((TM, RET_WIDTH), lambda j, i: (i, 0)),
                  pl.BlockSpec((TM, LRU_WIDTH), lambda j, i: (i, 0)),
                  pl.BlockSpec((None, d, OUT_TN), lambda j, i: (layer, 0, j))],
        out_specs=pl.BlockSpec((TM, OUT_TN), lambda j, i: (i, j)),
        out_shape=jax.ShapeDtypeStruct((rows, d), F32),
        compiler_params=_params("parallel", "arbitrary"),
        name="out_proj",
    )(x, o_ret, o_lru, w)


def _ffn_prefetch_x(x_hbm, o_ref, xnext_ref, sem):
    i, j = pl.program_id(0), pl.program_id(1)

    def ring_copy(tile):
        row0 = pl.multiple_of(tile * TM, TM)
        return pltpu.make_async_copy(x_hbm.at[pl.ds(row0, TM), :], xnext_ref, sem)

    @pl.when(jnp.logical_and(j == 0, i == 0))
    def _():
        pltpu.sync_copy(x_hbm.at[pl.ds(0, TM), :], o_ref)

    @pl.when(jnp.logical_and(j == 0, i > 0))
    def _():
        ring_copy(i).wait()
        o_ref[...] = xnext_ref[...]

    @pl.when(jnp.logical_and(j == MIX_STEPS, i + 1 < pl.num_programs(0)))
    def _():
        ring_copy(i + 1).start()


def _ffn_load_x(x_hbm, o_ref):
    @pl.when(pl.program_id(1) == 0)
    def _():
        row0 = pl.multiple_of(pl.program_id(0) * TM, TM)
        pltpu.sync_copy(x_hbm.at[pl.ds(row0, TM), :], o_ref)


def _ffn_gate_up(rows, wg, wu, h_ref):
    h = h_ref[rows, :]
    return (jnp.dot(h, wg, preferred_element_type=F32), jnp.dot(h, wu, preferred_element_type=F32))


def _ffn_down(rows, gate_up, wd, o_ref):
    gate, up = gate_up
    mid = (gate * _sigmoid(gate) * up).astype(BF16)
    o_ref[rows, :] += jnp.dot(mid, wd, preferred_element_type=F32)


def _ffn_rows(chunks, after, w_refs, o_ref, h_ref, norm=None, piece=None):
    wg, wu, wd = _ffn_weights(*w_refs)
    pending = None
    for rows, pieces in zip(chunks, after):
        if norm is not None:
            x_ref, g_ref = norm
            if x_ref is not None:
                o_ref[rows, :] = x_ref[rows, :]
            h_ref[rows, :] = _rms_scale(o_ref[rows, :], g_ref[...]).astype(BF16)
        gate_up = _ffn_gate_up(rows, wg, wu, h_ref)
        if pending is not None:
            _ffn_down(*pending, wd, o_ref)
        pending = (rows, gate_up)
        for c in pieces:
            piece(c)
    _ffn_down(*pending, wd, o_ref)


def _ffn_weights(wg_ref, wu_ref, wd_ref):
    return wg_ref[...].astype(BF16), wu_ref[...].astype(BF16), wd_ref[...].astype(BF16)


def _ffn_epilogue(fg_ref, o_ref, final_norm):
    if final_norm:
        @pl.when(pl.program_id(1) == pl.num_programs(1) - 1)
        def _():
            o_ref[...] = _rms_scale(o_ref[...], fg_ref[...])


def _ffn_kernel(x_ref, g_ref, wg_ref, wu_ref, wd_ref, fg_ref, *rest, final_norm, n_buf, n_cast,
                x_in_hbm):
    cast_in = rest[n_buf:n_buf + n_cast]
    o_ref = rest[n_buf + n_cast]
    cast_out = rest[n_buf + n_cast + 1:n_buf + 2 * n_cast + 1]
    h_ref = rest[-1]
    if x_in_hbm:
        _ffn_load_x(x_ref, o_ref)
    weights = (wg_ref, wu_ref, wd_ref)
    first = pl.program_id(1) == 0

    def cast_slabs():
        for src, dst in zip(cast_in, cast_out):
            dst[...] = src[...].astype(dst.dtype)

    @pl.when(first)
    def _():
        _ffn_rows(NORM_CHUNKS, NO_PIECES, weights, o_ref, h_ref,
                  norm=(None if x_in_hbm else x_ref, g_ref))
        cast_slabs()

    @pl.when(jnp.logical_not(first))
    def _():
        _ffn_rows([slice(0, TM)], [[]], weights, o_ref, h_ref)
        cast_slabs()

    _ffn_epilogue(fg_ref, o_ref, final_norm)


def _ffn_mix_kernel(*refs, final_norm, n_buf):
    x_hbm, g_ref, wg_ref, wu_ref, wd_ref, fg_ref = refs[:6]
    mix_in = refs[6:6 + N_MIX_IN]
    n_in = 6 + N_MIX_IN + n_buf
    o_ref = refs[n_in]
    mix_out = refs[n_in + 1:n_in + 1 + N_MIX_OUT]
    h_ref = refs[-N_MIX_SCRATCH - 1]
    mix_scratch = refs[-N_MIX_SCRATCH:]
    j = pl.program_id(1)
    active = j < MIX_STEPS
    s = pl.program_id(0) * MIX_STEPS + jnp.minimum(j, MIX_STEPS - 1)
    _ffn_prefetch_x(x_hbm, o_ref, refs[-N_MIX_SCRATCH - 3], refs[-N_MIX_SCRATCH - 2])
    _mixer_resets(s, active, mix_scratch)
    weights = (wg_ref, wu_ref, wd_ref)

    @pl.when(j == 0)
    def _():
        h_ref[...] = _rms_scale(o_ref[...], g_ref[...]).astype(BF16)

    @pl.when(active)
    def _():
        ysq = [None]

        def piece(c):
            ysq[0] = _mixer_piece(c, s, ysq[0], mix_in, mix_out, mix_scratch)

        _ffn_rows(MIX_CHUNKS, PIECES_AFTER, weights, o_ref, h_ref, piece=piece)

    @pl.when(jnp.logical_not(active))
    def _():
        _ffn_rows([slice(0, TM)], [[]], weights, o_ref, h_ref)

    _ffn_epilogue(fg_ref, o_ref, final_norm)


def _ffn(x, g, wg, wu, wd, fg, layer, final_norm, mix=None, out_rows=None, out_tile0=0,
         out_buf=None, cast=()):
    rows, d = x.shape
    out_rows = rows if out_rows is None else out_rows
    stacked = wg.ndim == 3
    tf = FFN_TF if stacked else FFN_TF_BF16
    grid = (rows // TM, wg.shape[-1] // tf)
    steps = grid[0] * grid[1]

    def wspec(block, idx):
        if stacked:
            return pl.BlockSpec((None,) + block, lambda i, j: (layer,) + idx(i, j))
        return pl.BlockSpec(block, idx)

    x_in_hbm = mix is not None
    x_spec = pl.BlockSpec(memory_space=pl.ANY) if x_in_hbm else pl.BlockSpec((TM, d), lambda i, j: (i, 0))
    in_specs = [x_spec,
                pl.BlockSpec((None, 1, d), lambda i, j: (layer, 0, 0)),
                wspec((d, tf), lambda i, j: (0, j)),
                wspec((d, tf), lambda i, j: (0, j)),
                wspec((tf, d), lambda i, j: (j, 0)),
                pl.BlockSpec((1, d), lambda i, j: (0, 0))]
    cast_specs, cast_out_specs, cast_shapes = [], [], []
    for w, w_layer in cast:
        slab = w.shape[1] // steps if w.shape[1] % steps == 0 else w.shape[1] // CAST_SLABS
        n_slabs = w.shape[1] // slab
        idx = lambda i, j, n=n_slabs: jnp.minimum(i * grid[1] + j, n - 1)
        cast_specs.append(pl.BlockSpec((None, slab, w.shape[2]),
                                       lambda i, j, idx=idx, wl=w_layer: (wl, idx(i, j), 0)))
        cast_out_specs.append(pl.BlockSpec((slab, w.shape[2]), lambda i, j, idx=idx: (idx(i, j), 0)))
        cast_shapes.append(jax.ShapeDtypeStruct(w.shape[1:], BF16))
    out_spec = pl.BlockSpec((TM, d), lambda i, j: (out_tile0 + i, 0))
    out_shape = jax.ShapeDtypeStruct((out_rows, d), F32)
    h_scratch = pltpu.VMEM((TM, d), BF16)
    operands = (x, g, wg, wu, wd, fg)
    buf_specs, bufs = ([], ()) if out_buf is None else ([pl.BlockSpec(memory_space=pl.ANY)], (out_buf,))
    if mix is None:
        outs = pl.pallas_call(
            functools.partial(_ffn_kernel, final_norm=final_norm, n_buf=len(bufs), n_cast=len(cast),
                              x_in_hbm=x_in_hbm),
            grid=grid, in_specs=in_specs + buf_specs + cast_specs,
            out_specs=[out_spec] + cast_out_specs, out_shape=[out_shape] + cast_shapes,
            scratch_shapes=[h_scratch],
            input_output_aliases={len(operands): 0} if bufs else {},
            compiler_params=_params("parallel", "arbitrary", vmem=VMEM_LIMIT_FFN),
            name="ffn_final" if final_norm else "ffn",
        )(*operands, *bufs, *(w for w, _ in cast))
        return outs if cast else outs[0]
    p_other, mix_params = mix
    m_in, m_out, m_scratch = _mixer_specs(
        lambda i, j: i * MIX_STEPS + jnp.minimum(j, MIX_STEPS - 1), layer)
    operands = operands + _mixer_operands(p_other, mix_params)
    return pl.pallas_call(
        functools.partial(_ffn_mix_kernel, final_norm=final_norm, n_buf=len(bufs)),
        grid=grid, in_specs=in_specs + m_in + buf_specs, out_specs=[out_spec] + m_out,
        out_shape=[out_shape] + _mixer_out_shapes(p_other.shape[0]),
        scratch_shapes=[pltpu.VMEM((TM, d), F32), pltpu.SemaphoreType.DMA(()), h_scratch] + m_scratch,
        input_output_aliases={len(operands): 0} if bufs else {},
        compiler_params=_params("arbitrary", "arbitrary", vmem=VMEM_LIMIT_FFN),
        name="ffn_mix_final" if final_norm else "ffn_mix",
    )(*operands, *bufs)


def _retention_tables():
    cs = RET_BLOCK
    pos = np.arange(SEQ, dtype=np.float32)
    inv = (1.0 / (ROPE_BASE ** (np.arange(0, HEAD_DIM, 2, dtype=np.float32) / HEAD_DIM))).astype(np.float32)
    ang = jnp.asarray(pos[:, None] * inv[None, :])
    cos, sin = jnp.cos(ang), jnp.sin(ang)
    cfull = jnp.concatenate([cos, cos], axis=-1)
    ssign = jnp.concatenate([-sin, sin], axis=-1)
    kscale = HEAD_DIM ** -0.5
    gamma_log = np.log1p(-np.exp2(-5.0 - np.arange(RET_HEADS, dtype=np.float64)))
    idx = np.arange(cs)
    dist = np.abs(idx[:, None] - idx[None, :])
    visible = (idx[None, :] // CHUNK) <= (idx[:, None] // CHUNK)
    dmask = kscale * np.where(visible[None], np.exp(gamma_log[:, None, None] * dist[None]), 0.0)
    qdec = np.exp(gamma_log[:, None] * (idx + 1.0)[None, :])
    kdec = kscale * np.exp(gamma_log[:, None] * (cs - 1.0 - idx)[None, :])
    bcast = lambda a: np.broadcast_to(a[:, :, None], (RET_HEADS, cs, HEAD_DIM))
    decay = np.concatenate([dmask, bcast(qdec), bcast(kdec)], axis=-1)
    return (jnp.concatenate([cfull, ssign], axis=-1), jnp.asarray(decay, F32),
            jnp.asarray(np.exp(gamma_log * cs), F32))


def _block_diag(w):
    per = GATE_BLOCK // LRU_GROUP_DIM
    nb = LRU_GROUPS // per
    w4 = w.reshape(w.shape[0], nb, per, LRU_GROUP_DIM, LRU_GROUP_DIM)
    bd = jnp.einsum('lcipq,ij->lcipjq', w4, jnp.eye(per, dtype=w.dtype))
    return bd.reshape(w.shape[0], nb, GATE_BLOCK, GATE_BLOCK).astype(BF16)


def kernel(x, norm1_g, w_in, ret_gn_g, lru_conv_w, lru_conv_b, lru_wa, lru_ba, lru_wx, lru_bx,
           lru_lambda, lru_norm_g, w_out, norm2_g, ffn_w_gate, ffn_w_up, ffn_w_down, final_g):
    batch, seq, d = x.shape
    depth = w_in.shape[0]
    assert (seq, d) == (SEQ, D_MODEL) and batch % 2 == 0
    rows = batch * seq
    half = rows // 2
    half_tiles = half // TM
    vecs = lambda a: a.reshape(depth, 1, -1).astype(F32)
    rot, decay, sdec = _retention_tables()
    head_rows = jnp.stack([ret_gn_g.reshape(depth, RET_HEADS, HEAD_DIM).astype(F32),
                           jnp.broadcast_to(sdec[None, :, None], (depth, RET_HEADS, HEAD_DIM))], axis=2)
    head = jnp.pad(head_rows, ((0, 0), (0, 0), (0, SUBLANES - 2), (0, 0)))
    vec_rows = jnp.concatenate([lru_conv_w.astype(F32), vecs(lru_conv_b), vecs(lru_ba), vecs(lru_bx),
                                vecs(lru_lambda), vecs(lru_norm_g)], axis=1)
    vec = jnp.pad(vec_rows, ((0, 0), (0, VEC_ROWS - vec_rows.shape[1]), (0, 0)))
    gates = jnp.concatenate([_block_diag(lru_wa), _block_diag(lru_wx)], axis=1)
    mix_params = (rot, decay, head, vec, gates)
    g1, g2, fg = vecs(norm1_g), vecs(norm2_g), final_g.reshape(1, d).astype(F32)

    assert depth >= 2
    xf = x.reshape(rows, d)
    xa, xb = (xf, 0), (xf, half_tiles)
    p_a, out_buf = _in_proj(*xa, half, g1, w_in, 0, zero_rows=rows)
    ffn_weights = (ffn_w_gate, ffn_w_up, ffn_w_down)
    for l in range(depth):
        last = l == depth - 1
        ffn_w = ffn_weights + (fg, l, last)
        p_b, ret_a, lru_a = _in_proj(*xb, half, g1, w_in, l, mix=(p_a, mix_params, l))
        x1_a = _out_proj(*xa, ret_a, lru_a, w_out, l)
        if last:
            x2_a, ret_b, lru_b = _ffn(x1_a, g2, *ffn_w, mix=(p_b, mix_params), out_rows=rows,
                                      out_buf=out_buf)
            x1_b = _out_proj(*xb, ret_b, lru_b, w_out, l)
            return _ffn(x1_b, g2, *ffn_w, out_rows=rows, out_tile0=half_tiles,
                        out_buf=x2_a).reshape(batch, seq, d)
        x2_a, wg16, wu16 = _ffn(x1_a, g2, *ffn_w, cast=((ffn_w_gate, l + 1), (ffn_w_up, l + 1)))
        xa = (x2_a, 0)
        p_a, ret_b, lru_b = _in_proj(*xa, half, g1, w_in, l + 1, mix=(p_b, mix_params, l))
        x1_b = _out_proj(*xb, ret_b, lru_b, w_out, l)
        x2_b, wd16 = _ffn(x1_b, g2, *ffn_w, cast=((ffn_w_down, l + 1),))
        xb = (x2_b, 0)
        ffn_weights = (wg16, wu16, wd16)
```
